```python
import math
import jax
import jax.numpy as jnp
from jax import lax
import numpy as np

D_MODEL = 4096
BATCH = 2
SEQ = 4096
DEPTH = 4

N_MIXERS = 4
EPS = 1e-6
NEG = -1e30
FORCE = 1e4
CONV_W = 4

A_HEADS = 8
A_DQK = D_MODEL // 16
A_DV = D_MODEL // 8
A_QK = A_HEADS * A_DQK
A_INNER = A_HEADS * A_DV
A_CHUNK = 128
A_COLS = 2 * A_QK + 3 * A_INNER + 2 * A_HEADS

B_HD = 128
B_HEADS = D_MODEL // B_HD
B_KV = 4
B_REP = B_HEADS // B_KV
B_INNER = B_HEADS * B_HD
B_KVW = B_KV * B_HD
B_CMP_LEN = 32
B_CMP_STRIDE = 16
B_SEL_LEN = 64
B_N_SEL = 16
B_WIN = 512
B_QBLK = 64
B_COLS = 2 * B_INNER + 6 * B_KVW + 3 * B_HEADS

C_WIDTH = (4 * D_MODEL // 3) // 256 * 256
C_BLOCKS = 16
C_BLK = C_WIDTH // C_BLOCKS
C_EXP = 8.0
C_COLS = 2 * C_WIDTH

D_HD = 128
D_SLOTS = D_MODEL // 256
D_INNER = D_SLOTS * D_HD
D_PATTERNS = ((128, 1), (512, 4), (2048, 16))
D_QBLK = 128
D_COLS = 3 * len(D_PATTERNS) * D_INNER + D_INNER

N_A = (DEPTH + N_MIXERS - 1) // N_MIXERS
N_B = (DEPTH + N_MIXERS - 2) // N_MIXERS
N_C = (DEPTH + N_MIXERS - 3) // N_MIXERS
N_D = (DEPTH + N_MIXERS - 4) // N_MIXERS

kernel_name = "hybrid_mlstm_nsa_rglru_dilated"


def rms_norm(x, w):
    xf = x.astype(jnp.float32)
    y = xf * lax.rsqrt(jnp.mean(xf * xf, axis=-1, keepdims=True) + EPS)
    return (y * w.astype(jnp.float32)).astype(x.dtype)


def causal_dwconv(x, w, b):
    ch = x.shape[-1]
    y = lax.conv_general_dilated(x, w[:, None, :].astype(x.dtype), window_strides=(1,),
                                 padding=[(CONV_W - 1, 0)],
                                 dimension_numbers=("NWC", "WIO", "NWC"),
                                 feature_group_count=ch)
    return y + b.astype(x.dtype)


def alibi_slopes(n):
    return jnp.asarray(2.0 ** (-8.0 * np.arange(1, n + 1) / n), dtype=jnp.float32)


def masked_softmax(s, mask):
    s = jnp.where(mask, s, NEG)
    m = jnp.max(s, axis=-1, keepdims=True)
    p = jnp.where(mask, jnp.exp(s - m), 0.0)
    den = jnp.sum(p, axis=-1, keepdims=True)
    return p / jnp.where(den > 0, den, 1.0)


def mlstm_mixer(h, w_in, conv_w, conv_b, gate_b, out_norm_w, w_out):
    bsz, seq, _ = h.shape
    n_chunks = seq // A_CHUNK
    u = h @ w_in
    o0 = 2 * A_QK
    o1 = o0 + A_INNER
    o2 = o1 + A_INNER
    o3 = o2 + A_INNER
    o4 = o3 + A_HEADS
    qk, v, og, z, ig, fg = jnp.split(u, [o0, o1, o2, o3, o4], axis=-1)
    qk = jax.nn.silu(causal_dwconv(qk, conv_w, conv_b)).astype(jnp.float32)
    q = qk[..., :A_QK].reshape(bsz, seq, A_HEADS, A_DQK) * (A_DQK ** -0.5)
    k = qk[..., A_QK:].reshape(bsz, seq, A_HEADS, A_DQK)
    v = v.astype(jnp.float32).reshape(bsz, seq, A_HEADS, A_DV)
    log_i = ig.astype(jnp.float32) + gate_b[0].astype(jnp.float32)
    log_f = jax.nn.log_sigmoid(fg.astype(jnp.float32) + gate_b[1].astype(jnp.float32))

    def to_chunks(t):
        t = t.reshape((bsz, n_chunks, A_CHUNK) + t.shape[2:])
        return jnp.swapaxes(jnp.moveaxis(t, 1, 0), 2, 3)

    causal = jnp.tril(jnp.ones((A_CHUNK, A_CHUNK), dtype=bool))

    def step(carry, xs):
        c_st, n_st, m_st = carry
        qc, kc, vc, li, lf = xs
        b = jnp.cumsum(lf, axis=-1)
        dmat = b[..., :, None] - b[..., None, :] + li[..., None, :]
        dmat = jnp.where(causal, dmat, NEG)
        inter = b + m_st[..., None]
        m_t = jnp.maximum(inter, jnp.max(dmat, axis=-1))
        w_intra = jnp.exp(dmat - m_t[..., None])
        w_inter = jnp.exp(inter - m_t)
        a = w_intra * jnp.einsum("bhtd,bhsd->bhts", qc, kc)
        num = (jnp.einsum("bhts,bhsv->bhtv", a, vc)
               + w_inter[..., None] * jnp.einsum("bhtd,bhdv->bhtv", qc, c_st))
        qn = jnp.sum(a, axis=-1) + w_inter * jnp.einsum("bhtd,bhd->bht", qc, n_st)
        hc = num / jnp.maximum(jnp.abs(qn), jnp.exp(-m_t))[..., None]
        b_last = b[..., -1]
        g = b_last[..., None] - b + li
        m_new = jnp.maximum(b_last + m_st, jnp.max(g, axis=-1))
        ws = jnp.exp(g - m_new[..., None])
        decay = jnp.exp(b_last + m_st - m_new)
        c_new = decay[..., None, None] * c_st + jnp.einsum("bhs,bhsd,bhsv->bhdv", ws, kc, vc)
        n_new = decay[..., None] * n_st + jnp.einsum("bhs,bhsd->bhd", ws, kc)
        return (c_new, n_new, m_new), hc

    init = (jnp.zeros((bsz, A_HEADS, A_DQK, A_DV), jnp.float32),
            jnp.zeros((bsz, A_HEADS, A_DQK), jnp.float32),
            jnp.zeros((bsz, A_HEADS), jnp.float32))
    xs = (to_chunks(q), to_chunks(k), to_chunks(v), to_chunks(log_i), to_chunks(log_f))
    _, hc = lax.scan(step, init, xs)
    hc = jnp.swapaxes(jnp.moveaxis(hc, 0, 1), 2, 3).reshape(bsz, seq, A_HEADS, A_DV)
    hn = hc * lax.rsqrt(jnp.mean(hc * hc, axis=-1, keepdims=True) + EPS)
    hn = hn.reshape(bsz, seq, A_INNER) * out_norm_w.astype(jnp.float32)
    y = hn * jax.nn.sigmoid(og.astype(jnp.float32)) * jax.nn.silu(z.astype(jnp.float32))
    return y.astype(h.dtype) @ w_out


def nsa_mixer(h, w_in, cmp_pe, cmp_wk, cmp_wv, q_norm_w, k_norm_w, w_out):
    bsz, seq, _ = h.shape
    u = h @ w_in
    q, z, kv, gates = jnp.split(u, [B_INNER, 2 * B_INNER, 2 * B_INNER + 6 * B_KVW], axis=-1)
    q = rms_norm(q.reshape(bsz, seq, B_HEADS, B_HD), q_norm_w)
    kv = kv.reshape(bsz, seq, 3, 2, B_KV, B_HD)
    gates = jax.nn.sigmoid(gates.astype(jnp.float32)).reshape(bsz, seq, B_HEADS, 3)
    scale = B_HD ** -0.5
    slopes = alibi_slopes(B_HEADS).reshape(B_KV, B_REP)

    n_cmp = (seq - B_CMP_LEN) // B_CMP_STRIDE + 1
    cmp_start = np.arange(n_cmp) * B_CMP_STRIDE
    cmp_idx = cmp_start[:, None] + np.arange(B_CMP_LEN)[None, :]
    cmp_end = jnp.asarray(cmp_start + B_CMP_LEN - 1, dtype=jnp.int32)
    kb = kv[:, :, 0, 0][:, cmp_idx] + cmp_pe[0][:, None, :]
    vb = kv[:, :, 0, 1][:, cmp_idx] + cmp_pe[1][:, None, :]
    k_cmp = jnp.einsum("bclgd,lde->bcge", kb, cmp_wk.reshape(B_CMP_LEN, B_HD, B_HD))
    k_cmp = rms_norm(k_cmp, k_norm_w[0])
    v_cmp = jnp.einsum("bclgd,lde->bcge", vb, cmp_wv.reshape(B_CMP_LEN, B_HD, B_HD))

    n_blk = seq // B_SEL_LEN
    n_top = min(B_N_SEL, n_blk)
    ks_blocks = rms_norm(kv[:, :, 1, 0], k_norm_w[1]).reshape(bsz, n_blk, B_SEL_LEN, B_KV, B_HD)
    ks_blocks = jnp.transpose(ks_blocks, (0, 3, 1, 2, 4))
    vs_blocks = jnp.transpose(kv[:, :, 1, 1].reshape(bsz, n_blk, B_SEL_LEN, B_KV, B_HD), (0, 3, 1, 2, 4))
    sel_start = np.arange(n_blk) * B_SEL_LEN
    ov = np.clip(np.minimum(cmp_start[:, None] + B_CMP_LEN, sel_start[None, :] + B_SEL_LEN)
                 - np.maximum(cmp_start[:, None], sel_start[None, :]), 0, None) / B_CMP_LEN
    overlap = jnp.asarray(ov, dtype=jnp.float32)

    kw_pad = jnp.pad(rms_norm(kv[:, :, 2, 0], k_norm_w[2]), ((0, 0), (B_WIN, 0), (0, 0), (0, 0)))
    vw_pad = jnp.pad(kv[:, :, 2, 1], ((0, 0), (B_WIN, 0), (0, 0), (0, 0)))

    bi = jnp.arange(bsz)[:, None, None, None]
    gi = jnp.arange(B_KV)[None, :, None, None]
    blk_ids = jnp.arange(n_blk)

    def block(i):
        t0 = i * B_QBLK
        t = t0 + jnp.arange(B_QBLK)
        qb = lax.dynamic_slice_in_dim(q, t0, B_QBLK, axis=1).reshape(bsz, B_QBLK, B_KV, B_REP, B_HD)
        dist_c = (t[:, None] - cmp_end[None, :]).astype(jnp.float32)
        s_c = jnp.einsum("bqgrd,bcgd->bgrqc", qb, k_cmp).astype(jnp.float32) * scale
        s_c = s_c - slopes[:, :, None, None] * dist_c
        p_c = masked_softmax(s_c, dist_c >= 0)
        o_c = jnp.einsum("bgrqc,bcgd->bqgrd", p_c, v_cmp.astype(jnp.float32))
        imp = jnp.einsum("bgrqc,cn->bgqn", p_c, overlap)
        cur = t // B_SEL_LEN
        forced = ((blk_ids[None, :] == 0) | (blk_ids[None, :] == cur[:, None])
                  | (blk_ids[None, :] == cur[:, None] - 1))
        causal_b = blk_ids[None, :] * B_SEL_LEN <= t[:, None]
        imp = jnp.where(forced, imp + FORCE, imp)
        imp = jnp.where(causal_b, imp, NEG)
        top_v, sel = lax.top_k(imp, n_top)
        ks = ks_blocks[bi, gi, sel].reshape(bsz, B_KV, B_QBLK, n_top * B_SEL_LEN, B_HD)
        vs = vs_blocks[bi, gi, sel].reshape(bsz, B_KV, B_QBLK, n_top * B_SEL_LEN, B_HD)
        pos = sel[..., None] * B_SEL_LEN + jnp.arange(B_SEL_LEN)
        valid = (top_v > NEG / 2)[..., None] & (pos <= t[None, None, :, None, None])
        pos = pos.reshape(bsz, B_KV, B_QBLK, n_top * B_SEL_LEN)
        valid = valid.reshape(bsz, B_KV, B_QBLK, n_top * B_SEL_LEN)
        dist_s = (t[None, None, :, None] - pos).astype(jnp.float32)
        s_s = jnp.einsum("bqgrd,bgqkd->bgrqk", qb, ks).astype(jnp.float32) * scale
        s_s = s_s - slopes[None, :, :, None, None] * dist_s[:, :, None]
        p_s = masked_softmax(s_s, valid[:, :, None])
        o_s = jnp.einsum("bgrqk,bgqkd->bqgrd", p_s, vs.astype(jnp.float32))
        kw = lax.dynamic_slice_in_dim(kw_pad, t0, B_QBLK + B_WIN, axis=1)
        vw = lax.dynamic_slice_in_dim(vw_pad, t0, B_QBLK + B_WIN, axis=1)
        pos_w = t0 - B_WIN + jnp.arange(B_QBLK + B_WIN)
        dist_w = t[:, None] - pos_w[None, :]
        mask_w = (dist_w >= 0) & (dist_w < B_WIN) & (pos_w[None, :] >= 0)
        s_w = jnp.einsum("bqgrd,bkgd->bgrqk", qb, kw).astype(jnp.float32) * scale
        s_w = s_w - slopes[:, :, None, None] * dist_w.astype(jnp.float32)
        p_w = masked_softmax(s_w, mask_w)
        o_w = jnp.einsum("bgrqk,bkgd->bqgrd", p_w, vw.astype(jnp.float32))
        g = lax.dynamic_slice_in_dim(gates, t0, B_QBLK, axis=1).reshape(bsz, B_QBLK, B_KV, B_REP, 3)
        o = g[..., 0:1] * o_c + g[..., 1:2] * o_s + g[..., 2:3] * o_w
        return o.reshape(bsz, B_QBLK, B_INNER)

    o = lax.map(block, jnp.arange(seq // B_QBLK))
    o = jnp.moveaxis(o, 0, 1).reshape(bsz, seq, B_INNER)
    y = o * jax.nn.silu(z.astype(jnp.float32))
    return y.astype(h.dtype) @ w_out


def rglru_mixer(h, w_in, conv_w, conv_b, w_a, b_a, w_x, b_x, lam, w_out):
    bsz, seq, _ = h.shape
    u = h @ w_in
    xb, z = jnp.split(u, [C_WIDTH], axis=-1)
    xb = causal_dwconv(xb, conv_w, conv_b).astype(jnp.float32)
    xblk = xb.reshape(bsz, seq, C_BLOCKS, C_BLK)
    r = jax.nn.sigmoid(jnp.einsum("bsnc,ncd->bsnd", xblk, w_a.astype(jnp.float32)).reshape(bsz, seq, C_WIDTH)
                       + b_a.astype(jnp.float32))
    ig = jax.nn.sigmoid(jnp.einsum("bsnc,ncd->bsnd", xblk, w_x.astype(jnp.float32)).reshape(bsz, seq, C_WIDTH)
                        + b_x.astype(jnp.float32))
    log_a = -C_EXP * jax.nn.softplus(-lam.astype(jnp.float32)) * r
    a = jnp.exp(log_a)
    inp = jnp.sqrt(-jnp.expm1(2.0 * log_a)) * (ig * xb)

    def combine(left, right):
        a1, b1 = left
        a2, b2 = right
        return a1 * a2, a2 * b1 + b2

    _, hs = lax.associative_scan(combine, (a, inp), axis=1)
    y = hs * jax.nn.silu(z.astype(jnp.float32))
    return y.astype(h.dtype) @ w_out


def dilated_window_attention(q, k, v, window, dil, slopes):
    bsz, seq, nh, hd = q.shape
    n_back = window // dil
    s_sub = seq // dil
    qb = math.gcd(D_QBLK, s_sub)
    n_blk = s_sub // qb

    def strided(t):
        return jnp.swapaxes(t.reshape(bsz, s_sub, dil, nh, hd), 1, 2)

    qs = strided(q).reshape(bsz, dil, n_blk, qb, nh, hd)
    kp = jnp.pad(strided(k), ((0, 0), (0, 0), (n_back, 0), (0, 0), (0, 0)))
    vp = jnp.pad(strided(v), ((0, 0), (0, 0), (n_back, 0), (0, 0), (0, 0)))
    key_idx = np.arange(n_blk)[:, None] * qb + np.arange(qb + n_back)[None, :]
    kb = kp[:, :, key_idx]
    vb = vp[:, :, key_idx]
    steps = np.arange(qb)[:, None] + n_back - np.arange(qb + n_back)[None, :]
    mask = jnp.asarray(((steps >= 0) & (steps <= n_back))[None] & ((key_idx - n_back) >= 0)[:, None, :])
    dist = jnp.asarray((steps * dil).astype(np.float32))
    s = jnp.einsum("bdnqhe,bdnkhe->bdnhqk", qs, kb).astype(jnp.float32) * (hd ** -0.5)
    s = s - slopes[:, None, None] * dist
    s = jnp.where(mask[:, None], s, NEG)
    m = jnp.max(s, axis=-1, keepdims=True)
    p = jnp.exp(s - m)
    den = jnp.sum(p, axis=-1, keepdims=True)
    o = jnp.einsum("bdnhqk,bdnkhe->bdnqhe", p / den, vb.astype(jnp.float32))
    lse = (m + jnp.log(den))[..., 0]
    o = jnp.swapaxes(o.reshape(bsz, dil, s_sub, nh, hd), 1, 2).reshape(bsz, seq, nh, hd)
    lse = jnp.swapaxes(jnp.swapaxes(lse, 3, 4).reshape(bsz, dil, s_sub, nh), 1, 2).reshape(bsz, seq, nh)
    return o, lse


def dilated_mixer(h, w_in, q_norm_w, k_norm_w, w_out):
    bsz, seq, _ = h.shape
    u = h @ w_in
    n_pat = len(D_PATTERNS)
    qkv, z = jnp.split(u, [3 * n_pat * D_INNER], axis=-1)
    qkv = qkv.reshape(bsz, seq, n_pat, 3, D_SLOTS, D_HD)
    slopes = alibi_slopes(D_SLOTS)
    outs = []
    lses = []
    for g, (window, dil) in enumerate(D_PATTERNS):
        qg = rms_norm(qkv[:, :, g, 0], q_norm_w)
        kg = rms_norm(qkv[:, :, g, 1], k_norm_w)
        og, lg = dilated_window_attention(qg, kg, qkv[:, :, g, 2], window, dil, slopes)
        outs.append(og)
        lses.append(lg)
    wts = jax.nn.softmax(jnp.stack(lses, axis=0), axis=0)
    o = jnp.sum(wts[..., None] * jnp.stack(outs, axis=0), axis=0).reshape(bsz, seq, D_INNER)
    y = o * jax.nn.silu(z.astype(jnp.float32))
    return y.astype(h.dtype) @ w_out


def _normal(k, shape, scale):
    return jax.random.normal(k, shape, jnp.float32) * scale


def setup_inputs(seed: int = 0) -> dict:
    key = jax.random.key(seed)
    ks = jax.random.split(key, 30)
    lam_u = jax.random.uniform(ks[22], (N_C, C_WIDTH), jnp.float32, 0.9, 0.999) ** (1.0 / C_EXP)
    gate_b = jnp.stack([
        _normal(ks[5], (N_A, A_HEADS), 0.1),
        jnp.broadcast_to(jnp.linspace(3.0, 6.0, A_HEADS), (N_A, A_HEADS)) + _normal(ks[26], (N_A, A_HEADS), 0.1),
    ], axis=1)
    return {
        "x": _normal(ks[0], (BATCH, SEQ, D_MODEL), 1.0),
        "norm_w": 1.0 + _normal(ks[1], (DEPTH, D_MODEL), 0.02),
        "a_w_in": _normal(ks[2], (N_A, D_MODEL, A_COLS), D_MODEL ** -0.5),
        "a_conv_w": _normal(ks[3], (N_A, CONV_W, 2 * A_QK), CONV_W ** -0.5),
        "a_conv_b": _normal(ks[4], (N_A, 2 * A_QK), 0.02),
        "a_gate_b": gate_b,
        "a_out_norm_w": 1.0 + _normal(ks[6], (N_A, A_INNER), 0.02),
        "a_w_out": _normal(ks[7], (N_A, A_INNER, D_MODEL), A_INNER ** -0.5),
        "b_w_in": _normal(ks[8], (N_B, D_MODEL, B_COLS), D_MODEL ** -0.5),
        "b_cmp_pe": _normal(ks[9], (N_B, 2, B_CMP_LEN, B_HD), 0.1),
        "b_cmp_wk": _normal(ks[10], (N_B, B_CMP_LEN * B_HD, B_HD), (B_CMP_LEN * B_HD) ** -0.5),
        "b_cmp_wv": _normal(ks[11], (N_B, B_CMP_LEN * B_HD, B_HD), (B_CMP_LEN * B_HD) ** -0.5),
        "b_q_norm_w": 1.0 + _normal(ks[12], (N_B, B_HD), 0.02),
        "b_k_norm_w": 1.0 + _normal(ks[13], (N_B, 3, B_HD), 0.02),
        "b_w_out": _normal(ks[14], (N_B, B_INNER, D_MODEL), B_INNER ** -0.5),
        "c_w_in": _normal(ks[15], (N_C, D_MODEL, C_COLS), D_MODEL ** -0.5),
        "c_conv_w": _normal(ks[16], (N_C, CONV_W, C_WIDTH), CONV_W ** -0.5),
        "c_conv_b": _normal(ks[17], (N_C, C_WIDTH), 0.02),
        "c_w_a": _normal(ks[18], (N_C, C_BLOCKS, C_BLK, C_BLK), C_BLK ** -0.5),
        "c_b_a": _normal(ks[19], (N_C, C_WIDTH), 0.02),
        "c_w_x": _normal(ks[20], (N_C, C_BLOCKS, C_BLK, C_BLK), C_BLK ** -0.5),
        "c_b_x": _normal(ks[21], (N_C, C_WIDTH), 0.02),
        "c_lambda": jnp.log(lam_u) - jnp.log1p(-lam_u),
        "c_w_out": _normal(ks[23], (N_C, C_WIDTH, D_MODEL), C_WIDTH ** -0.5),
        "d_w_in": _normal(ks[24], (N_D, D_MODEL, D_COLS), D_MODEL ** -0.5),
        "d_q_norm_w": 1.0 + _normal(ks[25], (N_D, D_HD), 0.02),
        "d_k_norm_w": 1.0 + _normal(ks[27], (N_D, D_HD), 0.02),
        "d_w_out": _normal(ks[28], (N_D, D_INNER, D_MODEL), D_INNER ** -0.5),
    }


def reference(x, norm_w, a_w_in, a_conv_w, a_conv_b, a_gate_b, a_out_norm_w, a_w_out,
              b_w_in, b_cmp_pe, b_cmp_wk, b_cmp_wv, b_q_norm_w, b_k_norm_w, b_w_out,
              c_w_in, c_conv_w, c_conv_b, c_w_a, c_b_a, c_w_x, c_b_x, c_lambda, c_w_out,
              d_w_in, d_q_norm_w, d_k_norm_w, d_w_out):
    for layer in range(DEPTH):
        kind = layer % N_MIXERS
        j = layer // N_MIXERS
        hn = rms_norm(x, norm_w[layer])
        if kind == 0:
            y = mlstm_mixer(hn, a_w_in[j], a_conv_w[j], a_conv_b[j], a_gate_b[j], a_out_norm_w[j], a_w_out[j])
        elif kind == 1:
            y = nsa_mixer(hn, b_w_in[j], b_cmp_pe[j], b_cmp_wk[j], b_cmp_wv[j], b_q_norm_w[j],
                          b_k_norm_w[j], b_w_out[j])
        elif kind == 2:
            y = rglru_mixer(hn, c_w_in[j], c_conv_w[j], c_conv_b[j], c_w_a[j], c_b_a[j], c_w_x[j],
                            c_b_x[j], c_lambda[j], c_w_out[j])
        else:
            y = dilated_mixer(hn, d_w_in[j], d_q_norm_w[j], d_k_norm_w[j], d_w_out[j])
        x = x + y.astype(x.dtype)
    return x
```

```python
import functools
import math

import numpy as np
import jax
import jax.numpy as jnp
from jax import lax
from jax.experimental import pallas as pl
from jax.experimental.pallas import tpu as pltpu

F32 = jnp.float32
BF16 = jnp.bfloat16

EPS = 1e-6
NEG = -1e30
FORCE = 1e4
CONV_W = 4

LANES = 128
V7X_VMEM_LIMIT = 56 * 1024 * 1024

A_CHUNK = 128
B_CMP_LEN = 32
B_CMP_STRIDE = 16
B_SEL_LEN = 64
B_N_SEL = 16
B_WIN = 512
B_KV = 4
C_BLOCKS = 16
C_EXP = 8.0
D_PATTERNS = ((128, 1), (512, 4), (2048, 16))
D_QBLK = 128


def _params(*sem):
    return pltpu.CompilerParams(dimension_semantics=sem, vmem_limit_bytes=V7X_VMEM_LIMIT)


def _sigmoid(x):
    return 1.0 / (1.0 + jnp.exp(-x))


def _silu(x):
    return x * _sigmoid(x)


def _softplus(x):
    return jnp.maximum(x, 0.0) + jnp.log1p(jnp.exp(-jnp.abs(x)))


def _head_rms(x, w):
    return x * lax.rsqrt(jnp.mean(x * x, axis=-1, keepdims=True) + EPS) * w


def _dot(a, b):
    return jnp.dot(a.astype(BF16), b.astype(BF16), preferred_element_type=F32)


def _dot_nt(a, b):
    return lax.dot_general(a.astype(BF16), b.astype(BF16), (((1,), (1,)), ((), ())),
                           preferred_element_type=F32)


def _dot_tn(a, b):
    return lax.dot_general(a.astype(BF16), b.astype(BF16), (((0,), (0,)), ((), ())),
                           preferred_element_type=F32)


def _pick(n, cands):
    for c in cands:
        if n % c == 0:
            return c
    raise ValueError(f"no tile for {n}")


def _rmsnorm_body(x_ref, w_ref, o_ref):
    x = x_ref[...]
    o_ref[...] = _head_rms(x, w_ref[...]).astype(o_ref.dtype)


def _rmsnorm(x2d, w):
    t, d = x2d.shape
    tm = _pick(t, (256, 128, 8))
    return pl.pallas_call(
        _rmsnorm_body, grid=(t // tm,),
        in_specs=[pl.BlockSpec((tm, d), lambda i: (i, 0)), pl.BlockSpec((1, d), lambda i: (0, 0))],
        out_specs=pl.BlockSpec((tm, d), lambda i: (i, 0)),
        out_shape=jax.ShapeDtypeStruct((t, d), BF16),
        compiler_params=_params("parallel"), name="rmsnorm")(x2d, w.reshape(1, d))


def _mm_body(a_ref, w_ref, o_ref):
    o_ref[...] = jnp.dot(a_ref[...], w_ref[...], preferred_element_type=F32)


def _mm_res_body(a_ref, w_ref, r_ref, o_ref):
    o_ref[...] = r_ref[...] + jnp.dot(a_ref[...], w_ref[...], preferred_element_type=F32)


def _matmul(a, w, resid=None, name="matmul"):
    m, k = a.shape
    n = w.shape[1]
    tm = _pick(m, (1024, 512, 256, 128))
    tn_cands = (1024, 768, 512, 384, 256, 128) if k <= 4096 else (512, 384, 256, 128)
    tn = _pick(n, tn_cands)
    in_specs = [pl.BlockSpec((tm, k), lambda i, j: (i, 0)), pl.BlockSpec((k, tn), lambda i, j: (0, j))]
    args = [a, w]
    body = _mm_body
    if resid is not None:
        in_specs.append(pl.BlockSpec((tm, tn), lambda i, j: (i, j)))
        args.append(resid)
        body = _mm_res_body
    return pl.pallas_call(
        body, grid=(m // tm, n // tn), in_specs=in_specs,
        out_specs=pl.BlockSpec((tm, tn), lambda i, j: (i, j)),
        out_shape=jax.ShapeDtypeStruct((m, n), F32),
        compiler_params=_params("parallel", "parallel"), name=name)(*args)


def _mlstm_body(gb_ref, q_ref, k_ref, v_ref, og_ref, z_ref, gcol_ref, grow_ref, cwq_ref, cwk_ref,
                cbq_ref, cbk_ref, onw_ref, y_ref, c_scr, n_scr, m_scr, extq, extk, *, q_scale):
    h = pl.program_id(1)
    c = pl.program_id(2)
    L = q_ref.shape[0]

    @pl.when(c == 0)
    def _():
        c_scr[...] = jnp.zeros_like(c_scr)
        n_scr[...] = jnp.zeros_like(n_scr)
        m_scr[...] = jnp.zeros_like(m_scr)
        extq[L:L + 8, :] = jnp.zeros((8, extq.shape[1]), F32)
        extk[L:L + 8, :] = jnp.zeros((8, extk.shape[1]), F32)

    def conv_silu(ext, x_ref, cw_ref, cb_ref):
        ext[0:8, :] = ext[L:L + 8, :]
        ext[8:L + 8, :] = x_ref[...]
        acc = cb_ref[...] + cw_ref[0:1, :] * ext[pl.ds(8 - CONV_W + 1, L), :]
        for j in range(1, CONV_W):
            acc = acc + cw_ref[j:j + 1, :] * ext[pl.ds(8 - CONV_W + 1 + j, L), :]
        return _silu(acc)

    q = conv_silu(extq, q_ref, cwq_ref, cbq_ref) * q_scale
    k = conv_silu(extk, k_ref, cwk_ref, cbk_ref)
    v = v_ref[...]

    bi = gb_ref[0, h]
    bf = gb_ref[1, h]
    gcol = gcol_ref[...]
    grow = grow_ref[...]
    li_col = gcol[:, 0:1] + bi
    li_row = grow[0:1, :] + bi
    lf_col = -_softplus(-(gcol[:, 1:2] + bf))
    lf_row = -_softplus(-(grow[1:2, :] + bf))

    t_io = lax.broadcasted_iota(jnp.int32, (L, L), 0)
    s_io = lax.broadcasted_iota(jnp.int32, (L, L), 1)
    causal = s_io <= t_io
    b_col = jnp.sum(jnp.where(causal, lf_row, 0.0), axis=1, keepdims=True)
    b_row = jnp.sum(jnp.where(t_io <= s_io, lf_col, 0.0), axis=0, keepdims=True)
    b_last = jnp.sum(lf_row, axis=1, keepdims=True)
    m_st = m_scr[...]

    dmat = jnp.where(causal, b_col - b_row + li_row, NEG)
    inter = b_col + m_st
    m_t = jnp.maximum(inter, jnp.max(dmat, axis=1, keepdims=True))
    w_intra = jnp.exp(dmat - m_t)
    w_inter = jnp.exp(inter - m_t)
    a = w_intra * _dot_nt(q, k)
    c_st = c_scr[...]
    n_st = n_scr[...]
    num = _dot(a, v) + w_inter * _dot(q, c_st)
    qn = jnp.sum(a, axis=1, keepdims=True) + w_inter * jnp.sum(q * n_st, axis=1, keepdims=True)
    hc = num / jnp.maximum(jnp.abs(qn), jnp.exp(-m_t))

    g_row = b_last - b_row + li_row
    g_col = b_last - b_col + li_col
    m_new = jnp.maximum(b_last + m_st, jnp.max(g_row, axis=1, keepdims=True))
    ws_col = jnp.exp(g_col - m_new)
    decay = jnp.exp(b_last + m_st - m_new)
    kw = k * ws_col
    c_scr[...] = decay * c_st + _dot_tn(kw, v)
    n_scr[...] = decay * n_st + jnp.sum(kw, axis=0, keepdims=True)
    m_scr[...] = m_new

    hn = _head_rms(hc, onw_ref[...])
    y_ref[...] = (hn * _sigmoid(og_ref[...]) * _silu(z_ref[...])).astype(y_ref.dtype)


def _mlstm_layer(x2d, hn, bsz, seq, w_in, conv_w, conv_b, gate_b, out_norm_w, w_out):
    t, d = x2d.shape
    heads = gate_b.shape[-1]
    qk = conv_w.shape[-1] // 2
    inner = out_norm_w.shape[-1]
    dqk, dv = qk // heads, inner // heads
    n_main = 2 * qk + 3 * inner
    L = A_CHUNK
    nc = seq // L
    assert dqk % LANES == 0 and dv % LANES == 0 and 2 * heads <= LANES

    u = _matmul(hn, w_in[:, :n_main].astype(BF16), name="a_in").reshape(bsz, seq, n_main)
    w_g = jnp.pad(w_in[:, n_main:], ((0, 0), (0, LANES - 2 * heads))).astype(BF16)
    gts = _matmul(hn, w_g, name="a_gates")[:, :2 * heads].reshape(bsz, nc, L, 2, heads)
    g_col = jnp.transpose(gts, (0, 4, 1, 2, 3))
    g_row = jnp.transpose(gts, (0, 4, 1, 3, 2))

    kq0, kv0, ko0, kz0 = qk // dqk, 2 * qk // dv, (2 * qk + inner) // dv, (2 * qk + 2 * inner) // dv
    blk = lambda w, off: pl.BlockSpec((None, L, w), lambda b, h, c: (b, c, off + h))
    par = lambda r, w, off: pl.BlockSpec((r, w), lambda b, h, c: (0, off + h))
    y = pl.pallas_call(
        functools.partial(_mlstm_body, q_scale=dqk ** -0.5),
        grid=(bsz, heads, nc),
        in_specs=[pl.BlockSpec(memory_space=pltpu.SMEM),
                  blk(dqk, 0), blk(dqk, kq0), blk(dv, kv0), blk(dv, ko0), blk(dv, kz0),
                  pl.BlockSpec((None, None, None, L, 2), lambda b, h, c: (b, h, c, 0, 0)),
                  pl.BlockSpec((None, None, None, 2, L), lambda b, h, c: (b, h, c, 0, 0)),
                  par(CONV_W, dqk, 0), par(CONV_W, dqk, kq0), par(1, dqk, 0), par(1, dqk, kq0),
                  par(1, dv, 0)],
        out_specs=pl.BlockSpec((None, L, dv), lambda b, h, c: (b, c, h)),
        out_shape=jax.ShapeDtypeStruct((bsz, seq, inner), BF16),
        scratch_shapes=[pltpu.VMEM((dqk, dv), F32), pltpu.VMEM((1, dqk), F32), pltpu.VMEM((1, 1), F32),
                        pltpu.VMEM((L + 8, dqk), F32), pltpu.VMEM((L + 8, dqk), F32)],
        compiler_params=_params("parallel", "parallel", "arbitrary"), name="a_mlstm",
    )(gate_b, u, u, u, u, u, g_col, g_row, conv_w, conv_w, conv_b.reshape(1, -1), conv_b.reshape(1, -1),
      out_norm_w.reshape(1, -1))
    return _matmul(y.reshape(t, inner), w_out.astype(BF16), resid=x2d, name="a_out")


def _rglru_body(x_ref, z_ref, cw_ref, cb_ref, wa_ref, wx_ref, ba_ref, bx_ref, lam_ref, y_ref,
                ext, a_scr, b_scr, h_scr, *, nblk, pb, chunk):
    s = pl.program_id(1)
    ts = x_ref.shape[0]

    @pl.when(s == 0)
    def _():
        ext[ts:ts + 8, :] = jnp.zeros((8, ext.shape[1]), F32)
        h_scr[...] = jnp.zeros_like(h_scr)

    ext[0:8, :] = ext[ts:ts + 8, :]
    ext[8:ts + 8, :] = x_ref[...]
    for n in range(nblk):
        cs = slice(n * pb, (n + 1) * pb)
        xc = cb_ref[:, cs] + cw_ref[0:1, cs] * ext[pl.ds(8 - CONV_W + 1, ts), cs]
        for j in range(1, CONV_W):
            xc = xc + cw_ref[j:j + 1, cs] * ext[pl.ds(8 - CONV_W + 1 + j, ts), cs]
        r = _sigmoid(_dot(xc, wa_ref[n]) + ba_ref[:, cs])
        ig = _sigmoid(_dot(xc, wx_ref[n]) + bx_ref[:, cs])
        log_a = (-C_EXP) * _softplus(-lam_ref[:, cs]) * r
        a = jnp.exp(log_a)
        a_scr[:, cs] = a
        b_scr[:, cs] = jnp.sqrt(-jnp.tanh(log_a) * (a * a + 1.0)) * (ig * xc)

    width = a_scr.shape[1]
    for c0 in range(0, width, chunk):
        cs = slice(c0, c0 + chunk)

        def row(t, h, cs=cs):
            h = a_scr[pl.ds(t, 1), cs] * h + b_scr[pl.ds(t, 1), cs]
            b_scr[pl.ds(t, 1), cs] = h
            return h

        h_scr[:, cs] = lax.fori_loop(0, ts, row, h_scr[:, cs], unroll=8)

    y_ref[...] = (b_scr[...] * _silu(z_ref[...])).astype(y_ref.dtype)


def _pad_blocks(w, nblk, blk, pb, axis):
    shp = w.shape
    w = w.reshape(shp[:axis] + (nblk, blk) + shp[axis + 1:])
    pad = [(0, 0)] * w.ndim
    pad[axis + 1] = (0, pb - blk)
    w = jnp.pad(w, pad)
    return w.reshape(shp[:axis] + (nblk * pb,) + shp[axis + 1:])


def _rglru_layer(x2d, hn, bsz, seq, w_in, conv_w, conv_b, w_a, b_a, w_x, b_x, lam, w_out):
    t, d = x2d.shape
    width = conv_w.shape[-1]
    nblk = w_a.shape[0]
    blk = width // nblk
    pb = -(-blk // LANES) * LANES
    wp = nblk * pb
    padv = lambda v: _pad_blocks(v.reshape(-1, width), nblk, blk, pb, 1)

    w_in_p = jnp.concatenate([_pad_blocks(w_in[:, :width], nblk, blk, pb, 1),
                              _pad_blocks(w_in[:, width:], nblk, blk, pb, 1)], axis=1).astype(BF16)
    u = _matmul(hn, w_in_p, name="c_in").reshape(bsz, seq, 2 * wp)
    padw = lambda w: jnp.pad(w, ((0, 0), (0, pb - blk), (0, pb - blk))).astype(BF16)

    ts = _pick(seq, (128, 64, 8))
    chunk = _pick(wp, (1536, 1024, 768, 512, 384, 256, 128))
    full = lambda r: pl.BlockSpec((r, wp), lambda b, s: (0, 0))
    y = pl.pallas_call(
        functools.partial(_rglru_body, nblk=nblk, pb=pb, chunk=chunk),
        grid=(bsz, seq // ts),
        in_specs=[pl.BlockSpec((None, ts, wp), lambda b, s: (b, s, 0)),
                  pl.BlockSpec((None, ts, wp), lambda b, s: (b, s, 1)),
                  full(CONV_W), full(1),
                  pl.BlockSpec((nblk, pb, pb), lambda b, s: (0, 0, 0)),
                  pl.BlockSpec((nblk, pb, pb), lambda b, s: (0, 0, 0)),
                  full(1), full(1), full(1)],
        out_specs=pl.BlockSpec((None, ts, wp), lambda b, s: (b, s, 0)),
        out_shape=jax.ShapeDtypeStruct((bsz, seq, wp), BF16),
        scratch_shapes=[pltpu.VMEM((ts + 8, wp), F32), pltpu.VMEM((ts, wp), F32), pltpu.VMEM((ts, wp), F32),
                        pltpu.VMEM((1, wp), F32)],
        compiler_params=_params("parallel", "arbitrary"), name="c_rglru",
    )(u, u, padv(conv_w), padv(conv_b), padw(w_a), padw(w_x), padv(b_a), padv(b_x), padv(lam))
    w_out_p = _pad_blocks(w_out, nblk, blk, pb, 0).astype(BF16)
    return _matmul(y.reshape(t, wp), w_out_p, resid=x2d, name="c_out")


def _dilated_body(q_ref, kp_ref, kc_ref, vp_ref, vc_ref, qw_ref, kw_ref, o_ref, lse_ref, *, heads, hd, dil,
                  n_back, slopes):
    i = pl.program_id(2)
    tq = q_ref.shape[0]
    qi = lax.broadcasted_iota(jnp.int32, (tq, 2 * tq), 0)
    kj = lax.broadcasted_iota(jnp.int32, (tq, 2 * tq), 1)
    steps = qi + tq - kj
    valid = (steps >= 0) & (steps <= n_back) & ((kj >= tq) | (i > 0))
    dist = (steps * dil).astype(F32)
    lane = lax.broadcasted_iota(jnp.int32, (tq, LANES), 1)
    lse_tile = jnp.zeros((tq, LANES), F32)
    scale = hd ** -0.5
    for h in range(heads):
        hs = slice(h * hd, (h + 1) * hd)
        qn = _head_rms(q_ref[:, hs], qw_ref[...])
        kn = _head_rms(jnp.concatenate([kp_ref[:, hs], kc_ref[:, hs]], axis=0), kw_ref[...])
        v2 = jnp.concatenate([vp_ref[:, hs], vc_ref[:, hs]], axis=0)
        s = _dot_nt(qn, kn) * scale - slopes[h] * dist
        s = jnp.where(valid, s, NEG)
        m = jnp.max(s, axis=1, keepdims=True)
        p = jnp.exp(s - m)
        den = jnp.sum(p, axis=1, keepdims=True)
        o_ref[:, hs] = _dot(p / den, v2)
        lse_tile = jnp.where(lane == h, m + jnp.log(den), lse_tile)
    lse_ref[...] = lse_tile


def _dmix_body(o0_ref, o1_ref, o2_ref, l0_ref, l1_ref, l2_ref, z_ref, y_ref, *, heads, hd):
    o_refs = (o0_ref, o1_ref, o2_ref)
    l_refs = (l0_ref, l1_ref, l2_ref)
    for h in range(heads):
        hs = slice(h * hd, (h + 1) * hd)
        ls = [l[:, h:h + 1] for l in l_refs]
        mx = functools.reduce(jnp.maximum, ls)
        es = [jnp.exp(l - mx) for l in ls]
        tot = functools.reduce(lambda a, b: a + b, es)
        o = sum((e / tot) * o_ref[:, hs] for e, o_ref in zip(es, o_refs))
        y_ref[:, hs] = (o * _silu(z_ref[:, hs])).astype(y_ref.dtype)


def _dilated_layer(x2d, hn, bsz, seq, w_in, q_norm_w, k_norm_w, w_out):
    t, d = x2d.shape
    hd = q_norm_w.shape[-1]
    inner = w_out.shape[0]
    heads = inner // hd
    n_pat = len(D_PATTERNS)
    cols = 3 * n_pat * inner + inner
    assert heads <= LANES and n_pat == 3
    slopes = tuple(float(2.0 ** (-8.0 * (h + 1) / heads)) for h in range(heads))
    u = _matmul(hn, w_in.astype(BF16), name="d_in")
    ncb = cols // inner

    outs, lses = [], []
    for g, (window, dil) in enumerate(D_PATTERNS):
        n_back = window // dil
        s_sub = seq // dil
        tq = D_QBLK
        assert n_back <= tq and s_sub % tq == 0
        ug = u.reshape(bsz, s_sub, dil * cols)
        cur = lambda cb: pl.BlockSpec((None, tq, inner), lambda b, r, i, cb=cb: (b, i, r * ncb + cb))
        prev = lambda cb: pl.BlockSpec((None, tq, inner),
                                       lambda b, r, i, cb=cb: (b, jnp.maximum(i - 1, 0), r * ncb + cb))
        wspec = pl.BlockSpec((1, hd), lambda b, r, i: (0, 0))
        o, lse = pl.pallas_call(
            functools.partial(_dilated_body, heads=heads, hd=hd, dil=dil, n_back=n_back, slopes=slopes),
            grid=(bsz, dil, s_sub // tq),
            in_specs=[cur(3 * g), prev(3 * g + 1), cur(3 * g + 1), prev(3 * g + 2), cur(3 * g + 2), wspec, wspec],
            out_specs=[pl.BlockSpec((None, tq, inner), lambda b, r, i: (b, i, r)),
                       pl.BlockSpec((None, tq, LANES), lambda b, r, i: (b, i, r))],
            out_shape=[jax.ShapeDtypeStruct((bsz, s_sub, dil * inner), F32),
                       jax.ShapeDtypeStruct((bsz, s_sub, dil * LANES), F32)],
            compiler_params=_params("parallel", "parallel", "parallel"), name=f"d_attn{g}",
        )(ug, ug, ug, ug, ug, q_norm_w.reshape(1, hd), k_norm_w.reshape(1, hd))
        outs.append(o.reshape(t, inner))
        lses.append(lse.reshape(t, LANES))

    tt = _pick(t, (256, 128, 8))
    ospec = pl.BlockSpec((tt, inner), lambda i: (i, 0))
    lspec = pl.BlockSpec((tt, LANES), lambda i: (i, 0))
    y = pl.pallas_call(
        functools.partial(_dmix_body, heads=heads, hd=hd),
        grid=(t // tt,),
        in_specs=[ospec, ospec, ospec, lspec, lspec, lspec, pl.BlockSpec((tt, inner), lambda i: (i, ncb - 1))],
        out_specs=ospec, out_shape=jax.ShapeDtypeStruct((t, inner), BF16),
        compiler_params=_params("parallel"), name="d_mix",
    )(*outs, *lses, u)
    return _matmul(y, w_out.astype(BF16), resid=x2d, name="d_out")


def _nsa_prep_body(k0_ref, v0_ref, k1_ref, v1_ref, k2_ref, v2_ref, pe_ref, wk_ref, wv_ref, knw_ref,
                   kc_ref, vc_ref, ks_ref, vs_ref, kw_ref, vw_ref, *, stride):
    n_c = kc_ref.shape[0]

    def compress(x_ref, w_ref, pe):
        p1 = jnp.zeros(kc_ref.shape, F32)
        p2 = jnp.zeros(kc_ref.shape, F32)
        for l in range(stride):
            x = x_ref[pl.ds(l, n_c, stride=stride), :]
            p1 = p1 + _dot(x + pe[l:l + 1, :], w_ref[l])
            p2 = p2 + _dot(x + pe[stride + l:stride + l + 1, :], w_ref[stride + l])
        return p1 + pltpu.roll(p2, n_c - 1, 0)

    kc_ref[...] = _head_rms(compress(k0_ref, wk_ref, pe_ref[0]), knw_ref[0:1, :])
    vc_ref[...] = compress(v0_ref, wv_ref, pe_ref[1])
    ks_ref[...] = _head_rms(k1_ref[...], knw_ref[1:2, :]).astype(ks_ref.dtype)
    vs_ref[...] = v1_ref[...].astype(vs_ref.dtype)
    kw_ref[...] = _head_rms(k2_ref[...], knw_ref[2:3, :]).astype(kw_ref.dtype)
    vw_ref[...] = v2_ref[...].astype(vw_ref.dtype)


def _nsa_body(sl_ref, q_ref, z_ref, g_ref, qw_ref, kc_ref, vc_ref, ks_ref, vs_ref, kw_ref, vw_ref, ov_ref,
              y_ref, acc, m_scr, l_scr, o_scr, *, rep, hd, n_top, n_wblk):
    g = pl.program_id(1)
    i = pl.program_id(2)
    tq = q_ref.shape[0]
    n_cmp = kc_ref.shape[0]
    n_blk = ov_ref.shape[1]
    t0 = i * tq
    scale = hd ** -0.5
    rows = lambda r: slice(r * tq, (r + 1) * tq)

    qs = [_head_rms(q_ref[:, r * hd:(r + 1) * hd], qw_ref[...]).astype(BF16) for r in range(rep)]
    qall = jnp.concatenate(qs, axis=0)
    gts = _sigmoid(g_ref[...])
    t_col = t0 + lax.broadcasted_iota(jnp.int32, (tq, 1), 0)

    cmp_end = lax.broadcasted_iota(jnp.int32, (tq, n_cmp), 1) * B_CMP_STRIDE + (B_CMP_LEN - 1)
    dist_ci = t_col - cmp_end
    mask_c = dist_ci >= 0
    dist_c = dist_ci.astype(F32)
    s_all = _dot_nt(qall, kc_ref[...])
    psum = jnp.zeros((tq, n_cmp), F32)
    ps = []
    for r in range(rep):
        s = s_all[rows(r), :] * scale - sl_ref[g, r] * dist_c
        s = jnp.where(mask_c, s, NEG)
        m = jnp.max(s, axis=1, keepdims=True)
        p = jnp.where(mask_c, jnp.exp(s - m), 0.0)
        den = jnp.sum(p, axis=1, keepdims=True)
        p = p / jnp.where(den > 0, den, 1.0)
        psum = psum + p
        ps.append(p.astype(BF16))
    o_c = _dot(jnp.concatenate(ps, axis=0), vc_ref[...])
    for r in range(rep):
        o_scr[rows(r), :] = gts[:, 3 * r:3 * r + 1] * o_c[rows(r), :]

    imp = jnp.dot(psum, ov_ref[...], preferred_element_type=F32, precision=lax.Precision.HIGHEST)
    blk = lax.broadcasted_iota(jnp.int32, (tq, n_blk), 1)
    sel_shift = B_SEL_LEN.bit_length() - 1
    cur = t_col >> sel_shift
    forced = (blk == 0) | (blk == cur) | (blk == cur - 1)
    imp = jnp.where(forced, imp + FORCE, imp)
    imp = jnp.where(blk * B_SEL_LEN <= t_col, imp, NEG)
    cnt = jnp.zeros((tq, n_blk), F32)
    for mb in range(n_blk):
        col = imp[:, mb:mb + 1]
        beats = (col > imp) | ((col == imp) & (blk > mb))
        cnt = cnt + jnp.where(beats, 1.0, 0.0)
    sel = jnp.where((cnt < n_top) & (imp > NEG / 2), 1.0, 0.0).astype(BF16)

    def init_state():
        acc[...] = jnp.zeros_like(acc)
        m_scr[...] = jnp.full(m_scr.shape, NEG, F32)
        l_scr[...] = jnp.zeros_like(l_scr)

    def flash_block(kb, k_ref, v_ref, ok_fn):
        kblk = k_ref[pl.ds(pl.multiple_of(kb * tq, tq), tq), :]
        vblk = v_ref[pl.ds(pl.multiple_of(kb * tq, tq), tq), :]
        s_all = _dot_nt(qall, kblk)
        pos = kb * tq + lax.broadcasted_iota(jnp.int32, (tq, tq), 1)
        dist_i = t_col - pos
        ok = ok_fn(kb, dist_i)
        dist = dist_i.astype(F32)
        for r in range(rep):
            s = s_all[rows(r), :] * scale - sl_ref[g, r] * dist
            s = jnp.where(ok, s, NEG)
            m_old = m_scr[rows(r), :]
            m_new = jnp.maximum(m_old, jnp.max(s, axis=1, keepdims=True))
            alpha = jnp.exp(m_old - m_new)
            p = jnp.exp(s - m_new)
            l_scr[rows(r), :] = alpha * l_scr[rows(r), :] + jnp.sum(p, axis=1, keepdims=True)
            acc[rows(r), :] = alpha * acc[rows(r), :] + _dot(p, vblk)
            m_scr[rows(r), :] = m_new

    def finish(branch):
        for r in range(rep):
            o_scr[rows(r), :] += gts[:, 3 * r + branch:3 * r + branch + 1] * (acc[rows(r), :] / l_scr[rows(r), :])

    per = tq // B_SEL_LEN
    nb_io = lax.broadcasted_iota(jnp.int32, (n_blk, tq), 0)
    kl_io = lax.broadcasted_iota(jnp.int32, (n_blk, tq), 1) >> sel_shift

    def sel_ok(kb, dist_i):
        expand = jnp.where(nb_io == kb * per + kl_io, 1.0, 0.0).astype(BF16)
        member = jnp.dot(sel, expand, preferred_element_type=F32)
        return (member > 0.5) & (dist_i >= 0)

    init_state()

    def sel_step(kb, carry):
        flash_block(kb, ks_ref, vs_ref, sel_ok)
        return carry

    lax.fori_loop(0, i + 1, sel_step, 0)
    finish(1)

    init_state()
    for j in range(n_wblk):
        kb = i - (n_wblk - 1) + j

        @pl.when(kb >= 0)
        def _(kb=kb):
            flash_block(kb, kw_ref, vw_ref, lambda kb_, d: (d >= 0) & (d < B_WIN))

    finish(2)

    for r in range(rep):
        hs = slice(r * hd, (r + 1) * hd)
        y_ref[:, hs] = (o_scr[rows(r), :] * _silu(z_ref[:, hs])).astype(y_ref.dtype)


def _nsa_layer(x2d, hn, bsz, seq, w_in, cmp_pe, cmp_wk, cmp_wv, q_norm_w, k_norm_w, w_out):
    t, d = x2d.shape
    hd = q_norm_w.shape[-1]
    inner = w_out.shape[0]
    heads = inner // hd
    rep = heads // B_KV
    kvw = B_KV * hd
    n_main = 2 * inner + 6 * kvw
    assert B_CMP_LEN == 2 * B_CMP_STRIDE and seq % B_CMP_STRIDE == 0 and 3 * heads <= LANES
    n_cmp = seq // B_CMP_STRIDE
    n_blk = seq // B_SEL_LEN
    n_top = min(B_N_SEL, n_blk)
    tq = 128
    assert tq % B_SEL_LEN == 0 and B_WIN % tq == 0 and seq % tq == 0

    u = _matmul(hn, w_in[:, :n_main].astype(BF16), name="b_in").reshape(bsz, seq, n_main)
    w_g = jnp.pad(w_in[:, n_main:], ((0, 0), (0, LANES - 3 * heads))).astype(BF16)
    gts = _matmul(hn, w_g, name="b_gates")[:, :3 * heads].reshape(bsz, seq, B_KV, 3 * rep)
    gts = jnp.transpose(gts, (0, 2, 1, 3))

    kv0 = 2 * inner // hd
    kvspec = lambda br, kv: pl.BlockSpec((None, seq, hd), lambda b, g: (b, 0, kv0 + (br * 2 + kv) * B_KV + g))
    cspec = pl.BlockSpec((None, None, n_cmp, hd), lambda b, g: (b, g, 0, 0))
    sspec = pl.BlockSpec((None, None, seq, hd), lambda b, g: (b, g, 0, 0))
    wspec = pl.BlockSpec((B_CMP_LEN, hd, hd), lambda b, g: (0, 0, 0))
    c_sh = jax.ShapeDtypeStruct((bsz, B_KV, n_cmp, hd), F32)
    s_sh = jax.ShapeDtypeStruct((bsz, B_KV, seq, hd), BF16)
    kc, vc, ks, vs, kw, vw = pl.pallas_call(
        functools.partial(_nsa_prep_body, stride=B_CMP_STRIDE),
        grid=(bsz, B_KV),
        in_specs=[kvspec(0, 0), kvspec(0, 1), kvspec(1, 0), kvspec(1, 1), kvspec(2, 0), kvspec(2, 1),
                  pl.BlockSpec((2, B_CMP_LEN, hd), lambda b, g: (0, 0, 0)), wspec, wspec,
                  pl.BlockSpec((3, hd), lambda b, g: (0, 0))],
        out_specs=[cspec, cspec, sspec, sspec, sspec, sspec],
        out_shape=[c_sh, c_sh, s_sh, s_sh, s_sh, s_sh],
        compiler_params=_params("parallel", "parallel"), name="b_prep",
    )(u, u, u, u, u, u, cmp_pe, cmp_wk.reshape(B_CMP_LEN, hd, hd).astype(BF16),
      cmp_wv.reshape(B_CMP_LEN, hd, hd).astype(BF16), k_norm_w)

    cmp_start = np.arange(n_cmp) * B_CMP_STRIDE
    sel_start = np.arange(n_blk) * B_SEL_LEN
    ov = np.clip(np.minimum(cmp_start[:, None] + B_CMP_LEN, sel_start[None, :] + B_SEL_LEN)
                 - np.maximum(cmp_start[:, None], sel_start[None, :]), 0, None) / B_CMP_LEN
    ov[cmp_start + B_CMP_LEN > seq] = 0.0
    slopes = jnp.asarray(2.0 ** (-8.0 * np.arange(1, heads + 1) / heads), dtype=F32).reshape(B_KV, rep)

    rw = rep * hd
    cspec3 = pl.BlockSpec((None, None, n_cmp, hd), lambda b, g, i: (b, g, 0, 0))
    sspec3 = pl.BlockSpec((None, None, seq, hd), lambda b, g, i: (b, g, 0, 0))
    y = pl.pallas_call(
        functools.partial(_nsa_body, rep=rep, hd=hd, n_top=n_top, n_wblk=B_WIN // tq + 1),
        grid=(bsz, B_KV, seq // tq),
        in_specs=[pl.BlockSpec(memory_space=pltpu.SMEM),
                  pl.BlockSpec((None, tq, rw), lambda b, g, i: (b, i, g)),
                  pl.BlockSpec((None, tq, rw), lambda b, g, i: (b, i, inner // rw + g)),
                  pl.BlockSpec((None, None, tq, 3 * rep), lambda b, g, i: (b, g, i, 0)),
                  pl.BlockSpec((1, hd), lambda b, g, i: (0, 0)),
                  cspec3, cspec3, sspec3, sspec3, sspec3, sspec3,
                  pl.BlockSpec((n_cmp, n_blk), lambda b, g, i: (0, 0))],
        out_specs=pl.BlockSpec((None, tq, rw), lambda b, g, i: (b, i, g)),
        out_shape=jax.ShapeDtypeStruct((bsz, seq, inner), BF16),
        scratch_shapes=[pltpu.VMEM((rep * tq, hd), F32), pltpu.VMEM((rep * tq, 1), F32),
                        pltpu.VMEM((rep * tq, 1), F32), pltpu.VMEM((rep * tq, hd), F32)],
        compiler_params=_params("parallel", "parallel", "arbitrary"), name="b_attn",
    )(slopes, u, u, gts, q_norm_w.reshape(1, hd), kc, vc, ks, vs, kw, vw, jnp.asarray(ov, dtype=F32))
    return _matmul(y.reshape(t, inner), w_out.astype(BF16), resid=x2d, name="b_out")


def kernel(x, norm_w, a_w_in, a_conv_w, a_conv_b, a_gate_b, a_out_norm_w, a_w_out, b_w_in, b_cmp_pe, b_cmp_wk, b_cmp_wv, b_q_norm_w, b_k_norm_w, b_w_out, c_w_in, c_conv_w, c_conv_b, c_w_a, c_b_a, c_w_x, c_b_x, c_lambda, c_w_out, d_w_in, d_q_norm_w, d_k_norm_w, d_w_out):
    bsz, seq, d = x.shape
    depth = norm_w.shape[0]
    x2d = x.reshape(bsz * seq, d)
    for layer in range(depth):
        kind, j = layer % 4, layer // 4
        hn = _rmsnorm(x2d, norm_w[layer])
        if kind == 0:
            x2d = _mlstm_layer(x2d, hn, bsz, seq, a_w_in[j], a_conv_w[j], a_conv_b[j], a_gate_b[j],
                               a_out_norm_w[j], a_w_out[j])
        elif kind == 1:
            x2d = _nsa_layer(x2d, hn, bsz, seq, b_w_in[j], b_cmp_pe[j], b_cmp_wk[j], b_cmp_wv[j], b_q_norm_w[j],
                             b_k_norm_w[j], b_w_out[j])
        elif kind == 2:
            x2d = _rglru_layer(x2d, hn, bsz, seq, c_w_in[j], c_conv_w[j], c_conv_b[j], c_w_a[j], c_b_a[j],
                               c_w_x[j], c_b_x[j], c_lambda[j], c_w_out[j])
        else:
            x2d = _dilated_layer(x2d, hn, bsz, seq, d_w_in[j], d_q_norm_w[j], d_k_norm_w[j], d_w_out[j])
    return x2d.reshape(bsz, seq, d)
```

```python
import functools
import math

import ml_dtypes
import numpy as np
import jax
import jax.numpy as jnp
from jax import lax
from jax.experimental import pallas as pl
from jax.experimental.pallas import tpu as pltpu

F32 = jnp.float32
BF16 = jnp.bfloat16

EPS = 1e-6
NEG = -1e30
FORCE = 1e4
CONV_W = 4

LANES = 128
V7X_VMEM_LIMIT = 56 * 1024 * 1024

A_CHUNK = 128
B_CMP_LEN = 32
B_CMP_STRIDE = 16
B_SEL_LEN = 64
B_N_SEL = 16
B_WIN = 512
B_KV = 4
NSA_POS_SPLIT = 128
NSA_SEL_OFF = 6
C_BLOCKS = 16
C_EXP = 8.0
D_PATTERNS = ((128, 1), (512, 4), (2048, 16))
D_QBLK = 128


def _params(*sem):
    return pltpu.CompilerParams(dimension_semantics=sem, vmem_limit_bytes=V7X_VMEM_LIMIT)


def _sigmoid(x):
    return 1.0 / (1.0 + jnp.exp(-x))


def _silu(x):
    return x * _sigmoid(x)


def _softplus(x):
    return jnp.maximum(x, 0.0) + jnp.log1p(jnp.exp(-jnp.abs(x)))


def _head_rms(x, w):
    return x * lax.rsqrt(jnp.mean(x * x, axis=-1, keepdims=True) + EPS) * w


def _dot(a, b):
    return jnp.dot(a.astype(BF16), b.astype(BF16), preferred_element_type=F32)


def _dot_nt(a, b):
    return lax.dot_general(a.astype(BF16), b.astype(BF16), (((1,), (1,)), ((), ())),
                           preferred_element_type=F32)


def _dot_tn(a, b):
    return lax.dot_general(a.astype(BF16), b.astype(BF16), (((0,), (0,)), ((), ())),
                           preferred_element_type=F32)


def _pick(n, cands):
    for c in cands:
        if n % c == 0:
            return c
    raise ValueError(f"no tile for {n}")


def _rmsnorm_body(x_ref, w_ref, o_ref):
    x = x_ref[...]
    o_ref[...] = _head_rms(x, w_ref[...]).astype(o_ref.dtype)


def _rmsnorm(x2d, w):
    t, d = x2d.shape
    tm = _pick(t, (256, 128, 8))
    return pl.pallas_call(
        _rmsnorm_body, grid=(t // tm,),
        in_specs=[pl.BlockSpec((tm, d), lambda i: (i, 0)), pl.BlockSpec((1, d), lambda i: (0, 0))],
        out_specs=pl.BlockSpec((tm, d), lambda i: (i, 0)),
        out_shape=jax.ShapeDtypeStruct((t, d), BF16),
        compiler_params=_params("parallel"), name="rmsnorm")(x2d, w.reshape(1, d))


def _mm_body(a_ref, w_ref, o_ref):
    o_ref[...] = jnp.dot(a_ref[...], w_ref[...], preferred_element_type=F32)


def _mm_res_body(a_ref, w_ref, r_ref, o_ref):
    o_ref[...] = r_ref[...] + jnp.dot(a_ref[...], w_ref[...], preferred_element_type=F32)


def _matmul(a, w, resid=None, name="matmul"):
    m, k = a.shape
    n = w.shape[1]
    tm = _pick(m, (1024, 512, 256, 128))
    tn_cands = (1024, 768, 512, 384, 256, 128) if k <= 4096 else (512, 384, 256, 128)
    tn = _pick(n, tn_cands)
    in_specs = [pl.BlockSpec((tm, k), lambda i, j: (i, 0)), pl.BlockSpec((k, tn), lambda i, j: (0, j))]
    args = [a, w]
    body = _mm_body
    if resid is not None:
        in_specs.append(pl.BlockSpec((tm, tn), lambda i, j: (i, j)))
        args.append(resid)
        body = _mm_res_body
    return pl.pallas_call(
        body, grid=(m // tm, n // tn), in_specs=in_specs,
        out_specs=pl.BlockSpec((tm, tn), lambda i, j: (i, j)),
        out_shape=jax.ShapeDtypeStruct((m, n), F32),
        compiler_params=_params("parallel", "parallel"), name=name)(*args)


def _mlstm_body(gb_ref, q_ref, k_ref, v_ref, og_ref, z_ref, gcol_ref, grow_ref, cwq_ref, cwk_ref,
                cbq_ref, cbk_ref, onw_ref, y_ref, c_scr, n_scr, m_scr, extq, extk, *, q_scale):
    h = pl.program_id(1)
    c = pl.program_id(2)
    L = q_ref.shape[0]

    @pl.when(c == 0)
    def _():
        c_scr[...] = jnp.zeros_like(c_scr)
        n_scr[...] = jnp.zeros_like(n_scr)
        m_scr[...] = jnp.zeros_like(m_scr)
        extq[L:L + 8, :] = jnp.zeros((8, extq.shape[1]), F32)
        extk[L:L + 8, :] = jnp.zeros((8, extk.shape[1]), F32)

    def conv_silu(ext, x_ref, cw_ref, cb_ref):
        ext[0:8, :] = ext[L:L + 8, :]
        ext[8:L + 8, :] = x_ref[...]
        acc = cb_ref[...] + cw_ref[0:1, :] * ext[pl.ds(8 - CONV_W + 1, L), :]
        for j in range(1, CONV_W):
            acc = acc + cw_ref[j:j + 1, :] * ext[pl.ds(8 - CONV_W + 1 + j, L), :]
        return _silu(acc)

    q = conv_silu(extq, q_ref, cwq_ref, cbq_ref) * q_scale
    k = conv_silu(extk, k_ref, cwk_ref, cbk_ref)
    v = v_ref[...]

    bi = gb_ref[0, h]
    bf = gb_ref[1, h]
    gcol = gcol_ref[...]
    grow = grow_ref[...]
    li_col = gcol[:, 0:1] + bi
    li_row = grow[0:1, :] + bi
    lf_col = -_softplus(-(gcol[:, 1:2] + bf))
    lf_row = -_softplus(-(grow[1:2, :] + bf))

    t_io = lax.broadcasted_iota(jnp.int32, (L, L), 0)
    s_io = lax.broadcasted_iota(jnp.int32, (L, L), 1)
    causal = s_io <= t_io
    b_col = jnp.sum(jnp.where(causal, lf_row, 0.0), axis=1, keepdims=True)
    b_row = jnp.sum(jnp.where(t_io <= s_io, lf_col, 0.0), axis=0, keepdims=True)
    b_last = jnp.sum(lf_row, axis=1, keepdims=True)
    m_st = m_scr[...]

    dmat = jnp.where(causal, b_col - b_row + li_row, NEG)
    inter = b_col + m_st
    m_t = jnp.maximum(inter, jnp.max(dmat, axis=1, keepdims=True))
    w_intra = jnp.exp(dmat - m_t)
    w_inter = jnp.exp(inter - m_t)
    a = w_intra * _dot_nt(q, k)
    c_st = c_scr[...]
    n_st = n_scr[...]
    num = _dot(a, v) + w_inter * _dot(q, c_st)
    qn = jnp.sum(a, axis=1, keepdims=True) + w_inter * jnp.sum(q * n_st, axis=1, keepdims=True)
    hc = num / jnp.maximum(jnp.abs(qn), jnp.exp(-m_t))

    g_row = b_last - b_row + li_row
    g_col = b_last - b_col + li_col
    m_new = jnp.maximum(b_last + m_st, jnp.max(g_row, axis=1, keepdims=True))
    ws_col = jnp.exp(g_col - m_new)
    decay = jnp.exp(b_last + m_st - m_new)
    kw = k * ws_col
    c_scr[...] = decay * c_st + _dot_tn(kw, v)
    n_scr[...] = decay * n_st + jnp.sum(kw, axis=0, keepdims=True)
    m_scr[...] = m_new

    hn = _head_rms(hc, onw_ref[...])
    y_ref[...] = (hn * _sigmoid(og_ref[...]) * _silu(z_ref[...])).astype(y_ref.dtype)


def _mlstm_layer(x2d, hn, bsz, seq, w_in, conv_w, conv_b, gate_b, out_norm_w, w_out):
    t, d = x2d.shape
    heads = gate_b.shape[-1]
    qk = conv_w.shape[-1] // 2
    inner = out_norm_w.shape[-1]
    dqk, dv = qk // heads, inner // heads
    n_main = 2 * qk + 3 * inner
    L = A_CHUNK
    nc = seq // L
    assert dqk % LANES == 0 and dv % LANES == 0 and 2 * heads <= LANES

    u = _matmul(hn, w_in[:, :n_main].astype(BF16), name="a_in").reshape(bsz, seq, n_main)
    w_g = jnp.pad(w_in[:, n_main:], ((0, 0), (0, LANES - 2 * heads))).astype(BF16)
    gts = _matmul(hn, w_g, name="a_gates")[:, :2 * heads].reshape(bsz, nc, L, 2, heads)
    g_col = jnp.transpose(gts, (0, 4, 1, 2, 3))
    g_row = jnp.transpose(gts, (0, 4, 1, 3, 2))

    kq0, kv0, ko0, kz0 = qk // dqk, 2 * qk // dv, (2 * qk + inner) // dv, (2 * qk + 2 * inner) // dv
    blk = lambda w, off: pl.BlockSpec((None, L, w), lambda b, h, c: (b, c, off + h))
    par = lambda r, w, off: pl.BlockSpec((r, w), lambda b, h, c: (0, off + h))
    y = pl.pallas_call(
        functools.partial(_mlstm_body, q_scale=dqk ** -0.5),
        grid=(bsz, heads, nc),
        in_specs=[pl.BlockSpec(memory_space=pltpu.SMEM),
                  blk(dqk, 0), blk(dqk, kq0), blk(dv, kv0), blk(dv, ko0), blk(dv, kz0),
                  pl.BlockSpec((None, None, None, L, 2), lambda b, h, c: (b, h, c, 0, 0)),
                  pl.BlockSpec((None, None, None, 2, L), lambda b, h, c: (b, h, c, 0, 0)),
                  par(CONV_W, dqk, 0), par(CONV_W, dqk, kq0), par(1, dqk, 0), par(1, dqk, kq0),
                  par(1, dv, 0)],
        out_specs=pl.BlockSpec((None, L, dv), lambda b, h, c: (b, c, h)),
        out_shape=jax.ShapeDtypeStruct((bsz, seq, inner), BF16),
        scratch_shapes=[pltpu.VMEM((dqk, dv), F32), pltpu.VMEM((1, dqk), F32), pltpu.VMEM((1, 1), F32),
                        pltpu.VMEM((L + 8, dqk), F32), pltpu.VMEM((L + 8, dqk), F32)],
        compiler_params=_params("parallel", "parallel", "arbitrary"), name="a_mlstm",
    )(gate_b, u, u, u, u, u, g_col, g_row, conv_w, conv_w, conv_b.reshape(1, -1), conv_b.reshape(1, -1),
      out_norm_w.reshape(1, -1))
    return _matmul(y.reshape(t, inner), w_out.astype(BF16), resid=x2d, name="a_out")


def _rglru_body(x_ref, z_ref, cw_ref, cb_ref, wa_ref, wx_ref, ba_ref, bx_ref, lam_ref, y_ref,
                ext, a_scr, b_scr, h_scr, *, nblk, pb, chunk):
    s = pl.program_id(1)
    ts = x_ref.shape[0]

    @pl.when(s == 0)
    def _():
        ext[ts:ts + 8, :] = jnp.zeros((8, ext.shape[1]), F32)
        h_scr[...] = jnp.zeros_like(h_scr)

    ext[0:8, :] = ext[ts:ts + 8, :]
    ext[8:ts + 8, :] = x_ref[...]
    for n in range(nblk):
        cs = slice(n * pb, (n + 1) * pb)
        xc = cb_ref[:, cs] + cw_ref[0:1, cs] * ext[pl.ds(8 - CONV_W + 1, ts), cs]
        for j in range(1, CONV_W):
            xc = xc + cw_ref[j:j + 1, cs] * ext[pl.ds(8 - CONV_W + 1 + j, ts), cs]
        r = _sigmoid(_dot(xc, wa_ref[n]) + ba_ref[:, cs])
        ig = _sigmoid(_dot(xc, wx_ref[n]) + bx_ref[:, cs])
        log_a = (-C_EXP) * _softplus(-lam_ref[:, cs]) * r
        a = jnp.exp(log_a)
        a_scr[:, cs] = a
        b_scr[:, cs] = jnp.sqrt(-jnp.tanh(log_a) * (a * a + 1.0)) * (ig * xc)

    width = a_scr.shape[1]
    for c0 in range(0, width, chunk):
        cs = slice(c0, c0 + chunk)

        def row(t, h, cs=cs):
            h = a_scr[pl.ds(t, 1), cs] * h + b_scr[pl.ds(t, 1), cs]
            b_scr[pl.ds(t, 1), cs] = h
            return h

        h_scr[:, cs] = lax.fori_loop(0, ts, row, h_scr[:, cs], unroll=8)

    y_ref[...] = (b_scr[...] * _silu(z_ref[...])).astype(y_ref.dtype)


def _pad_blocks(w, nblk, blk, pb, axis):
    shp = w.shape
    w = w.reshape(shp[:axis] + (nblk, blk) + shp[axis + 1:])
    pad = [(0, 0)] * w.ndim
    pad[axis + 1] = (0, pb - blk)
    w = jnp.pad(w, pad)
    return w.reshape(shp[:axis] + (nblk * pb,) + shp[axis + 1:])


def _rglru_layer(x2d, hn, bsz, seq, w_in, conv_w, conv_b, w_a, b_a, w_x, b_x, lam, w_out):
    t, d = x2d.shape
    width = conv_w.shape[-1]
    nblk = w_a.shape[0]
    blk = width // nblk
    pb = -(-blk // LANES) * LANES
    wp = nblk * pb
    padv = lambda v: _pad_blocks(v.reshape(-1, width), nblk, blk, pb, 1)

    w_in_p = jnp.concatenate([_pad_blocks(w_in[:, :width], nblk, blk, pb, 1),
                              _pad_blocks(w_in[:, width:], nblk, blk, pb, 1)], axis=1).astype(BF16)
    u = _matmul(hn, w_in_p, name="c_in").reshape(bsz, seq, 2 * wp)
    padw = lambda w: jnp.pad(w, ((0, 0), (0, pb - blk), (0, pb - blk))).astype(BF16)

    ts = _pick(seq, (128, 64, 8))
    chunk = _pick(wp, (1536, 1024, 768, 512, 384, 256, 128))
    full = lambda r: pl.BlockSpec((r, wp), lambda b, s: (0, 0))
    y = pl.pallas_call(
        functools.partial(_rglru_body, nblk=nblk, pb=pb, chunk=chunk),
        grid=(bsz, seq // ts),
        in_specs=[pl.BlockSpec((None, ts, wp), lambda b, s: (b, s, 0)),
                  pl.BlockSpec((None, ts, wp), lambda b, s: (b, s, 1)),
                  full(CONV_W), full(1),
                  pl.BlockSpec((nblk, pb, pb), lambda b, s: (0, 0, 0)),
                  pl.BlockSpec((nblk, pb, pb), lambda b, s: (0, 0, 0)),
                  full(1), full(1), full(1)],
        out_specs=pl.BlockSpec((None, ts, wp), lambda b, s: (b, s, 0)),
        out_shape=jax.ShapeDtypeStruct((bsz, seq, wp), BF16),
        scratch_shapes=[pltpu.VMEM((ts + 8, wp), F32), pltpu.VMEM((ts, wp), F32), pltpu.VMEM((ts, wp), F32),
                        pltpu.VMEM((1, wp), F32)],
        compiler_params=_params("parallel", "arbitrary"), name="c_rglru",
    )(u, u, padv(conv_w), padv(conv_b), padw(w_a), padw(w_x), padv(b_a), padv(b_x), padv(lam))
    w_out_p = _pad_blocks(w_out, nblk, blk, pb, 0).astype(BF16)
    return _matmul(y.reshape(t, wp), w_out_p, resid=x2d, name="c_out")


def _dilated_body(q_ref, kp_ref, kc_ref, vp_ref, vc_ref, qw_ref, kw_ref, o_ref, lse_ref, *, heads, hd, dil,
                  n_back, slopes):
    i = pl.program_id(2)
    tq = q_ref.shape[0]
    qi = lax.broadcasted_iota(jnp.int32, (tq, 2 * tq), 0)
    kj = lax.broadcasted_iota(jnp.int32, (tq, 2 * tq), 1)
    steps = qi + tq - kj
    valid = (steps >= 0) & (steps <= n_back) & ((kj >= tq) | (i > 0))
    dist = (steps * dil).astype(F32)
    lane = lax.broadcasted_iota(jnp.int32, (tq, LANES), 1)
    lse_tile = jnp.zeros((tq, LANES), F32)
    scale = hd ** -0.5
    for h in range(heads):
        hs = slice(h * hd, (h + 1) * hd)
        qn = _head_rms(q_ref[:, hs], qw_ref[...])
        kn = _head_rms(jnp.concatenate([kp_ref[:, hs], kc_ref[:, hs]], axis=0), kw_ref[...])
        v2 = jnp.concatenate([vp_ref[:, hs], vc_ref[:, hs]], axis=0)
        s = _dot_nt(qn, kn) * scale - slopes[h] * dist
        s = jnp.where(valid, s, NEG)
        m = jnp.max(s, axis=1, keepdims=True)
        p = jnp.exp(s - m)
        den = jnp.sum(p, axis=1, keepdims=True)
        o_ref[:, hs] = _dot(p / den, v2)
        lse_tile = jnp.where(lane == h, m + jnp.log(den), lse_tile)
    lse_ref[...] = lse_tile


def _dmix_body(o0_ref, o1_ref, o2_ref, l0_ref, l1_ref, l2_ref, z_ref, y_ref, *, heads, hd):
    o_refs = (o0_ref, o1_ref, o2_ref)
    l_refs = (l0_ref, l1_ref, l2_ref)
    for h in range(heads):
        hs = slice(h * hd, (h + 1) * hd)
        ls = [l[:, h:h + 1] for l in l_refs]
        mx = functools.reduce(jnp.maximum, ls)
        es = [jnp.exp(l - mx) for l in ls]
        tot = functools.reduce(lambda a, b: a + b, es)
        o = sum((e / tot) * o_ref[:, hs] for e, o_ref in zip(es, o_refs))
        y_ref[:, hs] = (o * _silu(z_ref[:, hs])).astype(y_ref.dtype)


def _dilated_layer(x2d, hn, bsz, seq, w_in, q_norm_w, k_norm_w, w_out):
    t, d = x2d.shape
    hd = q_norm_w.shape[-1]
    inner = w_out.shape[0]
    heads = inner // hd
    n_pat = len(D_PATTERNS)
    cols = 3 * n_pat * inner + inner
    assert heads <= LANES and n_pat == 3
    slopes = tuple(float(2.0 ** (-8.0 * (h + 1) / heads)) for h in range(heads))
    u = _matmul(hn, w_in.astype(BF16), name="d_in")
    ncb = cols // inner

    outs, lses = [], []
    for g, (window, dil) in enumerate(D_PATTERNS):
        n_back = window // dil
        s_sub = seq // dil
        tq = D_QBLK
        assert n_back <= tq and s_sub % tq == 0
        ug = u.reshape(bsz, s_sub, dil * cols)
        cur = lambda cb: pl.BlockSpec((None, tq, inner), lambda b, r, i, cb=cb: (b, i, r * ncb + cb))
        prev = lambda cb: pl.BlockSpec((None, tq, inner),
                                       lambda b, r, i, cb=cb: (b, jnp.maximum(i - 1, 0), r * ncb + cb))
        wspec = pl.BlockSpec((1, hd), lambda b, r, i: (0, 0))
        o, lse = pl.pallas_call(
            functools.partial(_dilated_body, heads=heads, hd=hd, dil=dil, n_back=n_back, slopes=slopes),
            grid=(bsz, dil, s_sub // tq),
            in_specs=[cur(3 * g), prev(3 * g + 1), cur(3 * g + 1), prev(3 * g + 2), cur(3 * g + 2), wspec, wspec],
            out_specs=[pl.BlockSpec((None, tq, inner), lambda b, r, i: (b, i, r)),
                       pl.BlockSpec((None, tq, LANES), lambda b, r, i: (b, i, r))],
            out_shape=[jax.ShapeDtypeStruct((bsz, s_sub, dil * inner), F32),
                       jax.ShapeDtypeStruct((bsz, s_sub, dil * LANES), F32)],
            compiler_params=_params("parallel", "parallel", "parallel"), name=f"d_attn{g}",
        )(ug, ug, ug, ug, ug, q_norm_w.reshape(1, hd), k_norm_w.reshape(1, hd))
        outs.append(o.reshape(t, inner))
        lses.append(lse.reshape(t, LANES))

    tt = _pick(t, (256, 128, 8))
    ospec = pl.BlockSpec((tt, inner), lambda i: (i, 0))
    lspec = pl.BlockSpec((tt, LANES), lambda i: (i, 0))
    y = pl.pallas_call(
        functools.partial(_dmix_body, heads=heads, hd=hd),
        grid=(t // tt,),
        in_specs=[ospec, ospec, ospec, lspec, lspec, lspec, pl.BlockSpec((tt, inner), lambda i: (i, ncb - 1))],
        out_specs=ospec, out_shape=jax.ShapeDtypeStruct((t, inner), BF16),
        compiler_params=_params("parallel"), name="d_mix",
    )(*outs, *lses, u)
    return _matmul(y, w_out.astype(BF16), resid=x2d, name="d_out")


def _nsa_prep_body(k0_ref, v0_ref, k1_ref, v1_ref, k2_ref, v2_ref, pe_ref, wk_ref, wv_ref, knw_ref,
                   kc_ref, vc_ref, ks_ref, vs_ref, kw_ref, vw_ref, *, stride):
    n_c = kc_ref.shape[0]

    def compress(x_ref, w_ref, pe):
        p1 = jnp.zeros(kc_ref.shape, F32)
        p2 = jnp.zeros(kc_ref.shape, F32)
        for l in range(stride):
            x = x_ref[pl.ds(l, n_c, stride=stride), :]
            p1 = p1 + _dot(x + pe[l:l + 1, :], w_ref[l])
            p2 = p2 + _dot(x + pe[stride + l:stride + l + 1, :], w_ref[stride + l])
        return p1 + pltpu.roll(p2, n_c - 1, 0)

    kc_ref[...] = _head_rms(compress(k0_ref, wk_ref, pe_ref[0]), knw_ref[0:1, :])
    vc_ref[...] = compress(v0_ref, wv_ref, pe_ref[1])

    seq, hd = k1_ref.shape
    pos = lax.broadcasted_iota(jnp.int32, (seq, hd), 0)
    lane = lax.broadcasted_iota(jnp.int32, (seq, hd), 1)
    split_shift = NSA_POS_SPLIT.bit_length() - 1
    pos_hi = ((pos >> split_shift) << split_shift).astype(F32)
    pos_lo = (pos & (NSA_POS_SPLIT - 1)).astype(F32)
    pos_cols = jnp.where(lane < 3, pos_hi, jnp.where(lane < NSA_SEL_OFF, pos_lo, 0.0))
    sel_blk = pos >> (B_SEL_LEN.bit_length() - 1)
    onehot = jnp.where(lane - NSA_SEL_OFF == sel_blk, NEG, 0.0)
    ones = jnp.ones((seq, hd), vs_ref.dtype)
    ks_ref[:, 0:hd] = _head_rms(k1_ref[...], knw_ref[1:2, :]).astype(ks_ref.dtype)
    ks_ref[:, hd:2 * hd] = (pos_cols + onehot).astype(ks_ref.dtype)
    vs_ref[:, 0:hd] = v1_ref[...].astype(vs_ref.dtype)
    vs_ref[:, hd:2 * hd] = ones
    kw_ref[:, 0:hd] = _head_rms(k2_ref[...], knw_ref[2:3, :]).astype(kw_ref.dtype)
    kw_ref[:, hd:2 * hd] = pos_cols.astype(kw_ref.dtype)
    vw_ref[:, 0:hd] = v2_ref[...].astype(vw_ref.dtype)
    vw_ref[:, hd:2 * hd] = ones


def _nsa_body(sl_ref, sp_ref, q_ref, z_ref, g_ref, qw_ref, kc_ref, vc_ref, ks_ref, vs_ref, kw_ref, vw_ref, ov_ref,
              y_ref, q2, acc, m_scr, o_scr, *, rep, hd, n_top, n_blk, kb_sel):
    g = pl.program_id(1)
    i = pl.program_id(2)
    tq = q_ref.shape[0]
    n_cmp = kc_ref.shape[0]
    t0 = i * tq
    scale = hd ** -0.5
    rows = lambda r: slice(r * tq, (r + 1) * tq)

    for r in range(rep):
        q2[rows(r), 0:hd] = (_head_rms(q_ref[:, r * hd:(r + 1) * hd], qw_ref[...]) * scale).astype(q2.dtype)
    gts = _sigmoid(g_ref[...])
    t_col = t0 + lax.broadcasted_iota(jnp.int32, (tq, 1), 0)

    cmp_end = lax.broadcasted_iota(jnp.int32, (tq, n_cmp), 1) * B_CMP_STRIDE + (B_CMP_LEN - 1)
    dist_ci = t_col - cmp_end
    mask_c = dist_ci >= 0
    dist_c = dist_ci.astype(F32)
    s_all = _dot_nt(q2[:, 0:hd], kc_ref[...])
    psum = jnp.zeros((tq, n_cmp), F32)
    ps = []
    for r in range(rep):
        s = s_all[rows(r), :] - sl_ref[g, r] * dist_c
        s = jnp.where(mask_c, s, NEG)
        m = jnp.max(s, axis=1, keepdims=True)
        p = jnp.where(mask_c, jnp.exp(s - m), 0.0)
        den = jnp.sum(p, axis=1, keepdims=True)
        p = p / jnp.where(den > 0, den, 1.0)
        psum = psum + p
        ps.append(p.astype(BF16))
    o_c = _dot(jnp.concatenate(ps, axis=0), vc_ref[...])
    for r in range(rep):
        o_scr[rows(r), :] = gts[:, 3 * r:3 * r + 1] * o_c[rows(r), :]

    imp = jnp.dot(psum, ov_ref[...], preferred_element_type=F32, precision=lax.Precision.HIGHEST)
    lane = lax.broadcasted_iota(jnp.int32, (tq, hd), 1)
    blk = lane - NSA_SEL_OFF
    in_range = (blk >= 0) & (blk < n_blk)
    cur = t_col >> (B_SEL_LEN.bit_length() - 1)
    forced = (blk == 0) | (blk == cur) | (blk == cur - 1)
    imp = jnp.where(forced, imp + FORCE, imp)
    imp = jnp.where(in_range & (blk * B_SEL_LEN <= t_col), imp, NEG)
    cnt = jnp.zeros((tq, hd), F32)
    for mb in range(n_blk):
        col = imp[:, NSA_SEL_OFF + mb:NSA_SEL_OFF + mb + 1]
        beats = (col > imp) | ((col == imp) & (blk > mb))
        cnt = cnt + jnp.where(beats, 1.0, 0.0)
    not_sel = jnp.where((cnt < n_top) & (imp > NEG / 2), 0.0, 1.0)
    for r in range(rep):
        piece = jnp.where((lane == 0) | (lane == 3), sp_ref[g, 3 * r],
                          jnp.where((lane == 1) | (lane == 4), sp_ref[g, 3 * r + 1],
                                    jnp.where((lane == 2) | (lane == 5), sp_ref[g, 3 * r + 2], 0.0)))
        q2[rows(r), hd:2 * hd] = jnp.where(in_range, not_sel, piece).astype(q2.dtype)

    def gate(r, branch):
        return gts[:, 3 * r + branch:3 * r + branch + 1]

    acc[...] = jnp.zeros_like(acc)
    m_scr[...] = jnp.full(m_scr.shape, NEG, F32)
    n_chunk = kb_sel // hd

    def sel_block(start, bias):
        kblk = ks_ref[pl.ds(start, kb_sel), :]
        vblk = vs_ref[pl.ds(start, kb_sel), :]
        s_all = lax.dot_general(q2[...], kblk, (((1,), (1,)), ((), ())), preferred_element_type=F32)
        for r in range(rep):
            s = s_all[rows(r), :]
            if bias is not None:
                s = s + bias
            mx = functools.reduce(jnp.maximum, [s[:, c * hd:(c + 1) * hd] for c in range(n_chunk)])
            m_old = m_scr[rows(r), :]
            m_new = jnp.maximum(m_old, jnp.max(mx, axis=1, keepdims=True))
            alpha = jnp.exp(m_old - m_new)
            p = jnp.exp(s - jnp.concatenate([m_new] * n_chunk, axis=1)).astype(BF16)
            acc[rows(r), :] = (jnp.concatenate([alpha, alpha], axis=1) * acc[rows(r), :]
                               + jnp.dot(p, vblk, preferred_element_type=F32))
            m_scr[rows(r), :] = m_new

    def sel_step(kb, carry):
        sel_block(pl.multiple_of(kb * kb_sel, kb_sel), None)
        return carry

    n_full = lax.div(t0, kb_sel)
    lax.fori_loop(0, n_full, sel_step, 0)
    d_start = pl.multiple_of(n_full * kb_sel, kb_sel)
    d_pos = d_start + lax.broadcasted_iota(jnp.int32, (tq, kb_sel), 1)
    sel_block(d_start, jnp.where(d_pos <= t_col, 0.0, NEG))
    for r in range(rep):
        o_scr[rows(r), :] += gate(r, 1) * (acc[rows(r), 0:hd] / acc[rows(r), hd:2 * hd])

    n_w = B_WIN + tq
    w_start = pl.multiple_of(jnp.maximum(t0 - B_WIN, 0), tq)
    w_dist = t_col - (w_start + lax.broadcasted_iota(jnp.int32, (tq, n_w), 1))
    w_bias = jnp.where((w_dist >= 0) & (w_dist < B_WIN), 0.0, NEG)
    kblk = kw_ref[pl.ds(w_start, n_w), :]
    vblk = vw_ref[pl.ds(w_start, n_w), :]
    s_all = lax.dot_general(q2[...], kblk, (((1,), (1,)), ((), ())), preferred_element_type=F32)
    for r in range(rep):
        s = s_all[rows(r), :] + w_bias
        p = jnp.exp(s - jnp.max(s, axis=1, keepdims=True)).astype(BF16)
        res = jnp.dot(p, vblk, preferred_element_type=F32)
        o_scr[rows(r), :] += gate(r, 2) * (res[:, 0:hd] / res[:, hd:2 * hd])

    for r in range(rep):
        hs = slice(r * hd, (r + 1) * hd)
        y_ref[:, hs] = (o_scr[rows(r), :] * _silu(z_ref[:, hs])).astype(y_ref.dtype)


def _nsa_layer(x2d, hn, bsz, seq, w_in, cmp_pe, cmp_wk, cmp_wv, q_norm_w, k_norm_w, w_out):
    t, d = x2d.shape
    hd = q_norm_w.shape[-1]
    inner = w_out.shape[0]
    heads = inner // hd
    rep = heads // B_KV
    kvw = B_KV * hd
    n_main = 2 * inner + 6 * kvw
    assert B_CMP_LEN == 2 * B_CMP_STRIDE and seq % B_CMP_STRIDE == 0 and 3 * heads <= LANES
    n_cmp = seq // B_CMP_STRIDE
    n_blk = seq // B_SEL_LEN
    n_top = min(B_N_SEL, n_blk)
    tq = 128
    kb_sel = 256
    assert hd == LANES and NSA_SEL_OFF + n_blk <= hd and NSA_POS_SPLIT == hd
    assert kb_sel % tq == 0 and seq % kb_sel == 0 and B_WIN % tq == 0 and seq >= B_WIN + tq

    u = _matmul(hn, w_in[:, :n_main].astype(BF16), name="b_in").reshape(bsz, seq, n_main)
    w_g = jnp.pad(w_in[:, n_main:], ((0, 0), (0, LANES - 3 * heads))).astype(BF16)
    gts = _matmul(hn, w_g, name="b_gates")[:, :3 * heads].reshape(bsz, seq, B_KV, 3 * rep)
    gts = jnp.transpose(gts, (0, 2, 1, 3))

    kv0 = 2 * inner // hd
    kvspec = lambda br, kv: pl.BlockSpec((None, seq, hd), lambda b, g: (b, 0, kv0 + (br * 2 + kv) * B_KV + g))
    cspec = pl.BlockSpec((None, None, n_cmp, hd), lambda b, g: (b, g, 0, 0))
    sspec = pl.BlockSpec((None, None, seq, 2 * hd), lambda b, g: (b, g, 0, 0))
    wspec = pl.BlockSpec((B_CMP_LEN, hd, hd), lambda b, g: (0, 0, 0))
    c_sh = jax.ShapeDtypeStruct((bsz, B_KV, n_cmp, hd), F32)
    s_sh = jax.ShapeDtypeStruct((bsz, B_KV, seq, 2 * hd), BF16)
    kc, vc, ks, vs, kw, vw = pl.pallas_call(
        functools.partial(_nsa_prep_body, stride=B_CMP_STRIDE),
        grid=(bsz, B_KV),
        in_specs=[kvspec(0, 0), kvspec(0, 1), kvspec(1, 0), kvspec(1, 1), kvspec(2, 0), kvspec(2, 1),
                  pl.BlockSpec((2, B_CMP_LEN, hd), lambda b, g: (0, 0, 0)), wspec, wspec,
                  pl.BlockSpec((3, hd), lambda b, g: (0, 0))],
        out_specs=[cspec, cspec, sspec, sspec, sspec, sspec],
        out_shape=[c_sh, c_sh, s_sh, s_sh, s_sh, s_sh],
        compiler_params=_params("parallel", "parallel"), name="b_prep",
    )(u, u, u, u, u, u, cmp_pe, cmp_wk.reshape(B_CMP_LEN, hd, hd).astype(BF16),
      cmp_wv.reshape(B_CMP_LEN, hd, hd).astype(BF16), k_norm_w)

    cmp_start = np.arange(n_cmp) * B_CMP_STRIDE
    sel_start = np.arange(n_blk) * B_SEL_LEN
    ov = np.clip(np.minimum(cmp_start[:, None] + B_CMP_LEN, sel_start[None, :] + B_SEL_LEN)
                 - np.maximum(cmp_start[:, None], sel_start[None, :]), 0, None) / B_CMP_LEN
    ov[cmp_start + B_CMP_LEN > seq] = 0.0
    ov = np.pad(ov, ((0, 0), (NSA_SEL_OFF, hd - NSA_SEL_OFF - n_blk)))
    slopes_np = (2.0 ** (-8.0 * np.arange(1, heads + 1) / heads)).astype(np.float32)
    slopes = jnp.asarray(slopes_np).reshape(B_KV, rep)
    to_bf16 = lambda v: v.astype(ml_dtypes.bfloat16).astype(np.float32)
    s_hi = to_bf16(slopes_np)
    s_mid = to_bf16(slopes_np - s_hi)
    s_lo = to_bf16(slopes_np - s_hi - s_mid)
    pieces = jnp.asarray(np.stack([s_hi, s_mid, s_lo], axis=-1).reshape(B_KV, 3 * rep))

    rw = rep * hd
    cspec3 = pl.BlockSpec((None, None, n_cmp, hd), lambda b, g, i: (b, g, 0, 0))
    sspec3 = pl.BlockSpec((None, None, seq, 2 * hd), lambda b, g, i: (b, g, 0, 0))
    y = pl.pallas_call(
        functools.partial(_nsa_body, rep=rep, hd=hd, n_top=n_top, n_blk=n_blk, kb_sel=kb_sel),
        grid=(bsz, B_KV, seq // tq),
        in_specs=[pl.BlockSpec(memory_space=pltpu.SMEM), pl.BlockSpec(memory_space=pltpu.SMEM),
                  pl.BlockSpec((None, tq, rw), lambda b, g, i: (b, i, g)),
                  pl.BlockSpec((None, tq, rw), lambda b, g, i: (b, i, inner // rw + g)),
                  pl.BlockSpec((None, None, tq, 3 * rep), lambda b, g, i: (b, g, i, 0)),
                  pl.BlockSpec((1, hd), lambda b, g, i: (0, 0)),
                  cspec3, cspec3, sspec3, sspec3, sspec3, sspec3,
                  pl.BlockSpec((n_cmp, hd), lambda b, g, i: (0, 0))],
        out_specs=pl.BlockSpec((None, tq, rw), lambda b, g, i: (b, i, g)),
        out_shape=jax.ShapeDtypeStruct((bsz, seq, inner), BF16),
        scratch_shapes=[pltpu.VMEM((rep * tq, 2 * hd), BF16), pltpu.VMEM((rep * tq, 2 * hd), F32),
                        pltpu.VMEM((rep * tq, hd), F32), pltpu.VMEM((rep * tq, hd), F32)],
        compiler_params=_params("parallel", "parallel", "arbitrary"), name="b_attn",
    )(slopes, pieces, u, u, gts, q_norm_w.reshape(1, hd), kc, vc, ks, vs, kw, vw, jnp.asarray(ov, dtype=F32))
    return _matmul(y.reshape(t, inner), w_out.astype(BF16), resid=x2d, name="b_out")


def kernel(x, norm_w, a_w_in, a_conv_w, a_conv_b, a_gate_b, a_out_norm_w, a_w_out, b_w_in, b_cmp_pe, b_cmp_wk, b_cmp_wv, b_q_norm_w, b_k_norm_w, b_w_out, c_w_in, c_conv_w, c_conv_b, c_w_a, c_b_a, c_w_x, c_b_x, c_lambda, c_w_out, d_w_in, d_q_norm_w, d_k_norm_w, d_w_out):
    bsz, seq, d = x.shape
    depth = norm_w.shape[0]
    x2d = x.reshape(bsz * seq, d)
    for layer in range(depth):
        kind, j = layer % 4, layer // 4
        hn = _rmsnorm(x2d, norm_w[layer])
        if kind == 0:
            x2d = _mlstm_layer(x2d, hn, bsz, seq, a_w_in[j], a_conv_w[j], a_conv_b[j], a_gate_b[j],
                               a_out_norm_w[j], a_w_out[j])
        elif kind == 1:
            x2d = _nsa_layer(x2d, hn, bsz, seq, b_w_in[j], b_cmp_pe[j], b_cmp_wk[j], b_cmp_wv[j], b_q_norm_w[j],
                             b_k_norm_w[j], b_w_out[j])
        elif kind == 2:
            x2d = _rglru_layer(x2d, hn, bsz, seq, c_w_in[j], c_conv_w[j], c_conv_b[j], c_w_a[j], c_b_a[j],
                               c_w_x[j], c_b_x[j], c_lambda[j], c_w_out[j])
        else:
            x2d = _dilated_layer(x2d, hn, bsz, seq, d_w_in[j], d_q_norm_w[j], d_k_norm_w[j], d_w_out[j])
    return x2d.reshape(bsz, seq, d)
```

```python
import functools
import math

import ml_dtypes
import numpy as np
import jax
import jax.numpy as jnp
from jax import lax
from jax.experimental import pallas as pl
from jax.experimental.pallas import tpu as pltpu

F32 = jnp.float32
BF16 = jnp.bfloat16

EPS = 1e-6
NEG = -1e30
FORCE = 1e4
CONV_W = 4

LANES = 128
SUBLANES = 8
V7X_VMEM_LIMIT = 56 * 1024 * 1024

A_CHUNK = 128
B_CMP_LEN = 32
B_CMP_STRIDE = 16
B_SEL_LEN = 64
B_N_SEL = 16
B_WIN = 512
B_KV = 4
NSA_POS_SPLIT = 128
NSA_SEL_OFF = 6
C_BLOCKS = 16
C_EXP = 8.0
D_PATTERNS = ((128, 1), (512, 4), (2048, 16))
D_QBLK = 128


def _params(*sem):
    return pltpu.CompilerParams(dimension_semantics=sem, vmem_limit_bytes=V7X_VMEM_LIMIT)


def _sigmoid(x):
    return 0.5 * jnp.tanh(0.5 * x) + 0.5


def _silu(x):
    return x * _sigmoid(x)


def _softplus(x):
    return jnp.maximum(x, 0.0) + jnp.log1p(jnp.exp(-jnp.abs(x)))


def _head_rms(x, w):
    return x * lax.rsqrt(jnp.mean(x * x, axis=-1, keepdims=True) + EPS) * w


def _dot(a, b):
    return jnp.dot(a.astype(BF16), b.astype(BF16), preferred_element_type=F32)


def _dot_nt(a, b):
    return lax.dot_general(a.astype(BF16), b.astype(BF16), (((1,), (1,)), ((), ())),
                           preferred_element_type=F32)


def _dot_tn(a, b):
    return lax.dot_general(a.astype(BF16), b.astype(BF16), (((0,), (0,)), ((), ())),
                           preferred_element_type=F32)


def _pick(n, cands):
    for c in cands:
        if n % c == 0:
            return c
    raise ValueError(f"no tile for {n}")


def _rmsnorm_body(x_ref, w_ref, o_ref):
    x = x_ref[...]
    o_ref[...] = _head_rms(x, w_ref[...]).astype(o_ref.dtype)


def _rmsnorm(x2d, w):
    t, d = x2d.shape
    tm = _pick(t, (256, 128, 8))
    return pl.pallas_call(
        _rmsnorm_body, grid=(t // tm,),
        in_specs=[pl.BlockSpec((tm, d), lambda i: (i, 0)), pl.BlockSpec((1, d), lambda i: (0, 0))],
        out_specs=pl.BlockSpec((tm, d), lambda i: (i, 0)),
        out_shape=jax.ShapeDtypeStruct((t, d), BF16),
        compiler_params=_params("parallel"), name="rmsnorm")(x2d, w.reshape(1, d))


def _mm_body(a_ref, w_ref, *rest):
    if len(rest) == 3:
        r_ref, o_ref, wbf = rest
    else:
        (o_ref, wbf), r_ref = rest, None

    @pl.when(pl.program_id(1) == 0)
    def _():
        wbf[...] = w_ref[...].astype(wbf.dtype)

    acc = jnp.dot(a_ref[...], wbf[...], preferred_element_type=F32)
    o_ref[...] = acc if r_ref is None else r_ref[...] + acc


def _matmul(a, w, n=None, col0=0, resid=None, name="matmul"):
    m, k = a.shape
    n = w.shape[1] if n is None else n
    tm = _pick(m, (1024, 512, 256, 128))
    w_bytes = jnp.dtype(w.dtype).itemsize
    tn = _pick(n, tuple(c for c in (1024, 768, 512, 384, 256, 128)
                        if k * c * (w_bytes + 2) + 2 * tm * k * 2 + (4 if resid is not None else 2) * tm * c * 4
                        <= V7X_VMEM_LIMIT - (4 << 20)))
    assert col0 % tn == 0
    cb = col0 // tn
    in_specs = [pl.BlockSpec((tm, k), lambda j, i: (i, 0)),
                pl.BlockSpec((k, tn), lambda j, i: (0, cb + j), pipeline_mode=pl.Buffered(1))]
    args = [a, w]
    if resid is not None:
        in_specs.append(pl.BlockSpec((tm, tn), lambda j, i: (i, j)))
        args.append(resid)
    return pl.pallas_call(
        _mm_body, grid=(n // tn, m // tm), in_specs=in_specs,
        out_specs=pl.BlockSpec((tm, tn), lambda j, i: (i, j)),
        out_shape=jax.ShapeDtypeStruct((m, n), F32),
        scratch_shapes=[pltpu.VMEM((k, tn), BF16)],
        compiler_params=_params("parallel", "arbitrary"), name=name)(*args)


def _mlstm_body(gb_ref, q_ref, k_ref, v_ref, og_ref, z_ref, gcol_ref, grow_ref, cwq_ref, cwk_ref,
                cbq_ref, cbk_ref, onw_ref, y_ref, c_scr, n_scr, m_scr, extq, extk, *, q_scale):
    h = pl.program_id(1)
    c = pl.program_id(2)
    L = q_ref.shape[0]

    @pl.when(c == 0)
    def _():
        c_scr[...] = jnp.zeros_like(c_scr)
        n_scr[...] = jnp.zeros_like(n_scr)
        m_scr[...] = jnp.zeros_like(m_scr)
        extq[L:L + 8, :] = jnp.zeros((8, extq.shape[1]), F32)
        extk[L:L + 8, :] = jnp.zeros((8, extk.shape[1]), F32)

    def conv_silu(ext, x_ref, cw_ref, cb_ref):
        ext[0:8, :] = ext[L:L + 8, :]
        ext[8:L + 8, :] = x_ref[...]
        acc = cb_ref[...] + cw_ref[0:1, :] * ext[pl.ds(8 - CONV_W + 1, L), :]
        for j in range(1, CONV_W):
            acc = acc + cw_ref[j:j + 1, :] * ext[pl.ds(8 - CONV_W + 1 + j, L), :]
        return _silu(acc)

    q = conv_silu(extq, q_ref, cwq_ref, cbq_ref) * q_scale
    k = conv_silu(extk, k_ref, cwk_ref, cbk_ref)
    v = v_ref[...]

    bi = gb_ref[0, h]
    bf = gb_ref[1, h]
    gcol = gcol_ref[...]
    grow = grow_ref[...]
    li_col = gcol[:, 0:1] + bi
    li_row = grow[0:1, :] + bi
    lf_col = -_softplus(-(gcol[:, 1:2] + bf))
    lf_row = -_softplus(-(grow[1:2, :] + bf))

    t_io = lax.broadcasted_iota(jnp.int32, (L, L), 0)
    s_io = lax.broadcasted_iota(jnp.int32, (L, L), 1)
    causal = s_io <= t_io
    b_col = jnp.sum(jnp.where(causal, lf_row, 0.0), axis=1, keepdims=True)
    b_row = jnp.sum(jnp.where(t_io <= s_io, lf_col, 0.0), axis=0, keepdims=True)
    b_last = jnp.sum(lf_row, axis=1, keepdims=True)
    m_st = m_scr[...]

    dmat = jnp.where(causal, b_col - b_row + li_row, NEG)
    inter = b_col + m_st
    m_t = jnp.maximum(inter, jnp.max(dmat, axis=1, keepdims=True))
    w_intra = jnp.exp(dmat - m_t)
    w_inter = jnp.exp(inter - m_t)
    a = w_intra * _dot_nt(q, k)
    c_st = c_scr[...]
    n_st = n_scr[...]
    num = _dot(a, v) + w_inter * _dot(q, c_st)
    qn = jnp.sum(a, axis=1, keepdims=True) + w_inter * jnp.sum(q * n_st, axis=1, keepdims=True)
    hc = num / jnp.maximum(jnp.abs(qn), jnp.exp(-m_t))

    g_row = b_last - b_row + li_row
    g_col = b_last - b_col + li_col
    m_new = jnp.maximum(b_last + m_st, jnp.max(g_row, axis=1, keepdims=True))
    ws_col = jnp.exp(g_col - m_new)
    decay = jnp.exp(b_last + m_st - m_new)
    kw = k * ws_col
    c_scr[...] = decay * c_st + _dot_tn(kw, v)
    n_scr[...] = decay * n_st + jnp.sum(kw, axis=0, keepdims=True)
    m_scr[...] = m_new

    hn = _head_rms(hc, onw_ref[...])
    y_ref[...] = (hn * _sigmoid(og_ref[...]) * _silu(z_ref[...])).astype(y_ref.dtype)


def _mlstm_layer(x2d, hn, bsz, seq, w_in, conv_w, conv_b, gate_b, out_norm_w, w_out):
    t, d = x2d.shape
    heads = gate_b.shape[-1]
    qk = conv_w.shape[-1] // 2
    inner = out_norm_w.shape[-1]
    dqk, dv = qk // heads, inner // heads
    n_main = 2 * qk + 3 * inner
    L = A_CHUNK
    nc = seq // L
    assert dqk % LANES == 0 and dv % LANES == 0 and 2 * heads <= LANES

    u = _matmul(hn, w_in, n=n_main, name="a_in").reshape(bsz, seq, n_main)
    gts = _matmul(hn, w_in, n=LANES, col0=n_main, name="a_gates")[:, :2 * heads].reshape(bsz, nc, L, 2, heads)
    g_col = jnp.transpose(gts, (0, 4, 1, 2, 3))
    g_row = jnp.transpose(gts, (0, 4, 1, 3, 2))

    kq0, kv0, ko0, kz0 = qk // dqk, 2 * qk // dv, (2 * qk + inner) // dv, (2 * qk + 2 * inner) // dv
    blk = lambda w, off: pl.BlockSpec((None, L, w), lambda b, h, c: (b, c, off + h))
    par = lambda r, w, off: pl.BlockSpec((r, w), lambda b, h, c: (0, off + h))
    y = pl.pallas_call(
        functools.partial(_mlstm_body, q_scale=dqk ** -0.5),
        grid=(bsz, heads, nc),
        in_specs=[pl.BlockSpec(memory_space=pltpu.SMEM),
                  blk(dqk, 0), blk(dqk, kq0), blk(dv, kv0), blk(dv, ko0), blk(dv, kz0),
                  pl.BlockSpec((None, None, None, L, 2), lambda b, h, c: (b, h, c, 0, 0)),
                  pl.BlockSpec((None, None, None, 2, L), lambda b, h, c: (b, h, c, 0, 0)),
                  par(CONV_W, dqk, 0), par(CONV_W, dqk, kq0), par(1, dqk, 0), par(1, dqk, kq0),
                  par(1, dv, 0)],
        out_specs=pl.BlockSpec((None, L, dv), lambda b, h, c: (b, c, h)),
        out_shape=jax.ShapeDtypeStruct((bsz, seq, inner), BF16),
        scratch_shapes=[pltpu.VMEM((dqk, dv), F32), pltpu.VMEM((1, dqk), F32), pltpu.VMEM((1, 1), F32),
                        pltpu.VMEM((L + 8, dqk), F32), pltpu.VMEM((L + 8, dqk), F32)],
        compiler_params=_params("parallel", "parallel", "arbitrary"), name="a_mlstm",
    )(gate_b, u, u, u, u, u, g_col, g_row, conv_w, conv_w, conv_b.reshape(1, -1), conv_b.reshape(1, -1),
      out_norm_w.reshape(1, -1))
    return _matmul(y.reshape(t, inner), w_out, resid=x2d, name="a_out")


def _rglru_body(x_ref, z_ref, cw_ref, cb_ref, wa_ref, wx_ref, ba_ref, bx_ref, lam_ref, y_ref,
                ext, a_scr, b_scr, h_scr, *, nblk, pb, chunk):
    s = pl.program_id(1)
    ts = x_ref.shape[0]

    @pl.when(s == 0)
    def _():
        ext[ts:ts + 8, :] = jnp.zeros((8, ext.shape[1]), F32)
        h_scr[...] = jnp.zeros_like(h_scr)

    ext[0:8, :] = ext[ts:ts + 8, :]
    ext[8:ts + 8, :] = x_ref[...]
    for n in range(nblk):
        cs = slice(n * pb, (n + 1) * pb)
        xc = cb_ref[:, cs] + cw_ref[0:1, cs] * ext[pl.ds(8 - CONV_W + 1, ts), cs]
        for j in range(1, CONV_W):
            xc = xc + cw_ref[j:j + 1, cs] * ext[pl.ds(8 - CONV_W + 1 + j, ts), cs]
        r = _sigmoid(_dot(xc, wa_ref[n]) + ba_ref[:, cs])
        ig = _sigmoid(_dot(xc, wx_ref[n]) + bx_ref[:, cs])
        log_a = (-C_EXP) * _softplus(-lam_ref[:, cs]) * r
        a = jnp.exp(log_a)
        a_scr[:, cs] = a
        b_scr[:, cs] = jnp.sqrt(-jnp.tanh(log_a) * (a * a + 1.0)) * (ig * xc)

    width = a_scr.shape[1]
    for c0 in range(0, width, chunk):
        cs = slice(c0, c0 + chunk)

        def row(t, h, cs=cs):
            h = a_scr[pl.ds(t, 1), cs] * h + b_scr[pl.ds(t, 1), cs]
            b_scr[pl.ds(t, 1), cs] = h
            return h

        h_scr[:, cs] = lax.fori_loop(0, ts, row, h_scr[:, cs], unroll=8)

    y_ref[...] = (b_scr[...] * _silu(z_ref[...])).astype(y_ref.dtype)


def _pad_blocks(w, nblk, blk, pb, axis):
    shp = w.shape
    w = w.reshape(shp[:axis] + (nblk, blk) + shp[axis + 1:])
    pad = [(0, 0)] * w.ndim
    pad[axis + 1] = (0, pb - blk)
    w = jnp.pad(w, pad)
    return w.reshape(shp[:axis] + (nblk * pb,) + shp[axis + 1:])


def _rglru_layer(x2d, hn, bsz, seq, w_in, conv_w, conv_b, w_a, b_a, w_x, b_x, lam, w_out):
    t, d = x2d.shape
    width = conv_w.shape[-1]
    nblk = w_a.shape[0]
    blk = width // nblk
    pb = -(-blk // LANES) * LANES
    wp = nblk * pb
    padv = lambda v: _pad_blocks(v.reshape(-1, width), nblk, blk, pb, 1)

    w_in_p = jnp.concatenate([_pad_blocks(w_in[:, :width], nblk, blk, pb, 1),
                              _pad_blocks(w_in[:, width:], nblk, blk, pb, 1)], axis=1).astype(BF16)
    u = _matmul(hn, w_in_p, name="c_in").reshape(bsz, seq, 2 * wp)
    padw = lambda w: jnp.pad(w, ((0, 0), (0, pb - blk), (0, pb - blk))).astype(BF16)

    ts = _pick(seq, (128, 64, 8))
    chunk = _pick(wp, (1536, 1024, 768, 512, 384, 256, 128))
    full = lambda r: pl.BlockSpec((r, wp), lambda b, s: (0, 0))
    y = pl.pallas_call(
        functools.partial(_rglru_body, nblk=nblk, pb=pb, chunk=chunk),
        grid=(bsz, seq // ts),
        in_specs=[pl.BlockSpec((None, ts, wp), lambda b, s: (b, s, 0)),
                  pl.BlockSpec((None, ts, wp), lambda b, s: (b, s, 1)),
                  full(CONV_W), full(1),
                  pl.BlockSpec((nblk, pb, pb), lambda b, s: (0, 0, 0)),
                  pl.BlockSpec((nblk, pb, pb), lambda b, s: (0, 0, 0)),
                  full(1), full(1), full(1)],
        out_specs=pl.BlockSpec((None, ts, wp), lambda b, s: (b, s, 0)),
        out_shape=jax.ShapeDtypeStruct((bsz, seq, wp), BF16),
        scratch_shapes=[pltpu.VMEM((ts + 8, wp), F32), pltpu.VMEM((ts, wp), F32), pltpu.VMEM((ts, wp), F32),
                        pltpu.VMEM((1, wp), F32)],
        compiler_params=_params("parallel", "arbitrary"), name="c_rglru",
    )(u, u, padv(conv_w), padv(conv_b), padw(w_a), padw(w_x), padv(b_a), padv(b_x), padv(lam))
    w_out_p = _pad_blocks(w_out, nblk, blk, pb, 0).astype(BF16)
    return _matmul(y.reshape(t, wp), w_out_p, resid=x2d, name="c_out")


def _dilated_body(sl_ref, *refs, patterns, tq, mix_rows):
    n_pat = len(patterns)
    qkv = [refs[5 * g:5 * g + 5] for g in range(n_pat)]
    z_ref, qw_ref, kw_ref, y_ref, knbuf, vbuf, obuf, lbuf = refs[5 * n_pat:]
    first_tile = pl.program_id(1) == 0
    slope = sl_ref[pl.program_id(2)]
    tt, hd = z_ref.shape
    scale = hd ** -0.5
    qi = lax.broadcasted_iota(jnp.int32, (tq, 2 * tq), 0)
    kj = lax.broadcasted_iota(jnp.int32, (tq, 2 * tq), 1)
    steps = qi + tq - kj

    for g, (window, dil) in enumerate(patterns):
        q_ref, kp_ref, kc_ref, vp_ref, vc_ref = qkv[g]
        halo = tq * dil
        knbuf[0:halo, :] = _head_rms(kp_ref[...], kw_ref[...])
        knbuf[halo:halo + tt, :] = _head_rms(kc_ref[...], kw_ref[...])
        vbuf[0:halo, :] = vp_ref[...]
        vbuf[halo:halo + tt, :] = vc_ref[...]
        bias = jnp.where((steps >= 0) & (steps <= window // dil), (-slope) * (steps * dil).astype(F32), NEG)
        bias_first = jnp.where((kj >= tq) | jnp.logical_not(first_tile), bias, NEG)
        for r in range(dil):
            for j in range(tt // halo):
                start = r + halo * j
                rows = lambda n, st=start: pl.ds(st, n, stride=dil) if dil > 1 else pl.ds(st, n)
                qn = _head_rms(q_ref[rows(tq), :], qw_ref[...]) * scale
                s = _dot_nt(qn, knbuf[rows(2 * tq), :]) + (bias_first if j == 0 else bias)
                m = jnp.max(s, axis=1, keepdims=True)
                p = jnp.exp(s - m)
                den = jnp.sum(p, axis=1, keepdims=True)
                obuf[g, rows(tq), :] = _dot(p / den, vbuf[rows(2 * tq), :])
                lbuf[g, rows(tq), :] = jnp.broadcast_to(m + jnp.log(den), (tq, hd))

    for c in range(0, tt, mix_rows):
        cs = slice(c, c + mix_rows)
        ls = [lbuf[g, cs, :] for g in range(n_pat)]
        mx = functools.reduce(jnp.maximum, ls)
        es = [jnp.exp(l - mx) for l in ls]
        tot = functools.reduce(lambda a, b: a + b, es)
        o = sum((e / tot) * obuf[g, cs, :] for g, e in enumerate(es))
        y_ref[cs, :] = (o * _silu(z_ref[cs, :])).astype(y_ref.dtype)


def _dilated_layer(x2d, hn, bsz, seq, w_in, q_norm_w, k_norm_w, w_out):
    t, d = x2d.shape
    hd = q_norm_w.shape[-1]
    inner = w_out.shape[0]
    heads = inner // hd
    n_pat = len(D_PATTERNS)
    cols = 3 * n_pat * inner + inner
    tq = D_QBLK
    max_dil = max(dil for _, dil in D_PATTERNS)
    tt = tq * max_dil
    assert hd == LANES and seq % tt == 0
    assert all(window // dil <= tq and tt % (tq * dil) == 0 for window, dil in D_PATTERNS)
    slopes = jnp.asarray(2.0 ** (-8.0 * np.arange(1, heads + 1) / heads), dtype=F32)
    u = _matmul(hn, w_in, name="d_in").reshape(bsz, seq, cols)

    in_specs = [pl.BlockSpec(memory_space=pltpu.SMEM)]
    args = [slopes]
    for g, (window, dil) in enumerate(D_PATTERNS):
        halo = tq * dil
        cur = lambda part, g=g: pl.BlockSpec((None, tt, hd), lambda b, s, h: (b, s, (3 * g + part) * heads + h))
        prev = lambda part, g=g, halo=halo: pl.BlockSpec(
            (None, halo, hd), lambda b, s, h: (b, jnp.maximum(s * (tt // halo) - 1, 0), (3 * g + part) * heads + h))
        in_specs += [cur(0), prev(1), cur(1), prev(2), cur(2)]
        args += [u] * 5
    wspec = pl.BlockSpec((1, hd), lambda b, s, h: (0, 0))
    in_specs += [pl.BlockSpec((None, tt, hd), lambda b, s, h: (b, s, 3 * n_pat * heads + h)), wspec, wspec]
    args += [u, q_norm_w.reshape(1, hd), k_norm_w.reshape(1, hd)]
    y = pl.pallas_call(
        functools.partial(_dilated_body, patterns=D_PATTERNS, tq=tq, mix_rows=_pick(tt, (256, 128))),
        grid=(bsz, seq // tt, heads), in_specs=in_specs,
        out_specs=pl.BlockSpec((None, tt, hd), lambda b, s, h: (b, s, h)),
        out_shape=jax.ShapeDtypeStruct((bsz, seq, inner), BF16),
        scratch_shapes=[pltpu.VMEM((2 * tt, hd), F32), pltpu.VMEM((2 * tt, hd), F32),
                        pltpu.VMEM((n_pat, tt, hd), F32), pltpu.VMEM((n_pat, tt, hd), F32)],
        compiler_params=_params("parallel", "parallel", "parallel"), name="d_attn",
    )(*args)
    return _matmul(y.reshape(t, inner), w_out, resid=x2d, name="d_out")


def _nsa_prep_body(k0_ref, v0_ref, k1_ref, v1_ref, k2_ref, v2_ref, pe_ref, wk_ref, wv_ref, knw_ref,
                   kc_ref, vc_ref, ks_ref, vs_ref, kw_ref, vw_ref, *, stride):
    n_c = kc_ref.shape[0]

    def compress(x_ref, w_ref, pe):
        p1 = jnp.zeros(kc_ref.shape, F32)
        p2 = jnp.zeros(kc_ref.shape, F32)
        for l in range(stride):
            x = x_ref[pl.ds(l, n_c, stride=stride), :]
            p1 = p1 + _dot(x + pe[l:l + 1, :], w_ref[l])
            p2 = p2 + _dot(x + pe[stride + l:stride + l + 1, :], w_ref[stride + l])
        return p1 + pltpu.roll(p2, n_c - 1, 0)

    kc_ref[...] = _head_rms(compress(k0_ref, wk_ref, pe_ref[0]), knw_ref[0:1, :])
    vc_ref[...] = compress(v0_ref, wv_ref, pe_ref[1])

    seq, hd = k1_ref.shape
    pos = lax.broadcasted_iota(jnp.int32, (seq, hd), 0)
    lane = lax.broadcasted_iota(jnp.int32, (seq, hd), 1)
    split_shift = NSA_POS_SPLIT.bit_length() - 1
    pos_hi = ((pos >> split_shift) << split_shift).astype(F32)
    pos_lo = (pos & (NSA_POS_SPLIT - 1)).astype(F32)
    pos_cols = jnp.where(lane < 3, pos_hi, jnp.where(lane < NSA_SEL_OFF, pos_lo, 0.0))
    sel_blk = pos >> (B_SEL_LEN.bit_length() - 1)
    onehot = jnp.where(lane - NSA_SEL_OFF == sel_blk, NEG, 0.0)
    ones = jnp.ones((seq, hd), vs_ref.dtype)
    ks_ref[:, 0:hd] = _head_rms(k1_ref[...], knw_ref[1:2, :]).astype(ks_ref.dtype)
    ks_ref[:, hd:2 * hd] = (pos_cols + onehot).astype(ks_ref.dtype)
    vs_ref[:, 0:hd] = v1_ref[...].astype(vs_ref.dtype)
    vs_ref[:, hd:2 * hd] = ones
    kw_ref[:, 0:hd] = _head_rms(k2_ref[...], knw_ref[2:3, :]).astype(kw_ref.dtype)
    kw_ref[:, hd:2 * hd] = pos_cols.astype(kw_ref.dtype)
    vw_ref[:, 0:hd] = v2_ref[...].astype(vw_ref.dtype)
    vw_ref[:, hd:2 * hd] = ones


def _nsa_body(sl_ref, sp_ref, q_ref, z_ref, g_ref, qw_ref, kc_ref, vc_ref, ks_ref, vs_ref, kw_ref, vw_ref, ov_ref,
              y_ref, q2, acc, m_scr, o_scr, used, *, rep, hd, n_top, n_blk, kb_sel):
    g = pl.program_id(1)
    i = pl.program_id(2)
    tq = q_ref.shape[0]
    n_cmp = kc_ref.shape[0]
    t0 = i * tq
    scale = hd ** -0.5
    rows = lambda r: slice(r * tq, (r + 1) * tq)

    for r in range(rep):
        q2[rows(r), 0:hd] = (_head_rms(q_ref[:, r * hd:(r + 1) * hd], qw_ref[...]) * scale).astype(q2.dtype)
    gts = _sigmoid(g_ref[...])
    t_col = t0 + lax.broadcasted_iota(jnp.int32, (tq, 1), 0)

    cmp_end = lax.broadcasted_iota(jnp.int32, (tq, n_cmp), 1) * B_CMP_STRIDE + (B_CMP_LEN - 1)
    dist_ci = t_col - cmp_end
    mask_c = dist_ci >= 0
    dist_c = dist_ci.astype(F32)
    s_all = _dot_nt(q2[:, 0:hd], kc_ref[...])
    psum = jnp.zeros((tq, n_cmp), F32)
    ps = []
    for r in range(rep):
        s = s_all[rows(r), :] - sl_ref[g, r] * dist_c
        s = jnp.where(mask_c, s, NEG)
        m = jnp.max(s, axis=1, keepdims=True)
        p = jnp.where(mask_c, jnp.exp(s - m), 0.0)
        den = jnp.sum(p, axis=1, keepdims=True)
        p = p / jnp.where(den > 0, den, 1.0)
        psum = psum + p
        ps.append(p.astype(BF16))
    o_c = _dot(jnp.concatenate(ps, axis=0), vc_ref[...])
    for r in range(rep):
        o_scr[rows(r), :] = gts[:, 3 * r:3 * r + 1] * o_c[rows(r), :]

    imp = jnp.dot(psum, ov_ref[...], preferred_element_type=F32, precision=lax.Precision.HIGHEST)
    lane = lax.broadcasted_iota(jnp.int32, (tq, hd), 1)
    blk = lane - NSA_SEL_OFF
    in_range = (blk >= 0) & (blk < n_blk)
    cur = t_col >> (B_SEL_LEN.bit_length() - 1)
    forced = (blk == 0) | (blk == cur) | (blk == cur - 1)
    imp = jnp.where(forced, imp + FORCE, imp)
    imp = jnp.where(in_range & (blk * B_SEL_LEN <= t_col), imp, NEG)
    imp_t = imp.T
    n_grp = -(-(NSA_SEL_OFF + n_blk) // SUBLANES)
    grp = [imp_t[SUBLANES * k:SUBLANES * (k + 1), :] for k in range(n_grp)]
    cnt = [jnp.zeros((SUBLANES, tq), F32) for _ in range(n_grp)]
    sub = lax.broadcasted_iota(jnp.int32, (SUBLANES, tq), 0)
    for mb in range(n_blk):
        slot = NSA_SEL_OFF + mb
        row = imp_t[slot:slot + 1, :]
        for k in range(n_grp):
            if SUBLANES * k > slot:
                beats = row >= grp[k]
            elif SUBLANES * (k + 1) <= slot:
                beats = row > grp[k]
            else:
                beats = (row > grp[k]) | ((row == grp[k]) & (sub > slot - SUBLANES * k))
            cnt[k] = cnt[k] + jnp.where(beats, 1.0, 0.0)
    cnt_t = jnp.concatenate(cnt + [jnp.zeros((hd - SUBLANES * n_grp, tq), F32)], axis=0)
    not_sel = jnp.where((cnt_t < n_top) & (imp_t > NEG / 2), 0.0, 1.0).T

    sel_bf = jnp.where(in_range, 1.0 - not_sel, 0.0).astype(BF16)
    per_slot = jnp.dot(jnp.ones((SUBLANES, tq), BF16), sel_bf, preferred_element_type=F32)
    slot_io = lax.broadcasted_iota(jnp.int32, (hd, hd), 0) - NSA_SEL_OFF
    kb_io = lax.broadcasted_iota(jnp.int32, (hd, hd), 1)
    kb_shift = (kb_sel // B_SEL_LEN).bit_length() - 1
    group = jnp.where((slot_io >= 0) & ((slot_io >> kb_shift) == kb_io), 1.0, 0.0).astype(BF16)
    per_kb = jnp.dot(per_slot.astype(BF16), group, preferred_element_type=F32)
    lane_s = lax.broadcasted_iota(jnp.int32, (SUBLANES, hd), 1)
    for kb in range(used.shape[0]):
        used[kb] = jnp.max(jnp.where(lane_s == kb, per_kb, 0.0)).astype(jnp.int32)

    for r in range(rep):
        piece = jnp.where((lane == 0) | (lane == 3), sp_ref[g, 3 * r],
                          jnp.where((lane == 1) | (lane == 4), sp_ref[g, 3 * r + 1],
                                    jnp.where((lane == 2) | (lane == 5), sp_ref[g, 3 * r + 2], 0.0)))
        q2[rows(r), hd:2 * hd] = jnp.where(in_range, not_sel, piece).astype(q2.dtype)

    def gate(r, branch):
        return gts[:, 3 * r + branch:3 * r + branch + 1]

    acc[...] = jnp.zeros_like(acc)
    m_scr[...] = jnp.full(m_scr.shape, NEG, F32)
    n_chunk = kb_sel // hd

    def sel_block(start, bias):
        kblk = ks_ref[pl.ds(start, kb_sel), :]
        vblk = vs_ref[pl.ds(start, kb_sel), :]
        s_all = lax.dot_general(q2[...], kblk, (((1,), (1,)), ((), ())), preferred_element_type=F32)
        for r in range(rep):
            s = s_all[rows(r), :]
            if bias is not None:
                s = s + bias
            mx = functools.reduce(jnp.maximum, [s[:, c * hd:(c + 1) * hd] for c in range(n_chunk)])
            m_old = m_scr[rows(r), :]
            m_new = jnp.maximum(m_old, jnp.max(mx, axis=1, keepdims=True))
            alpha = jnp.exp(m_old - m_new)
            p = jnp.exp(s - jnp.concatenate([m_new] * n_chunk, axis=1)).astype(BF16)
            acc[rows(r), :] = (jnp.concatenate([alpha, alpha], axis=1) * acc[rows(r), :]
                               + jnp.dot(p, vblk, preferred_element_type=F32))
            m_scr[rows(r), :] = m_new

    def sel_step(kb, carry):
        @pl.when(used[kb] > 0)
        def _():
            sel_block(pl.multiple_of(kb * kb_sel, kb_sel), None)

        return carry

    n_full = lax.div(t0, kb_sel)
    lax.fori_loop(0, n_full, sel_step, 0)
    d_start = pl.multiple_of(n_full * kb_sel, kb_sel)
    d_pos = d_start + lax.broadcasted_iota(jnp.int32, (tq, kb_sel), 1)
    sel_block(d_start, jnp.where(d_pos <= t_col, 0.0, NEG))
    for r in range(rep):
        o_scr[rows(r), :] += gate(r, 1) * (acc[rows(r), 0:hd] / acc[rows(r), hd:2 * hd])

    n_w = B_WIN + tq
    w_start = pl.multiple_of(jnp.maximum(t0 - B_WIN, 0), tq)
    w_dist = t_col - (w_start + lax.broadcasted_iota(jnp.int32, (tq, n_w), 1))
    w_bias = jnp.where((w_dist >= 0) & (w_dist < B_WIN), 0.0, NEG)
    kblk = kw_ref[pl.ds(w_start, n_w), :]
    vblk = vw_ref[pl.ds(w_start, n_w), :]
    s_all = lax.dot_general(q2[...], kblk, (((1,), (1,)), ((), ())), preferred_element_type=F32)
    for r in range(rep):
        s = s_all[rows(r), :] + w_bias
        p = jnp.exp(s - jnp.max(s, axis=1, keepdims=True)).astype(BF16)
        res = jnp.dot(p, vblk, preferred_element_type=F32)
        o_scr[rows(r), :] += gate(r, 2) * (res[:, 0:hd] / res[:, hd:2 * hd])

    for r in range(rep):
        hs = slice(r * hd, (r + 1) * hd)
        y_ref[:, hs] = (o_scr[rows(r), :] * _silu(z_ref[:, hs])).astype(y_ref.dtype)


def _nsa_layer(x2d, hn, bsz, seq, w_in, cmp_pe, cmp_wk, cmp_wv, q_norm_w, k_norm_w, w_out):
    t, d = x2d.shape
    hd = q_norm_w.shape[-1]
    inner = w_out.shape[0]
    heads = inner // hd
    rep = heads // B_KV
    kvw = B_KV * hd
    n_main = 2 * inner + 6 * kvw
    assert B_CMP_LEN == 2 * B_CMP_STRIDE and seq % B_CMP_STRIDE == 0 and 3 * heads <= LANES
    n_cmp = seq // B_CMP_STRIDE
    n_blk = seq // B_SEL_LEN
    n_top = min(B_N_SEL, n_blk)
    tq = 128
    kb_sel = 256
    assert hd == LANES and NSA_SEL_OFF + n_blk <= hd and NSA_POS_SPLIT == hd
    assert kb_sel % tq == 0 and seq % kb_sel == 0 and B_WIN % tq == 0 and seq >= B_WIN + tq

    u = _matmul(hn, w_in, n=n_main, name="b_in").reshape(bsz, seq, n_main)
    gts = _matmul(hn, w_in, n=LANES, col0=n_main, name="b_gates")[:, :3 * heads].reshape(bsz, seq, B_KV, 3 * rep)
    gts = jnp.transpose(gts, (0, 2, 1, 3))

    kv0 = 2 * inner // hd
    kvspec = lambda br, kv: pl.BlockSpec((None, seq, hd), lambda b, g: (b, 0, kv0 + (br * 2 + kv) * B_KV + g))
    cspec = pl.BlockSpec((None, None, n_cmp, hd), lambda b, g: (b, g, 0, 0))
    sspec = pl.BlockSpec((None, None, seq, 2 * hd), lambda b, g: (b, g, 0, 0))
    wspec = pl.BlockSpec((B_CMP_LEN, hd, hd), lambda b, g: (0, 0, 0))
    c_sh = jax.ShapeDtypeStruct((bsz, B_KV, n_cmp, hd), F32)
    s_sh = jax.ShapeDtypeStruct((bsz, B_KV, seq, 2 * hd), BF16)
    kc, vc, ks, vs, kw, vw = pl.pallas_call(
        functools.partial(_nsa_prep_body, stride=B_CMP_STRIDE),
        grid=(bsz, B_KV),
        in_specs=[kvspec(0, 0), kvspec(0, 1), kvspec(1, 0), kvspec(1, 1), kvspec(2, 0), kvspec(2, 1),
                  pl.BlockSpec((2, B_CMP_LEN, hd), lambda b, g: (0, 0, 0)), wspec, wspec,
                  pl.BlockSpec((3, hd), lambda b, g: (0, 0))],
        out_specs=[cspec, cspec, sspec, sspec, sspec, sspec],
        out_shape=[c_sh, c_sh, s_sh, s_sh, s_sh, s_sh],
        compiler_params=_params("parallel", "parallel"), name="b_prep",
    )(u, u, u, u, u, u, cmp_pe, cmp_wk.reshape(B_CMP_LEN, hd, hd).astype(BF16),
      cmp_wv.reshape(B_CMP_LEN, hd, hd).astype(BF16), k_norm_w)

    cmp_start = np.arange(n_cmp) * B_CMP_STRIDE
    sel_start = np.arange(n_blk) * B_SEL_LEN
    ov = np.clip(np.minimum(cmp_start[:, None] + B_CMP_LEN, sel_start[None, :] + B_SEL_LEN)
                 - np.maximum(cmp_start[:, None], sel_start[None, :]), 0, None) / B_CMP_LEN
    ov[cmp_start + B_CMP_LEN > seq] = 0.0
    ov = np.pad(ov, ((0, 0), (NSA_SEL_OFF, hd - NSA_SEL_OFF - n_blk)))
    slopes_np = (2.0 ** (-8.0 * np.arange(1, heads + 1) / heads)).astype(np.float32)
    slopes = jnp.asarray(slopes_np).reshape(B_KV, rep)
    to_bf16 = lambda v: v.astype(ml_dtypes.bfloat16).astype(np.float32)
    s_hi = to_bf16(slopes_np)
    s_mid = to_bf16(slopes_np - s_hi)
    s_lo = to_bf16(slopes_np - s_hi - s_mid)
    pieces = jnp.asarray(np.stack([s_hi, s_mid, s_lo], axis=-1).reshape(B_KV, 3 * rep))

    rw = rep * hd
    cspec3 = pl.BlockSpec((None, None, n_cmp, hd), lambda b, g, i: (b, g, 0, 0))
    sspec3 = pl.BlockSpec((None, None, seq, 2 * hd), lambda b, g, i: (b, g, 0, 0))
    y = pl.pallas_call(
        functools.partial(_nsa_body, rep=rep, hd=hd, n_top=n_top, n_blk=n_blk, kb_sel=kb_sel),
        grid=(bsz, B_KV, seq // tq),
        in_specs=[pl.BlockSpec(memory_space=pltpu.SMEM), pl.BlockSpec(memory_space=pltpu.SMEM),
                  pl.BlockSpec((None, tq, rw), lambda b, g, i: (b, i, g)),
                  pl.BlockSpec((None, tq, rw), lambda b, g, i: (b, i, inner // rw + g)),
                  pl.BlockSpec((None, None, tq, 3 * rep), lambda b, g, i: (b, g, i, 0)),
                  pl.BlockSpec((1, hd), lambda b, g, i: (0, 0)),
                  cspec3, cspec3, sspec3, sspec3, sspec3, sspec3,
                  pl.BlockSpec((n_cmp, hd), lambda b, g, i: (0, 0))],
        out_specs=pl.BlockSpec((None, tq, rw), lambda b, g, i: (b, i, g)),
        out_shape=jax.ShapeDtypeStruct((bsz, seq, inner), BF16),
        scratch_shapes=[pltpu.VMEM((rep * tq, 2 * hd), BF16), pltpu.VMEM((rep * tq, 2 * hd), F32),
                        pltpu.VMEM((rep * tq, hd), F32), pltpu.VMEM((rep * tq, hd), F32),
                        pltpu.SMEM((seq // kb_sel,), jnp.int32)],
        compiler_params=_params("parallel", "parallel", "arbitrary"), name="b_attn",
    )(slopes, pieces, u, u, gts, q_norm_w.reshape(1, hd), kc, vc, ks, vs, kw, vw, jnp.asarray(ov, dtype=F32))
    return _matmul(y.reshape(t, inner), w_out, resid=x2d, name="b_out")


def kernel(x, norm_w, a_w_in, a_conv_w, a_conv_b, a_gate_b, a_out_norm_w, a_w_out, b_w_in, b_cmp_pe, b_cmp_wk, b_cmp_wv, b_q_norm_w, b_k_norm_w, b_w_out, c_w_in, c_conv_w, c_conv_b, c_w_a, c_b_a, c_w_x, c_b_x, c_lambda, c_w_out, d_w_in, d_q_norm_w, d_k_norm_w, d_w_out):
    bsz, seq, d = x.shape
    depth = norm_w.shape[0]
    x2d = x.reshape(bsz * seq, d)
    for layer in range(depth):
        kind, j = layer % 4, layer // 4
        hn = _rmsnorm(x2d, norm_w[layer])
        if kind == 0:
            x2d = _mlstm_layer(x2d, hn, bsz, seq, a_w_in[j], a_conv_w[j], a_conv_b[j], a_gate_b[j],
                               a_out_norm_w[j], a_w_out[j])
        elif kind == 1:
            x2d = _nsa_layer(x2d, hn, bsz, seq, b_w_in[j], b_cmp_pe[j], b_cmp_wk[j], b_cmp_wv[j], b_q_norm_w[j],
                             b_k_norm_w[j], b_w_out[j])
        elif kind == 2:
            x2d = _rglru_layer(x2d, hn, bsz, seq, c_w_in[j], c_conv_w[j], c_conv_b[j], c_w_a[j], c_b_a[j],
                               c_w_x[j], c_b_x[j], c_lambda[j], c_w_out[j])
        else:
            x2d = _dilated_layer(x2d, hn, bsz, seq, d_w_in[j], d_q_norm_w[j], d_k_norm_w[j], d_w_out[j])
    return x2d.reshape(bsz, seq, d)
```

```python
import functools
import math

import ml_dtypes
import numpy as np
import jax
import jax.numpy as jnp
from jax import lax
from jax.experimental import pallas as pl
from jax.experimental.pallas import tpu as pltpu

F32 = jnp.float32
BF16 = jnp.bfloat16

EPS = 1e-6
NEG = -1e30
FORCE = 1e4
CONV_W = 4

LANES = 128
SUBLANES = 8
V7X_VMEM_LIMIT = 56 * 1024 * 1024

A_CHUNK = 128
B_CMP_LEN = 32
B_CMP_STRIDE = 16
B_SEL_LEN = 64
B_N_SEL = 16
B_WIN = 512
B_KV = 4
NSA_POS_SPLIT = 128
NSA_SEL_OFF = 6
C_BLOCKS = 16
C_EXP = 8.0
D_PATTERNS = ((128, 1), (512, 4), (2048, 16))
D_QBLK = 128


def _params(*sem):
    return pltpu.CompilerParams(dimension_semantics=sem, vmem_limit_bytes=V7X_VMEM_LIMIT)


def _sigmoid(x):
    return 0.5 * jnp.tanh(0.5 * x) + 0.5


def _silu(x):
    return x * _sigmoid(x)


def _softplus(x):
    return jnp.maximum(x, 0.0) + jnp.log1p(jnp.exp(-jnp.abs(x)))


def _head_rms(x, w):
    return x * lax.rsqrt(jnp.mean(x * x, axis=-1, keepdims=True) + EPS) * w


def _dot(a, b):
    return jnp.dot(a.astype(BF16), b.astype(BF16), preferred_element_type=F32)


def _dot_nt(a, b):
    return lax.dot_general(a.astype(BF16), b.astype(BF16), (((1,), (1,)), ((), ())),
                           preferred_element_type=F32)


def _dot_tn(a, b):
    return lax.dot_general(a.astype(BF16), b.astype(BF16), (((0,), (0,)), ((), ())),
                           preferred_element_type=F32)


def _pick(n, cands):
    for c in cands:
        if n % c == 0:
            return c
    raise ValueError(f"no tile for {n}")


def _rmsnorm_body(x_ref, w_ref, o_ref):
    x = x_ref[...]
    o_ref[...] = _head_rms(x, w_ref[...]).astype(o_ref.dtype)


def _rmsnorm(x2d, w):
    t, d = x2d.shape
    tm = _pick(t, (256, 128, 8))
    return pl.pallas_call(
        _rmsnorm_body, grid=(t // tm,),
        in_specs=[pl.BlockSpec((tm, d), lambda i: (i, 0)), pl.BlockSpec((1, d), lambda i: (0, 0))],
        out_specs=pl.BlockSpec((tm, d), lambda i: (i, 0)),
        out_shape=jax.ShapeDtypeStruct((t, d), BF16),
        compiler_params=_params("parallel"), name="rmsnorm")(x2d, w.reshape(1, d))


def _mm_body(a_ref, w_ref, o_ref):
    o_ref[...] = jnp.dot(a_ref[...], w_ref[...], preferred_element_type=F32)


def _mm_res_body(a_ref, w_ref, r_ref, o_ref):
    o_ref[...] = r_ref[...] + jnp.dot(a_ref[...], w_ref[...], preferred_element_type=F32)


def _matmul(a, w, n=None, col0=0, resid=None, name="matmul"):
    m, k = a.shape
    n = w.shape[1] if n is None else n
    w = w.astype(BF16)
    tm = _pick(m, (1024, 512, 256, 128))
    tn = _pick(n, tuple(c for c in (1024, 768, 512, 384, 256, 128)
                        if 2 * (k * c * 2 + tm * k * 2 + (2 if resid is not None else 1) * tm * c * 4)
                        <= V7X_VMEM_LIMIT - (4 << 20)))
    assert col0 % tn == 0
    cb = col0 // tn
    in_specs = [pl.BlockSpec((tm, k), lambda i, j: (i, 0)), pl.BlockSpec((k, tn), lambda i, j: (0, cb + j))]
    args = [a, w]
    body = _mm_body
    if resid is not None:
        in_specs.append(pl.BlockSpec((tm, tn), lambda i, j: (i, j)))
        args.append(resid)
        body = _mm_res_body
    return pl.pallas_call(
        body, grid=(m // tm, n // tn), in_specs=in_specs,
        out_specs=pl.BlockSpec((tm, tn), lambda i, j: (i, j)),
        out_shape=jax.ShapeDtypeStruct((m, n), F32),
        compiler_params=_params("parallel", "parallel"), name=name)(*args)


def _mlstm_body(gb_ref, q_ref, k_ref, v_ref, og_ref, z_ref, gcol_ref, grow_ref, cwq_ref, cwk_ref,
                cbq_ref, cbk_ref, onw_ref, y_ref, c_scr, n_scr, m_scr, extq, extk, *, q_scale):
    h = pl.program_id(1)
    c = pl.program_id(2)
    L = q_ref.shape[0]

    @pl.when(c == 0)
    def _():
        c_scr[...] = jnp.zeros_like(c_scr)
        n_scr[...] = jnp.zeros_like(n_scr)
        m_scr[...] = jnp.zeros_like(m_scr)
        extq[L:L + 8, :] = jnp.zeros((8, extq.shape[1]), F32)
        extk[L:L + 8, :] = jnp.zeros((8, extk.shape[1]), F32)

    def conv_silu(ext, x_ref, cw_ref, cb_ref):
        ext[0:8, :] = ext[L:L + 8, :]
        ext[8:L + 8, :] = x_ref[...]
        acc = cb_ref[...] + cw_ref[0:1, :] * ext[pl.ds(8 - CONV_W + 1, L), :]
        for j in range(1, CONV_W):
            acc = acc + cw_ref[j:j + 1, :] * ext[pl.ds(8 - CONV_W + 1 + j, L), :]
        return _silu(acc)

    q = conv_silu(extq, q_ref, cwq_ref, cbq_ref) * q_scale
    k = conv_silu(extk, k_ref, cwk_ref, cbk_ref)
    v = v_ref[...]

    bi = gb_ref[0, h]
    bf = gb_ref[1, h]
    gcol = gcol_ref[...]
    grow = grow_ref[...]
    li_col = gcol[:, 0:1] + bi
    li_row = grow[0:1, :] + bi
    lf_col = -_softplus(-(gcol[:, 1:2] + bf))
    lf_row = -_softplus(-(grow[1:2, :] + bf))

    t_io = lax.broadcasted_iota(jnp.int32, (L, L), 0)
    s_io = lax.broadcasted_iota(jnp.int32, (L, L), 1)
    causal = s_io <= t_io
    b_col = jnp.sum(jnp.where(causal, lf_row, 0.0), axis=1, keepdims=True)
    b_row = jnp.sum(jnp.where(t_io <= s_io, lf_col, 0.0), axis=0, keepdims=True)
    b_last = jnp.sum(lf_row, axis=1, keepdims=True)
    m_st = m_scr[...]

    dmat = jnp.where(causal, b_col - b_row + li_row, NEG)
    inter = b_col + m_st
    m_t = jnp.maximum(inter, jnp.max(dmat, axis=1, keepdims=True))
    w_intra = jnp.exp(dmat - m_t)
    w_inter = jnp.exp(inter - m_t)
    a = w_intra * _dot_nt(q, k)
    c_st = c_scr[...]
    n_st = n_scr[...]
    num = _dot(a, v) + w_inter * _dot(q, c_st)
    qn = jnp.sum(a, axis=1, keepdims=True) + w_inter * jnp.sum(q * n_st, axis=1, keepdims=True)
    hc = num / jnp.maximum(jnp.abs(qn), jnp.exp(-m_t))

    g_row = b_last - b_row + li_row
    g_col = b_last - b_col + li_col
    m_new = jnp.maximum(b_last + m_st, jnp.max(g_row, axis=1, keepdims=True))
    ws_col = jnp.exp(g_col - m_new)
    decay = jnp.exp(b_last + m_st - m_new)
    kw = k * ws_col
    c_scr[...] = decay * c_st + _dot_tn(kw, v)
    n_scr[...] = decay * n_st + jnp.sum(kw, axis=0, keepdims=True)
    m_scr[...] = m_new

    hn = _head_rms(hc, onw_ref[...])
    y_ref[...] = (hn * _sigmoid(og_ref[...]) * _silu(z_ref[...])).astype(y_ref.dtype)


def _mlstm_layer(x2d, hn, bsz, seq, w_in, conv_w, conv_b, gate_b, out_norm_w, w_out):
    t, d = x2d.shape
    heads = gate_b.shape[-1]
    qk = conv_w.shape[-1] // 2
    inner = out_norm_w.shape[-1]
    dqk, dv = qk // heads, inner // heads
    n_main = 2 * qk + 3 * inner
    L = A_CHUNK
    nc = seq // L
    assert dqk % LANES == 0 and dv % LANES == 0 and 2 * heads <= LANES

    u = _matmul(hn, w_in, n=n_main, name="a_in").reshape(bsz, seq, n_main)
    gts = _matmul(hn, w_in, n=LANES, col0=n_main, name="a_gates")[:, :2 * heads].reshape(bsz, nc, L, 2, heads)
    g_col = jnp.transpose(gts, (0, 4, 1, 2, 3))
    g_row = jnp.transpose(gts, (0, 4, 1, 3, 2))

    kq0, kv0, ko0, kz0 = qk // dqk, 2 * qk // dv, (2 * qk + inner) // dv, (2 * qk + 2 * inner) // dv
    blk = lambda w, off: pl.BlockSpec((None, L, w), lambda b, h, c: (b, c, off + h))
    par = lambda r, w, off: pl.BlockSpec((r, w), lambda b, h, c: (0, off + h))
    y = pl.pallas_call(
        functools.partial(_mlstm_body, q_scale=dqk ** -0.5),
        grid=(bsz, heads, nc),
        in_specs=[pl.BlockSpec(memory_space=pltpu.SMEM),
                  blk(dqk, 0), blk(dqk, kq0), blk(dv, kv0), blk(dv, ko0), blk(dv, kz0),
                  pl.BlockSpec((None, None, None, L, 2), lambda b, h, c: (b, h, c, 0, 0)),
                  pl.BlockSpec((None, None, None, 2, L), lambda b, h, c: (b, h, c, 0, 0)),
                  par(CONV_W, dqk, 0), par(CONV_W, dqk, kq0), par(1, dqk, 0), par(1, dqk, kq0),
                  par(1, dv, 0)],
        out_specs=pl.BlockSpec((None, L, dv), lambda b, h, c: (b, c, h)),
        out_shape=jax.ShapeDtypeStruct((bsz, seq, inner), BF16),
        scratch_shapes=[pltpu.VMEM((dqk, dv), F32), pltpu.VMEM((1, dqk), F32), pltpu.VMEM((1, 1), F32),
                        pltpu.VMEM((L + 8, dqk), F32), pltpu.VMEM((L + 8, dqk), F32)],
        compiler_params=_params("parallel", "parallel", "arbitrary"), name="a_mlstm",
    )(gate_b, u, u, u, u, u, g_col, g_row, conv_w, conv_w, conv_b.reshape(1, -1), conv_b.reshape(1, -1),
      out_norm_w.reshape(1, -1))
    return _matmul(y.reshape(t, inner), w_out, resid=x2d, name="a_out")


def _rglru_body(x_ref, z_ref, cw_ref, cb_ref, wa_ref, wx_ref, ba_ref, bx_ref, lam_ref, y_ref,
                ext, xc_scr, a_scr, b_scr, h_scr, *, starts, win, chunk, scan_chunk):
    s = pl.program_id(1)
    ts, width = x_ref.shape

    @pl.when(s == 0)
    def _():
        ext[ts:ts + 8, :] = jnp.zeros((8, width), F32)
        h_scr[...] = jnp.zeros_like(h_scr)

    ext[0:8, :] = ext[ts:ts + 8, :]
    ext[8:ts + 8, :] = x_ref[...]
    for c0 in range(0, width, chunk):
        cs = slice(c0, c0 + chunk)
        xc = cb_ref[:, cs] + cw_ref[0:1, cs] * ext[pl.ds(8 - CONV_W + 1, ts), cs]
        for j in range(1, CONV_W):
            xc = xc + cw_ref[j:j + 1, cs] * ext[pl.ds(8 - CONV_W + 1 + j, ts), cs]
        xc_scr[:, cs] = xc

    a_scr[...] = jnp.zeros_like(a_scr)
    b_scr[...] = jnp.zeros_like(b_scr)
    for n, a0 in enumerate(starts):
        ws = slice(a0, a0 + win)
        xw = xc_scr[:, ws].astype(BF16)
        a_scr[:, ws] += jnp.dot(xw, wa_ref[n], preferred_element_type=F32)
        b_scr[:, ws] += jnp.dot(xw, wx_ref[n], preferred_element_type=F32)

    for c0 in range(0, width, chunk):
        cs = slice(c0, c0 + chunk)
        r = _sigmoid(a_scr[:, cs] + ba_ref[:, cs])
        ig = _sigmoid(b_scr[:, cs] + bx_ref[:, cs])
        log_a = (-C_EXP) * _softplus(-lam_ref[:, cs]) * r
        a = jnp.exp(log_a)
        a_scr[:, cs] = a
        b_scr[:, cs] = jnp.sqrt(-jnp.tanh(log_a) * (a * a + 1.0)) * (ig * xc_scr[:, cs])

    for c0 in range(0, width, scan_chunk):
        cs = slice(c0, c0 + scan_chunk)

        def row(t, h, cs=cs):
            h = a_scr[pl.ds(t, 1), cs] * h + b_scr[pl.ds(t, 1), cs]
            b_scr[pl.ds(t, 1), cs] = h
            return h

        h_scr[:, cs] = lax.fori_loop(0, ts, row, h_scr[:, cs], unroll=8)

    y_ref[...] = (b_scr[...] * _silu(z_ref[...])).astype(y_ref.dtype)


def _rglru_layer(x2d, hn, bsz, seq, w_in, conv_w, conv_b, w_a, b_a, w_x, b_x, lam, w_out):
    t, d = x2d.shape
    width = conv_w.shape[-1]
    nblk = w_a.shape[0]
    blk = width // nblk
    assert width % LANES == 0
    win = LANES
    while True:
        starts = tuple(min(n * blk // LANES * LANES, width - win) for n in range(nblk))
        if all(n * blk - a0 + blk <= win for n, a0 in enumerate(starts)):
            break
        win += LANES

    def window_weights(w):
        return jnp.stack([jnp.pad(w[n], ((n * blk - a0, win - blk - (n * blk - a0)),) * 2)
                          for n, a0 in enumerate(starts)]).astype(BF16)

    u = _matmul(hn, w_in, name="c_in").reshape(bsz, seq, 2 * width)
    ts = _pick(seq, (128, 64, 8))
    chunk = _pick(width, (768, 512, 384, 256, 128))
    scan_chunk = _pick(width, (1792, 1536, 1024, 896, 768, 512, 384, 256, 128))
    full = lambda r: pl.BlockSpec((r, width), lambda b, s: (0, 0))
    wspec = pl.BlockSpec((nblk, win, win), lambda b, s: (0, 0, 0), pipeline_mode=pl.Buffered(1))
    row = lambda v: v.reshape(1, width)
    y = pl.pallas_call(
        functools.partial(_rglru_body, starts=starts, win=win, chunk=chunk, scan_chunk=scan_chunk),
        grid=(bsz, seq // ts),
        in_specs=[pl.BlockSpec((None, ts, width), lambda b, s: (b, s, 0)),
                  pl.BlockSpec((None, ts, width), lambda b, s: (b, s, 1)),
                  full(CONV_W), full(1), wspec, wspec, full(1), full(1), full(1)],
        out_specs=pl.BlockSpec((None, ts, width), lambda b, s: (b, s, 0)),
        out_shape=jax.ShapeDtypeStruct((bsz, seq, width), BF16),
        scratch_shapes=[pltpu.VMEM((ts + 8, width), F32), pltpu.VMEM((ts, width), F32),
                        pltpu.VMEM((ts, width), F32), pltpu.VMEM((ts, width), F32), pltpu.VMEM((1, width), F32)],
        compiler_params=_params("parallel", "arbitrary"), name="c_rglru",
    )(u, u, conv_w, row(conv_b), window_weights(w_a), window_weights(w_x), row(b_a), row(b_x), row(lam))
    return _matmul(y.reshape(t, width), w_out, resid=x2d, name="c_out")


def _dilated_body(sl_ref, *refs, patterns, tq, mix_rows):
    n_pat = len(patterns)
    qkv = [refs[5 * g:5 * g + 5] for g in range(n_pat)]
    z_ref, qw_ref, kw_ref, y_ref, knbuf, vbuf, obuf, lbuf = refs[5 * n_pat:]
    first_tile = pl.program_id(1) == 0
    slope = sl_ref[pl.program_id(2)]
    tt, hd = z_ref.shape
    scale = hd ** -0.5
    qi = lax.broadcasted_iota(jnp.int32, (tq, 2 * tq), 0)
    kj = lax.broadcasted_iota(jnp.int32, (tq, 2 * tq), 1)
    steps = qi + tq - kj

    for g, (window, dil) in enumerate(patterns):
        q_ref, kp_ref, kc_ref, vp_ref, vc_ref = qkv[g]
        halo = tq * dil
        knbuf[0:halo, :] = _head_rms(kp_ref[...], kw_ref[...])
        knbuf[halo:halo + tt, :] = _head_rms(kc_ref[...], kw_ref[...])
        vbuf[0:halo, :] = vp_ref[...]
        vbuf[halo:halo + tt, :] = vc_ref[...]
        bias = jnp.where((steps >= 0) & (steps <= window // dil), (-slope) * (steps * dil).astype(F32), NEG)
        bias_first = jnp.where((kj >= tq) | jnp.logical_not(first_tile), bias, NEG)
        for r in range(dil):
            for j in range(tt // halo):
                start = r + halo * j
                rows = lambda n, st=start: pl.ds(st, n, stride=dil) if dil > 1 else pl.ds(st, n)
                qn = _head_rms(q_ref[rows(tq), :], qw_ref[...]) * scale
                s = _dot_nt(qn, knbuf[rows(2 * tq), :]) + (bias_first if j == 0 else bias)
                m = jnp.max(s, axis=1, keepdims=True)
                p = jnp.exp(s - m)
                den = jnp.sum(p, axis=1, keepdims=True)
                obuf[g, rows(tq), :] = _dot(p / den, vbuf[rows(2 * tq), :])
                lbuf[g, rows(tq), :] = jnp.broadcast_to(m + jnp.log(den), (tq, hd))

    for c in range(0, tt, mix_rows):
        cs = slice(c, c + mix_rows)
        ls = [lbuf[g, cs, :] for g in range(n_pat)]
        mx = functools.reduce(jnp.maximum, ls)
        es = [jnp.exp(l - mx) for l in ls]
        tot = functools.reduce(lambda a, b: a + b, es)
        o = sum((e / tot) * obuf[g, cs, :] for g, e in enumerate(es))
        y_ref[cs, :] = (o * _silu(z_ref[cs, :])).astype(y_ref.dtype)


def _dilated_layer(x2d, hn, bsz, seq, w_in, q_norm_w, k_norm_w, w_out):
    t, d = x2d.shape
    hd = q_norm_w.shape[-1]
    inner = w_out.shape[0]
    heads = inner // hd
    n_pat = len(D_PATTERNS)
    cols = 3 * n_pat * inner + inner
    tq = D_QBLK
    max_dil = max(dil for _, dil in D_PATTERNS)
    tt = tq * max_dil
    assert hd == LANES and seq % tt == 0
    assert all(window // dil <= tq and tt % (tq * dil) == 0 for window, dil in D_PATTERNS)
    slopes = jnp.asarray(2.0 ** (-8.0 * np.arange(1, heads + 1) / heads), dtype=F32)
    u = _matmul(hn, w_in, name="d_in").reshape(bsz, seq, cols)

    in_specs = [pl.BlockSpec(memory_space=pltpu.SMEM)]
    args = [slopes]
    for g, (window, dil) in enumerate(D_PATTERNS):
        halo = tq * dil
        cur = lambda part, g=g: pl.BlockSpec((None, tt, hd), lambda b, s, h: (b, s, (3 * g + part) * heads + h))
        prev = lambda part, g=g, halo=halo: pl.BlockSpec(
            (None, halo, hd), lambda b, s, h: (b, jnp.maximum(s * (tt // halo) - 1, 0), (3 * g + part) * heads + h))
        in_specs += [cur(0), prev(1), cur(1), prev(2), cur(2)]
        args += [u] * 5
    wspec = pl.BlockSpec((1, hd), lambda b, s, h: (0, 0))
    in_specs += [pl.BlockSpec((None, tt, hd), lambda b, s, h: (b, s, 3 * n_pat * heads + h)), wspec, wspec]
    args += [u, q_norm_w.reshape(1, hd), k_norm_w.reshape(1, hd)]
    y = pl.pallas_call(
        functools.partial(_dilated_body, patterns=D_PATTERNS, tq=tq, mix_rows=_pick(tt, (256, 128))),
        grid=(bsz, seq // tt, heads), in_specs=in_specs,
        out_specs=pl.BlockSpec((None, tt, hd), lambda b, s, h: (b, s, h)),
        out_shape=jax.ShapeDtypeStruct((bsz, seq, inner), BF16),
        scratch_shapes=[pltpu.VMEM((2 * tt, hd), F32), pltpu.VMEM((2 * tt, hd), F32),
                        pltpu.VMEM((n_pat, tt, hd), F32), pltpu.VMEM((n_pat, tt, hd), F32)],
        compiler_params=_params("parallel", "parallel", "parallel"), name="d_attn",
    )(*args)
    return _matmul(y.reshape(t, inner), w_out, resid=x2d, name="d_out")


def _nsa_prep_body(k0_ref, v0_ref, k1_ref, v1_ref, k2_ref, v2_ref, pe_ref, wk_ref, wv_ref, knw_ref,
                   kc_ref, vc_ref, ks_ref, vs_ref, kw_ref, vw_ref, *, stride):
    n_c = kc_ref.shape[0]

    def compress(x_ref, w_ref, pe):
        p1 = jnp.zeros(kc_ref.shape, F32)
        p2 = jnp.zeros(kc_ref.shape, F32)
        for l in range(stride):
            x = x_ref[pl.ds(l, n_c, stride=stride), :]
            p1 = p1 + _dot(x + pe[l:l + 1, :], w_ref[l])
            p2 = p2 + _dot(x + pe[stride + l:stride + l + 1, :], w_ref[stride + l])
        return p1 + pltpu.roll(p2, n_c - 1, 0)

    kc_ref[...] = _head_rms(compress(k0_ref, wk_ref, pe_ref[0]), knw_ref[0:1, :])
    vc_ref[...] = compress(v0_ref, wv_ref, pe_ref[1])

    seq, hd = k1_ref.shape
    pos = lax.broadcasted_iota(jnp.int32, (seq, hd), 0)
    lane = lax.broadcasted_iota(jnp.int32, (seq, hd), 1)
    split_shift = NSA_POS_SPLIT.bit_length() - 1
    pos_hi = ((pos >> split_shift) << split_shift).astype(F32)
    pos_lo = (pos & (NSA_POS_SPLIT - 1)).astype(F32)
    pos_cols = jnp.where(lane < 3, pos_hi, jnp.where(lane < NSA_SEL_OFF, pos_lo, 0.0))
    sel_blk = pos >> (B_SEL_LEN.bit_length() - 1)
    onehot = jnp.where(lane - NSA_SEL_OFF == sel_blk, NEG, 0.0)
    ones = jnp.ones((seq, hd), vs_ref.dtype)
    ks_ref[:, 0:hd] = _head_rms(k1_ref[...], knw_ref[1:2, :]).astype(ks_ref.dtype)
    ks_ref[:, hd:2 * hd] = (pos_cols + onehot).astype(ks_ref.dtype)
    vs_ref[:, 0:hd] = v1_ref[...].astype(vs_ref.dtype)
    vs_ref[:, hd:2 * hd] = ones
    kw_ref[:, 0:hd] = _head_rms(k2_ref[...], knw_ref[2:3, :]).astype(kw_ref.dtype)
    kw_ref[:, hd:2 * hd] = pos_cols.astype(kw_ref.dtype)
    vw_ref[:, 0:hd] = v2_ref[...].astype(vw_ref.dtype)
    vw_ref[:, hd:2 * hd] = ones


def _nsa_body(sl_ref, sp_ref, q_ref, z_ref, g_ref, qw_ref, kc_ref, vc_ref, ks_ref, vs_ref, kw_ref, vw_ref, ov_ref,
              y_ref, q2, acc, m_scr, o_scr, used, *, rep, hd, n_top, n_blk, kb_sel):
    g = pl.program_id(1)
    i = pl.program_id(2)
    tq = q_ref.shape[0]
    n_cmp = kc_ref.shape[0]
    t0 = i * tq
    scale = hd ** -0.5
    rows = lambda r: slice(r * tq, (r + 1) * tq)

    for r in range(rep):
        q2[rows(r), 0:hd] = (_head_rms(q_ref[:, r * hd:(r + 1) * hd], qw_ref[...]) * scale).astype(q2.dtype)
    gts = _sigmoid(g_ref[...])
    t_col = t0 + lax.broadcasted_iota(jnp.int32, (tq, 1), 0)

    cmp_end = lax.broadcasted_iota(jnp.int32, (tq, n_cmp), 1) * B_CMP_STRIDE + (B_CMP_LEN - 1)
    dist_ci = t_col - cmp_end
    mask_c = dist_ci >= 0
    dist_c = dist_ci.astype(F32)
    s_all = _dot_nt(q2[:, 0:hd], kc_ref[...])
    psum = jnp.zeros((tq, n_cmp), F32)
    ps = []
    for r in range(rep):
        s = s_all[rows(r), :] - sl_ref[g, r] * dist_c
        s = jnp.where(mask_c, s, NEG)
        m = jnp.max(s, axis=1, keepdims=True)
        p = jnp.where(mask_c, jnp.exp(s - m), 0.0)
        den = jnp.sum(p, axis=1, keepdims=True)
        p = p / jnp.where(den > 0, den, 1.0)
        psum = psum + p
        ps.append(p.astype(BF16))
    o_c = _dot(jnp.concatenate(ps, axis=0), vc_ref[...])
    for r in range(rep):
        o_scr[rows(r), :] = gts[:, 3 * r:3 * r + 1] * o_c[rows(r), :]

    imp = jnp.dot(psum, ov_ref[...], preferred_element_type=F32, precision=lax.Precision.HIGHEST)
    lane = lax.broadcasted_iota(jnp.int32, (tq, hd), 1)
    blk = lane - NSA_SEL_OFF
    in_range = (blk >= 0) & (blk < n_blk)
    cur = t_col >> (B_SEL_LEN.bit_length() - 1)
    forced = (blk == 0) | (blk == cur) | (blk == cur - 1)
    imp = jnp.where(forced, imp + FORCE, imp)
    imp = jnp.where(in_range & (blk * B_SEL_LEN <= t_col), imp, NEG)
    imp_t = imp.T
    n_grp = -(-(NSA_SEL_OFF + n_blk) // SUBLANES)
    grp = [imp_t[SUBLANES * k:SUBLANES * (k + 1), :] for k in range(n_grp)]
    cnt = [jnp.zeros((SUBLANES, tq), F32) for _ in range(n_grp)]
    sub = lax.broadcasted_iota(jnp.int32, (SUBLANES, tq), 0)
    for mb in range(n_blk):
        slot = NSA_SEL_OFF + mb
        row = imp_t[slot:slot + 1, :]
        for k in range(n_grp):
            if SUBLANES * k > slot:
                beats = row >= grp[k]
            elif SUBLANES * (k + 1) <= slot:
                beats = row > grp[k]
            else:
                beats = (row > grp[k]) | ((row == grp[k]) & (sub > slot - SUBLANES * k))
            cnt[k] = cnt[k] + jnp.where(beats, 1.0, 0.0)
    cnt_t = jnp.concatenate(cnt + [jnp.zeros((hd - SUBLANES * n_grp, tq), F32)], axis=0)
    not_sel = jnp.where((cnt_t < n_top) & (imp_t > NEG / 2), 0.0, 1.0).T

    sel_bf = jnp.where(in_range, 1.0 - not_sel, 0.0).astype(BF16)
    per_slot = jnp.dot(jnp.ones((SUBLANES, tq), BF16), sel_bf, preferred_element_type=F32)
    slot_io = lax.broadcasted_iota(jnp.int32, (hd, hd), 0) - NSA_SEL_OFF
    kb_io = lax.broadcasted_iota(jnp.int32, (hd, hd), 1)
    kb_shift = (kb_sel // B_SEL_LEN).bit_length() - 1
    group = jnp.where((slot_io >= 0) & ((slot_io >> kb_shift) == kb_io), 1.0, 0.0).astype(BF16)
    per_kb = jnp.dot(per_slot.astype(BF16), group, preferred_element_type=F32)
    lane_s = lax.broadcasted_iota(jnp.int32, (SUBLANES, hd), 1)
    for kb in range(used.shape[0]):
        used[kb] = jnp.max(jnp.where(lane_s == kb, per_kb, 0.0)).astype(jnp.int32)

    for r in range(rep):
        piece = jnp.where((lane == 0) | (lane == 3), sp_ref[g, 3 * r],
                          jnp.where((lane == 1) | (lane == 4), sp_ref[g, 3 * r + 1],
                                    jnp.where((lane == 2) | (lane == 5), sp_ref[g, 3 * r + 2], 0.0)))
        q2[rows(r), hd:2 * hd] = jnp.where(in_range, not_sel, piece).astype(q2.dtype)

    def gate(r, branch):
        return gts[:, 3 * r + branch:3 * r + branch + 1]

    acc[...] = jnp.zeros_like(acc)
    m_scr[...] = jnp.full(m_scr.shape, NEG, F32)
    n_chunk = kb_sel // hd

    def sel_block(start, bias):
        kblk = ks_ref[pl.ds(start, kb_sel), :]
        vblk = vs_ref[pl.ds(start, kb_sel), :]
        s_all = lax.dot_general(q2[...], kblk, (((1,), (1,)), ((), ())), preferred_element_type=F32)
        for r in range(rep):
            s = s_all[rows(r), :]
            if bias is not None:
                s = s + bias
            mx = functools.reduce(jnp.maximum, [s[:, c * hd:(c + 1) * hd] for c in range(n_chunk)])
            m_old = m_scr[rows(r), :]
            m_new = jnp.maximum(m_old, jnp.max(mx, axis=1, keepdims=True))
            alpha = jnp.exp(m_old - m_new)
            p = jnp.exp(s - jnp.concatenate([m_new] * n_chunk, axis=1)).astype(BF16)
            acc[rows(r), :] = (jnp.concatenate([alpha, alpha], axis=1) * acc[rows(r), :]
                               + jnp.dot(p, vblk, preferred_element_type=F32))
            m_scr[rows(r), :] = m_new

    def sel_step(kb, carry):
        @pl.when(used[kb] > 0)
        def _():
            sel_block(pl.multiple_of(kb * kb_sel, kb_sel), None)

        return carry

    n_full = lax.div(t0, kb_sel)
    lax.fori_loop(0, n_full, sel_step, 0)
    d_start = pl.multiple_of(n_full * kb_sel, kb_sel)
    d_pos = d_start + lax.broadcasted_iota(jnp.int32, (tq, kb_sel), 1)
    sel_block(d_start, jnp.where(d_pos <= t_col, 0.0, NEG))
    for r in range(rep):
        o_scr[rows(r), :] += gate(r, 1) * (acc[rows(r), 0:hd] / acc[rows(r), hd:2 * hd])

    n_w = B_WIN + tq
    w_start = pl.multiple_of(jnp.maximum(t0 - B_WIN, 0), tq)
    w_dist = t_col - (w_start + lax.broadcasted_iota(jnp.int32, (tq, n_w), 1))
    w_bias = jnp.where((w_dist >= 0) & (w_dist < B_WIN), 0.0, NEG)
    kblk = kw_ref[pl.ds(w_start, n_w), :]
    vblk = vw_ref[pl.ds(w_start, n_w), :]
    s_all = lax.dot_general(q2[...], kblk, (((1,), (1,)), ((), ())), preferred_element_type=F32)
    for r in range(rep):
        s = s_all[rows(r), :] + w_bias
        p = jnp.exp(s - jnp.max(s, axis=1, keepdims=True)).astype(BF16)
        res = jnp.dot(p, vblk, preferred_element_type=F32)
        o_scr[rows(r), :] += gate(r, 2) * (res[:, 0:hd] / res[:, hd:2 * hd])

    for r in range(rep):
        hs = slice(r * hd, (r + 1) * hd)
        y_ref[:, hs] = (o_scr[rows(r), :] * _silu(z_ref[:, hs])).astype(y_ref.dtype)


def _nsa_layer(x2d, hn, bsz, seq, w_in, cmp_pe, cmp_wk, cmp_wv, q_norm_w, k_norm_w, w_out):
    t, d = x2d.shape
    hd = q_norm_w.shape[-1]
    inner = w_out.shape[0]
    heads = inner // hd
    rep = heads // B_KV
    kvw = B_KV * hd
    n_main = 2 * inner + 6 * kvw
    assert B_CMP_LEN == 2 * B_CMP_STRIDE and seq % B_CMP_STRIDE == 0 and 3 * heads <= LANES
    n_cmp = seq // B_CMP_STRIDE
    n_blk = seq // B_SEL_LEN
    n_top = min(B_N_SEL, n_blk)
    tq = 128
    kb_sel = 512
    assert hd == LANES and NSA_SEL_OFF + n_blk <= hd and NSA_POS_SPLIT == hd
    assert kb_sel % tq == 0 and seq % kb_sel == 0 and B_WIN % tq == 0 and seq >= B_WIN + tq

    u = _matmul(hn, w_in, n=n_main, name="b_in").reshape(bsz, seq, n_main)
    gts = _matmul(hn, w_in, n=LANES, col0=n_main, name="b_gates")[:, :3 * heads].reshape(bsz, seq, B_KV, 3 * rep)
    gts = jnp.transpose(gts, (0, 2, 1, 3))

    kv0 = 2 * inner // hd
    kvspec = lambda br, kv: pl.BlockSpec((None, seq, hd), lambda b, g: (b, 0, kv0 + (br * 2 + kv) * B_KV + g))
    cspec = pl.BlockSpec((None, None, n_cmp, hd), lambda b, g: (b, g, 0, 0))
    sspec = pl.BlockSpec((None, None, seq, 2 * hd), lambda b, g: (b, g, 0, 0))
    wspec = pl.BlockSpec((B_CMP_LEN, hd, hd), lambda b, g: (0, 0, 0))
    c_sh = jax.ShapeDtypeStruct((bsz, B_KV, n_cmp, hd), F32)
    s_sh = jax.ShapeDtypeStruct((bsz, B_KV, seq, 2 * hd), BF16)
    kc, vc, ks, vs, kw, vw = pl.pallas_call(
        functools.partial(_nsa_prep_body, stride=B_CMP_STRIDE),
        grid=(bsz, B_KV),
        in_specs=[kvspec(0, 0), kvspec(0, 1), kvspec(1, 0), kvspec(1, 1), kvspec(2, 0), kvspec(2, 1),
                  pl.BlockSpec((2, B_CMP_LEN, hd), lambda b, g: (0, 0, 0)), wspec, wspec,
                  pl.BlockSpec((3, hd), lambda b, g: (0, 0))],
        out_specs=[cspec, cspec, sspec, sspec, sspec, sspec],
        out_shape=[c_sh, c_sh, s_sh, s_sh, s_sh, s_sh],
        compiler_params=_params("parallel", "parallel"), name="b_prep",
    )(u, u, u, u, u, u, cmp_pe, cmp_wk.reshape(B_CMP_LEN, hd, hd).astype(BF16),
      cmp_wv.reshape(B_CMP_LEN, hd, hd).astype(BF16), k_norm_w)

    cmp_start = np.arange(n_cmp) * B_CMP_STRIDE
    sel_start = np.arange(n_blk) * B_SEL_LEN
    ov = np.clip(np.minimum(cmp_start[:, None] + B_CMP_LEN, sel_start[None, :] + B_SEL_LEN)
                 - np.maximum(cmp_start[:, None], sel_start[None, :]), 0, None) / B_CMP_LEN
    ov[cmp_start + B_CMP_LEN > seq] = 0.0
    ov = np.pad(ov, ((0, 0), (NSA_SEL_OFF, hd - NSA_SEL_OFF - n_blk)))
    slopes_np = (2.0 ** (-8.0 * np.arange(1, heads + 1) / heads)).astype(np.float32)
    slopes = jnp.asarray(slopes_np).reshape(B_KV, rep)
    to_bf16 = lambda v: v.astype(ml_dtypes.bfloat16).astype(np.float32)
    s_hi = to_bf16(slopes_np)
    s_mid = to_bf16(slopes_np - s_hi)
    s_lo = to_bf16(slopes_np - s_hi - s_mid)
    pieces = jnp.asarray(np.stack([s_hi, s_mid, s_lo], axis=-1).reshape(B_KV, 3 * rep))

    rw = rep * hd
    cspec3 = pl.BlockSpec((None, None, n_cmp, hd), lambda b, g, i: (b, g, 0, 0))
    sspec3 = pl.BlockSpec((None, None, seq, 2 * hd), lambda b, g, i: (b, g, 0, 0))
    y = pl.pallas_call(
        functools.partial(_nsa_body, rep=rep, hd=hd, n_top=n_top, n_blk=n_blk, kb_sel=kb_sel),
        grid=(bsz, B_KV, seq // tq),
        in_specs=[pl.BlockSpec(memory_space=pltpu.SMEM), pl.BlockSpec(memory_space=pltpu.SMEM),
                  pl.BlockSpec((None, tq, rw), lambda b, g, i: (b, i, g)),
                  pl.BlockSpec((None, tq, rw), lambda b, g, i: (b, i, inner // rw + g)),
                  pl.BlockSpec((None, None, tq, 3 * rep), lambda b, g, i: (b, g, i, 0)),
                  pl.BlockSpec((1, hd), lambda b, g, i: (0, 0)),
                  cspec3, cspec3, sspec3, sspec3, sspec3, sspec3,
                  pl.BlockSpec((n_cmp, hd), lambda b, g, i: (0, 0))],
        out_specs=pl.BlockSpec((None, tq, rw), lambda b, g, i: (b, i, g)),
        out_shape=jax.ShapeDtypeStruct((bsz, seq, inner), BF16),
        scratch_shapes=[pltpu.VMEM((rep * tq, 2 * hd), BF16), pltpu.VMEM((rep * tq, 2 * hd), F32),
                        pltpu.VMEM((rep * tq, hd), F32), pltpu.VMEM((rep * tq, hd), F32),
                        pltpu.SMEM((seq // kb_sel,), jnp.int32)],
        compiler_params=_params("parallel", "parallel", "arbitrary"), name="b_attn",
    )(slopes, pieces, u, u, gts, q_norm_w.reshape(1, hd), kc, vc, ks, vs, kw, vw, jnp.asarray(ov, dtype=F32))
    return _matmul(y.reshape(t, inner), w_out, resid=x2d, name="b_out")


def kernel(x, norm_w, a_w_in, a_conv_w, a_conv_b, a_gate_b, a_out_norm_w, a_w_out, b_w_in, b_cmp_pe, b_cmp_wk, b_cmp_wv, b_q_norm_w, b_k_norm_w, b_w_out, c_w_in, c_conv_w, c_conv_b, c_w_a, c_b_a, c_w_x, c_b_x, c_lambda, c_w_out, d_w_in, d_q_norm_w, d_k_norm_w, d_w_out):
    bsz, seq, d = x.shape
    depth = norm_w.shape[0]
    x2d = x.reshape(bsz * seq, d)
    for layer in range(depth):
        kind, j = layer % 4, layer // 4
        hn = _rmsnorm(x2d, norm_w[layer])
        if kind == 0:
            x2d = _mlstm_layer(x2d, hn, bsz, seq, a_w_in[j], a_conv_w[j], a_conv_b[j], a_gate_b[j],
                               a_out_norm_w[j], a_w_out[j])
        elif kind == 1:
            x2d = _nsa_layer(x2d, hn, bsz, seq, b_w_in[j], b_cmp_pe[j], b_cmp_wk[j], b_cmp_wv[j], b_q_norm_w[j],
                             b_k_norm_w[j], b_w_out[j])
        elif kind == 2:
            x2d = _rglru_layer(x2d, hn, bsz, seq, c_w_in[j], c_conv_w[j], c_conv_b[j], c_w_a[j], c_b_a[j],
                               c_w_x[j], c_b_x[j], c_lambda[j], c_w_out[j])
        else:
            x2d = _dilated_layer(x2d, hn, bsz, seq, d_w_in[j], d_q_norm_w[j], d_k_norm_w[j], d_w_out[j])
    return x2d.reshape(bsz, seq, d)
```

```python
import functools
import math

import ml_dtypes
import numpy as np
import jax
import jax.numpy as jnp
from jax import lax
from jax.experimental import pallas as pl
from jax.experimental.pallas import tpu as pltpu

F32 = jnp.float32
BF16 = jnp.bfloat16

EPS = 1e-6
NEG = -1e30
FORCE = 1e4
CONV_W = 4

LANES = 128
SUBLANES = 8
V7X_VMEM_LIMIT = 56 * 1024 * 1024

A_CHUNK = 128
B_CMP_LEN = 32
B_CMP_STRIDE = 16
B_SEL_LEN = 64
B_N_SEL = 16
B_WIN = 512
B_KV = 4
NSA_POS_SPLIT = 128
NSA_SEL_OFF = 6
C_BLOCKS = 16
C_EXP = 8.0
D_PATTERNS = ((128, 1), (512, 4), (2048, 16))
D_QBLK = 128


def _params(*sem):
    return pltpu.CompilerParams(dimension_semantics=sem, vmem_limit_bytes=V7X_VMEM_LIMIT)


def _sigmoid(x):
    return 0.5 * jnp.tanh(0.5 * x) + 0.5


def _silu(x):
    return x * _sigmoid(x)


def _softplus(x):
    return jnp.maximum(x, 0.0) + jnp.log1p(jnp.exp(-jnp.abs(x)))


def _head_rms(x, w):
    return x * lax.rsqrt(jnp.mean(x * x, axis=-1, keepdims=True) + EPS) * w


def _dot(a, b):
    return jnp.dot(a.astype(BF16), b.astype(BF16), preferred_element_type=F32)


def _dot_nt(a, b):
    return lax.dot_general(a.astype(BF16), b.astype(BF16), (((1,), (1,)), ((), ())),
                           preferred_element_type=F32)


def _dot_tn(a, b):
    return lax.dot_general(a.astype(BF16), b.astype(BF16), (((0,), (0,)), ((), ())),
                           preferred_element_type=F32)


def _pick(n, cands):
    for c in cands:
        if n % c == 0:
            return c
    raise ValueError(f"no tile for {n}")


def _rmsnorm_body(x_ref, w_ref, o_ref):
    x = x_ref[...]
    o_ref[...] = _head_rms(x, w_ref[...]).astype(o_ref.dtype)


def _rmsnorm(x2d, w):
    t, d = x2d.shape
    tm = _pick(t, (256, 128, 8))
    return pl.pallas_call(
        _rmsnorm_body, grid=(t // tm,),
        in_specs=[pl.BlockSpec((tm, d), lambda i: (i, 0)), pl.BlockSpec((1, d), lambda i: (0, 0))],
        out_specs=pl.BlockSpec((tm, d), lambda i: (i, 0)),
        out_shape=jax.ShapeDtypeStruct((t, d), BF16),
        compiler_params=_params("parallel"), name="rmsnorm")(x2d, w.reshape(1, d))


def _mm_body(a_ref, w_ref, *rest, transposed):
    r_ref, o_ref = rest if len(rest) == 2 else (None, rest[0])
    if transposed:
        acc = lax.dot_general(a_ref[...], w_ref[...], (((1,), (1,)), ((), ())), preferred_element_type=F32)
    else:
        acc = jnp.dot(a_ref[...], w_ref[...], preferred_element_type=F32)
    if r_ref is not None:
        acc = r_ref[...] + acc
    if len(o_ref.shape) == 2:
        o_ref[...] = acc
    else:
        gw = o_ref.shape[2]
        for c in range(o_ref.shape[0]):
            o_ref[c] = acc[:, c * gw:(c + 1) * gw]


def _matmul(a, w, n=None, col0=0, resid=None, transposed=False, group=None, name="matmul"):
    m, k = a.shape
    n = w.shape[1] if n is None else n
    tm = _pick(m, (1024, 512, 256, 128))
    tn = _pick(n, tuple(c for c in (1024, 768, 512, 384, 256, 128)
                        if 2 * (k * c * 2 + tm * k * 2 + (2 if resid is not None else 1) * tm * c * 4)
                        <= V7X_VMEM_LIMIT - (4 << 20)))
    assert col0 % tn == 0
    cb = col0 // tn
    if transposed:
        w = jnp.swapaxes(w, 0, 1).astype(BF16)
        w_spec = pl.BlockSpec((tn, k), lambda i, j: (cb + j, 0))
    else:
        w = w.astype(BF16)
        w_spec = pl.BlockSpec((k, tn), lambda i, j: (0, cb + j))
    in_specs = [pl.BlockSpec((tm, k), lambda i, j: (i, 0)), w_spec]
    args = [a, w]
    body = functools.partial(_mm_body, transposed=transposed)
    if resid is not None:
        in_specs.append(pl.BlockSpec((tm, tn), lambda i, j: (i, j)))
        args.append(resid)
    if group is None:
        out_spec = pl.BlockSpec((tm, tn), lambda i, j: (i, j))
        out_shape = jax.ShapeDtypeStruct((m, n), F32)
    else:
        assert tn % group == 0 and group % LANES == 0 and resid is None
        out_spec = pl.BlockSpec((tn // group, tm, group), lambda i, j: (j, i, 0))
        out_shape = jax.ShapeDtypeStruct((n // group, m, group), F32)
    return pl.pallas_call(
        body, grid=(m // tm, n // tn), in_specs=in_specs, out_specs=out_spec, out_shape=out_shape,
        compiler_params=_params("parallel", "parallel"), name=name)(*args)


def _mlstm_body(gb_ref, q_ref, k_ref, v_ref, og_ref, z_ref, gcol_ref, grow_ref, cwq_ref, cwk_ref,
                cbq_ref, cbk_ref, onw_ref, y_ref, c_scr, n_scr, m_scr, extq, extk, *, hb, dqk, dv):
    c = pl.program_id(2)
    L = q_ref.shape[0]

    @pl.when(c == 0)
    def _():
        c_scr[...] = jnp.zeros_like(c_scr)
        n_scr[...] = jnp.zeros_like(n_scr)
        m_scr[...] = jnp.zeros_like(m_scr)
        extq[L:L + 8, :] = jnp.zeros((8, extq.shape[1]), F32)
        extk[L:L + 8, :] = jnp.zeros((8, extk.shape[1]), F32)

    def conv_silu(ext, x_ref, cw_ref, cb_ref):
        ext[0:8, :] = ext[L:L + 8, :]
        ext[8:L + 8, :] = x_ref[...]
        acc = cb_ref[...] + cw_ref[0:1, :] * ext[pl.ds(8 - CONV_W + 1, L), :]
        for j in range(1, CONV_W):
            acc = acc + cw_ref[j:j + 1, :] * ext[pl.ds(8 - CONV_W + 1 + j, L), :]
        return _silu(acc)

    q_all = conv_silu(extq, q_ref, cwq_ref, cbq_ref) * (dqk ** -0.5)
    k_all = conv_silu(extk, k_ref, cwk_ref, cbk_ref)
    t_io = lax.broadcasted_iota(jnp.int32, (L, L), 0)
    s_io = lax.broadcasted_iota(jnp.int32, (L, L), 1)
    causal = s_io <= t_io

    for hh in range(hb):
        h = pl.program_id(1) * hb + hh
        qs, vs = slice(hh * dqk, (hh + 1) * dqk), slice(hh * dv, (hh + 1) * dv)
        q, k, v = q_all[:, qs], k_all[:, qs], v_ref[:, vs]
        bi = gb_ref[0, h]
        bf = gb_ref[1, h]
        gcol = gcol_ref[hh]
        grow = grow_ref[hh]
        li_col = gcol[:, 0:1] + bi
        li_row = grow[0:1, :] + bi
        lf_col = -_softplus(-(gcol[:, 1:2] + bf))
        lf_row = -_softplus(-(grow[1:2, :] + bf))

        b_col = jnp.sum(jnp.where(causal, lf_row, 0.0), axis=1, keepdims=True)
        b_row = jnp.sum(jnp.where(t_io <= s_io, lf_col, 0.0), axis=0, keepdims=True)
        b_last = jnp.sum(lf_row, axis=1, keepdims=True)
        m_st = m_scr[hh]

        dmat = jnp.where(causal, b_col - b_row + li_row, NEG)
        inter = b_col + m_st
        m_t = jnp.maximum(inter, jnp.max(dmat, axis=1, keepdims=True))
        w_intra = jnp.exp(dmat - m_t)
        w_inter = jnp.exp(inter - m_t)
        a = w_intra * _dot_nt(q, k)
        c_st = c_scr[hh]
        n_st = n_scr[hh]
        num = _dot(a, v) + w_inter * _dot(q, c_st)
        qn = jnp.sum(a, axis=1, keepdims=True) + w_inter * jnp.sum(q * n_st, axis=1, keepdims=True)
        hc = num / jnp.maximum(jnp.abs(qn), jnp.exp(-m_t))

        g_row = b_last - b_row + li_row
        g_col = b_last - b_col + li_col
        m_new = jnp.maximum(b_last + m_st, jnp.max(g_row, axis=1, keepdims=True))
        ws_col = jnp.exp(g_col - m_new)
        decay = jnp.exp(b_last + m_st - m_new)
        kw = k * ws_col
        c_scr[hh] = decay * c_st + _dot_tn(kw, v)
        n_scr[hh] = decay * n_st + jnp.sum(kw, axis=0, keepdims=True)
        m_scr[hh] = m_new

        hn = _head_rms(hc, onw_ref[:, vs])
        y_ref[:, vs] = (hn * _sigmoid(og_ref[:, vs]) * _silu(z_ref[:, vs])).astype(y_ref.dtype)


def _mlstm_layer(x2d, hn, bsz, seq, w_in, conv_w, conv_b, gate_b, out_norm_w, w_out):
    t, d = x2d.shape
    heads = gate_b.shape[-1]
    qk = conv_w.shape[-1] // 2
    inner = out_norm_w.shape[-1]
    dqk, dv = qk // heads, inner // heads
    n_main = 2 * qk + 3 * inner
    L = A_CHUNK
    nc = seq // L
    hb = heads
    assert dqk % LANES == 0 and dv % LANES == 0 and 2 * heads <= LANES and heads % hb == 0

    u = _matmul(hn, w_in, n=n_main, transposed=True, name="a_in").reshape(bsz, seq, n_main)
    gts = _matmul(hn, w_in, n=LANES, col0=n_main, transposed=True, name="a_gates")
    gts = gts[:, :2 * heads].reshape(bsz, nc, L, 2, heads)
    g_col = jnp.transpose(gts, (0, 4, 1, 2, 3))
    g_row = jnp.transpose(gts, (0, 4, 1, 3, 2))

    wq, wv = hb * dqk, hb * dv
    assert qk % wq == 0 and 2 * qk % wv == 0 and inner % wv == 0
    kq0, kv0, ko0, kz0 = qk // wq, 2 * qk // wv, (2 * qk + inner) // wv, (2 * qk + 2 * inner) // wv
    blk = lambda w, off: pl.BlockSpec((None, L, w), lambda b, h, c: (b, c, off + h))
    par = lambda r, w, off: pl.BlockSpec((r, w), lambda b, h, c: (0, off + h))
    y = pl.pallas_call(
        functools.partial(_mlstm_body, hb=hb, dqk=dqk, dv=dv),
        grid=(bsz, heads // hb, nc),
        in_specs=[pl.BlockSpec(memory_space=pltpu.SMEM),
                  blk(wq, 0), blk(wq, kq0), blk(wv, kv0), blk(wv, ko0), blk(wv, kz0),
                  pl.BlockSpec((None, hb, None, L, 2), lambda b, h, c: (b, h, c, 0, 0)),
                  pl.BlockSpec((None, hb, None, 2, L), lambda b, h, c: (b, h, c, 0, 0)),
                  par(CONV_W, wq, 0), par(CONV_W, wq, kq0), par(1, wq, 0), par(1, wq, kq0),
                  par(1, wv, 0)],
        out_specs=pl.BlockSpec((None, L, wv), lambda b, h, c: (b, c, h)),
        out_shape=jax.ShapeDtypeStruct((bsz, seq, inner), BF16),
        scratch_shapes=[pltpu.VMEM((hb, dqk, dv), F32), pltpu.VMEM((hb, 1, dqk), F32), pltpu.VMEM((hb, 1, 1), F32),
                        pltpu.VMEM((L + 8, wq), F32), pltpu.VMEM((L + 8, wq), F32)],
        compiler_params=_params("parallel", "parallel", "arbitrary"), name="a_mlstm",
    )(gate_b, u, u, u, u, u, g_col, g_row, conv_w, conv_w, conv_b.reshape(1, -1), conv_b.reshape(1, -1),
      out_norm_w.reshape(1, -1))
    return _matmul(y.reshape(t, inner), w_out, resid=x2d, name="a_out")


def _rglru_body(x_ref, z_ref, cw_ref, cb_ref, wa_ref, wx_ref, ba_ref, bx_ref, lam_ref, y_ref,
                ext, xc_scr, a_scr, b_scr, h_scr, *, starts, win, chunk, scan_chunk):
    s = pl.program_id(1)
    ts, width = x_ref.shape

    @pl.when(s == 0)
    def _():
        ext[ts:ts + 8, :] = jnp.zeros((8, width), F32)
        h_scr[...] = jnp.zeros_like(h_scr)

    ext[0:8, :] = ext[ts:ts + 8, :]
    ext[8:ts + 8, :] = x_ref[...]
    for c0 in range(0, width, chunk):
        cs = slice(c0, c0 + chunk)
        xc = cb_ref[:, cs] + cw_ref[0:1, cs] * ext[pl.ds(8 - CONV_W + 1, ts), cs]
        for j in range(1, CONV_W):
            xc = xc + cw_ref[j:j + 1, cs] * ext[pl.ds(8 - CONV_W + 1 + j, ts), cs]
        xc_scr[:, cs] = xc

    a_scr[...] = jnp.zeros_like(a_scr)
    b_scr[...] = jnp.zeros_like(b_scr)
    for n, a0 in enumerate(starts):
        ws = slice(a0, a0 + win)
        xw = xc_scr[:, ws].astype(BF16)
        a_scr[:, ws] += jnp.dot(xw, wa_ref[n], preferred_element_type=F32)
        b_scr[:, ws] += jnp.dot(xw, wx_ref[n], preferred_element_type=F32)

    for c0 in range(0, width, chunk):
        cs = slice(c0, c0 + chunk)
        r = _sigmoid(a_scr[:, cs] + ba_ref[:, cs])
        ig = _sigmoid(b_scr[:, cs] + bx_ref[:, cs])
        log_a = (-C_EXP) * _softplus(-lam_ref[:, cs]) * r
        a = jnp.exp(log_a)
        a_scr[:, cs] = a
        b_scr[:, cs] = jnp.sqrt(-jnp.tanh(log_a) * (a * a + 1.0)) * (ig * xc_scr[:, cs])

    for c0 in range(0, width, scan_chunk):
        cs = slice(c0, c0 + scan_chunk)

        def row(t, h, cs=cs):
            h = a_scr[pl.ds(t, 1), cs] * h + b_scr[pl.ds(t, 1), cs]
            b_scr[pl.ds(t, 1), cs] = h
            return h

        h_scr[:, cs] = lax.fori_loop(0, ts, row, h_scr[:, cs], unroll=8)

    y_ref[...] = (b_scr[...] * _silu(z_ref[...])).astype(y_ref.dtype)


def _rglru_layer(x2d, hn, bsz, seq, w_in, conv_w, conv_b, w_a, b_a, w_x, b_x, lam, w_out):
    t, d = x2d.shape
    width = conv_w.shape[-1]
    nblk = w_a.shape[0]
    blk = width // nblk
    assert width % LANES == 0
    win = LANES
    while True:
        starts = tuple(min(n * blk // LANES * LANES, width - win) for n in range(nblk))
        if all(n * blk - a0 + blk <= win for n, a0 in enumerate(starts)):
            break
        win += LANES

    def window_weights(w):
        return jnp.stack([jnp.pad(w[n], ((n * blk - a0, win - blk - (n * blk - a0)),) * 2)
                          for n, a0 in enumerate(starts)]).astype(BF16)

    u = _matmul(hn, w_in, name="c_in").reshape(bsz, seq, 2 * width)
    ts = _pick(seq, (128, 64, 8))
    chunk = _pick(width, (768, 512, 384, 256, 128))
    scan_chunk = _pick(width, (1792, 1536, 1024, 896, 768, 512, 384, 256, 128))
    full = lambda r: pl.BlockSpec((r, width), lambda b, s: (0, 0))
    wspec = pl.BlockSpec((nblk, win, win), lambda b, s: (0, 0, 0), pipeline_mode=pl.Buffered(1))
    row = lambda v: v.reshape(1, width)
    y = pl.pallas_call(
        functools.partial(_rglru_body, starts=starts, win=win, chunk=chunk, scan_chunk=scan_chunk),
        grid=(bsz, seq // ts),
        in_specs=[pl.BlockSpec((None, ts, width), lambda b, s: (b, s, 0)),
                  pl.BlockSpec((None, ts, width), lambda b, s: (b, s, 1)),
                  full(CONV_W), full(1), wspec, wspec, full(1), full(1), full(1)],
        out_specs=pl.BlockSpec((None, ts, width), lambda b, s: (b, s, 0)),
        out_shape=jax.ShapeDtypeStruct((bsz, seq, width), BF16),
        scratch_shapes=[pltpu.VMEM((ts + 8, width), F32), pltpu.VMEM((ts, width), F32),
                        pltpu.VMEM((ts, width), F32), pltpu.VMEM((ts, width), F32), pltpu.VMEM((1, width), F32)],
        compiler_params=_params("parallel", "arbitrary"), name="c_rglru",
    )(u, u, conv_w, row(conv_b), window_weights(w_a), window_weights(w_x), row(b_a), row(b_x), row(lam))
    return _matmul(y.reshape(t, width), w_out, resid=x2d, name="c_out")


def _dilated_body(sl_ref, *refs, patterns, tq, mix_rows):
    n_pat = len(patterns)
    qkv = [refs[5 * g:5 * g + 5] for g in range(n_pat)]
    z_ref, qw_ref, kw_ref, y_ref, knbuf, vbuf, obuf, lbuf = refs[5 * n_pat:]
    first_tile = pl.program_id(1) == 0
    slope = sl_ref[pl.program_id(2)]
    tt, hd = z_ref.shape
    scale = hd ** -0.5
    qi = lax.broadcasted_iota(jnp.int32, (tq, 2 * tq), 0)
    kj = lax.broadcasted_iota(jnp.int32, (tq, 2 * tq), 1)
    steps = qi + tq - kj

    for g, (window, dil) in enumerate(patterns):
        q_ref, kp_ref, kc_ref, vp_ref, vc_ref = qkv[g]
        halo = tq * dil
        knbuf[0:halo, :] = _head_rms(kp_ref[...], kw_ref[...])
        knbuf[halo:halo + tt, :] = _head_rms(kc_ref[...], kw_ref[...])
        vbuf[0:halo, :] = vp_ref[...]
        vbuf[halo:halo + tt, :] = vc_ref[...]
        bias = jnp.where((steps >= 0) & (steps <= window // dil), (-slope) * (steps * dil).astype(F32), NEG)
        bias_first = jnp.where((kj >= tq) | jnp.logical_not(first_tile), bias, NEG)
        for r in range(dil):
            for j in range(tt // halo):
                start = r + halo * j
                rows = lambda n, st=start: pl.ds(st, n, stride=dil) if dil > 1 else pl.ds(st, n)
                qn = _head_rms(q_ref[rows(tq), :], qw_ref[...]) * scale
                s = _dot_nt(qn, knbuf[rows(2 * tq), :]) + (bias_first if j == 0 else bias)
                m = jnp.max(s, axis=1, keepdims=True)
                p = jnp.exp(s - m)
                den = jnp.sum(p, axis=1, keepdims=True)
                obuf[g, rows(tq), :] = _dot(p / den, vbuf[rows(2 * tq), :])
                lbuf[g, rows(tq), :] = jnp.broadcast_to(m + jnp.log(den), (tq, hd))

    for c in range(0, tt, mix_rows):
        cs = slice(c, c + mix_rows)
        ls = [lbuf[g, cs, :] for g in range(n_pat)]
        mx = functools.reduce(jnp.maximum, ls)
        es = [jnp.exp(l - mx) for l in ls]
        tot = functools.reduce(lambda a, b: a + b, es)
        o = sum((e / tot) * obuf[g, cs, :] for g, e in enumerate(es))
        y_ref[cs, :] = (o * _silu(z_ref[cs, :])).astype(y_ref.dtype)


def _dilated_layer(x2d, hn, bsz, seq, w_in, q_norm_w, k_norm_w, w_out):
    t, d = x2d.shape
    hd = q_norm_w.shape[-1]
    inner = w_out.shape[0]
    heads = inner // hd
    n_pat = len(D_PATTERNS)
    cols = 3 * n_pat * inner + inner
    tq = D_QBLK
    max_dil = max(dil for _, dil in D_PATTERNS)
    tt = tq * max_dil
    assert hd == LANES and seq % tt == 0
    assert all(window // dil <= tq and tt % (tq * dil) == 0 for window, dil in D_PATTERNS)
    slopes = jnp.asarray(2.0 ** (-8.0 * np.arange(1, heads + 1) / heads), dtype=F32)
    u = _matmul(hn, w_in, group=hd, name="d_in").reshape(cols // hd, bsz, seq, hd)

    in_specs = [pl.BlockSpec(memory_space=pltpu.SMEM)]
    args = [slopes]
    for g, (window, dil) in enumerate(D_PATTERNS):
        halo = tq * dil
        cur = lambda part, g=g: pl.BlockSpec((None, None, tt, hd),
                                             lambda b, s, h: ((3 * g + part) * heads + h, b, s, 0))
        prev = lambda part, g=g, halo=halo: pl.BlockSpec(
            (None, None, halo, hd),
            lambda b, s, h: ((3 * g + part) * heads + h, b, jnp.maximum(s * (tt // halo) - 1, 0), 0))
        in_specs += [cur(0), prev(1), cur(1), prev(2), cur(2)]
        args += [u] * 5
    wspec = pl.BlockSpec((1, hd), lambda b, s, h: (0, 0))
    in_specs += [pl.BlockSpec((None, None, tt, hd), lambda b, s, h: (3 * n_pat * heads + h, b, s, 0)), wspec, wspec]
    args += [u, q_norm_w.reshape(1, hd), k_norm_w.reshape(1, hd)]
    y = pl.pallas_call(
        functools.partial(_dilated_body, patterns=D_PATTERNS, tq=tq, mix_rows=_pick(tt, (256, 128))),
        grid=(bsz, seq // tt, heads), in_specs=in_specs,
        out_specs=pl.BlockSpec((None, tt, hd), lambda b, s, h: (b, s, h)),
        out_shape=jax.ShapeDtypeStruct((bsz, seq, inner), BF16),
        scratch_shapes=[pltpu.VMEM((2 * tt, hd), F32), pltpu.VMEM((2 * tt, hd), F32),
                        pltpu.VMEM((n_pat, tt, hd), F32), pltpu.VMEM((n_pat, tt, hd), F32)],
        compiler_params=_params("parallel", "parallel", "parallel"), name="d_attn",
    )(*args)
    return _matmul(y.reshape(t, inner), w_out, resid=x2d, name="d_out")


def _nsa_prep_body(k0_ref, v0_ref, k1_ref, v1_ref, k2_ref, v2_ref, pe_ref, wk_ref, wv_ref, knw_ref,
                   kc_ref, vc_ref, ks_ref, vs_ref, kw_ref, vw_ref, *, stride):
    n_c = kc_ref.shape[0]

    def compress(x_ref, w_ref, pe):
        p1 = jnp.zeros(kc_ref.shape, F32)
        p2 = jnp.zeros(kc_ref.shape, F32)
        for l in range(stride):
            x = x_ref[pl.ds(l, n_c, stride=stride), :]
            p1 = p1 + _dot(x + pe[l:l + 1, :], w_ref[l])
            p2 = p2 + _dot(x + pe[stride + l:stride + l + 1, :], w_ref[stride + l])
        return p1 + pltpu.roll(p2, n_c - 1, 0)

    kc_ref[...] = _head_rms(compress(k0_ref, wk_ref, pe_ref[0]), knw_ref[0:1, :])
    vc_ref[...] = compress(v0_ref, wv_ref, pe_ref[1])

    seq, hd = k1_ref.shape
    pos = lax.broadcasted_iota(jnp.int32, (seq, hd), 0)
    lane = lax.broadcasted_iota(jnp.int32, (seq, hd), 1)
    split_shift = NSA_POS_SPLIT.bit_length() - 1
    pos_hi = ((pos >> split_shift) << split_shift).astype(F32)
    pos_lo = (pos & (NSA_POS_SPLIT - 1)).astype(F32)
    pos_cols = jnp.where(lane < 3, pos_hi, jnp.where(lane < NSA_SEL_OFF, pos_lo, 0.0))
    sel_blk = pos >> (B_SEL_LEN.bit_length() - 1)
    onehot = jnp.where(lane - NSA_SEL_OFF == sel_blk, NEG, 0.0)
    ones = jnp.ones((seq, hd), vs_ref.dtype)
    ks_ref[:, 0:hd] = _head_rms(k1_ref[...], knw_ref[1:2, :]).astype(ks_ref.dtype)
    ks_ref[:, hd:2 * hd] = (pos_cols + onehot).astype(ks_ref.dtype)
    vs_ref[:, 0:hd] = v1_ref[...].astype(vs_ref.dtype)
    vs_ref[:, hd:2 * hd] = ones
    kw_ref[:, 0:hd] = _head_rms(k2_ref[...], knw_ref[2:3, :]).astype(kw_ref.dtype)
    kw_ref[:, hd:2 * hd] = pos_cols.astype(kw_ref.dtype)
    vw_ref[:, 0:hd] = v2_ref[...].astype(vw_ref.dtype)
    vw_ref[:, hd:2 * hd] = ones


def _nsa_body(sl_ref, sp_ref, q_ref, z_ref, g_ref, qw_ref, kc_ref, vc_ref, ks_ref, vs_ref, kw_ref, vw_ref, ov_ref,
              y_ref, q2, acc, m_scr, o_scr, used, *, rep, hd, n_top, n_blk, kb_sel):
    g = pl.program_id(1)
    i = pl.program_id(2)
    tq = q_ref.shape[0]
    n_cmp = kc_ref.shape[0]
    t0 = i * tq
    scale = hd ** -0.5
    rows = lambda r: slice(r * tq, (r + 1) * tq)

    for r in range(rep):
        q2[rows(r), 0:hd] = (_head_rms(q_ref[:, r * hd:(r + 1) * hd], qw_ref[...]) * scale).astype(q2.dtype)
    gts = _sigmoid(g_ref[...])

    def gate(r, branch):
        return gts[:, 3 * r + branch:3 * r + branch + 1]

    t_col = t0 + lax.broadcasted_iota(jnp.int32, (tq, 1), 0)

    cmp_end = lax.broadcasted_iota(jnp.int32, (tq, n_cmp), 1) * B_CMP_STRIDE + (B_CMP_LEN - 1)
    dist_ci = t_col - cmp_end
    mask_c = dist_ci >= 0
    dist_c = dist_ci.astype(F32)
    s_all = _dot_nt(q2[:, 0:hd], kc_ref[...])
    psum = jnp.zeros((tq, n_cmp), F32)
    ps = []
    for r in range(rep):
        s = s_all[rows(r), :] - sl_ref[g, r] * dist_c
        s = jnp.where(mask_c, s, NEG)
        m = jnp.max(s, axis=1, keepdims=True)
        p = jnp.where(mask_c, jnp.exp(s - m), 0.0)
        den = jnp.sum(p, axis=1, keepdims=True)
        p = p / jnp.where(den > 0, den, 1.0)
        psum = psum + p
        ps.append(p.astype(BF16))
    o_c = _dot(jnp.concatenate(ps, axis=0), vc_ref[...])
    for r in range(rep):
        o_scr[rows(r), :] = gate(r, 0) * o_c[rows(r), :]

    imp = jnp.dot(psum, ov_ref[...], preferred_element_type=F32, precision=lax.Precision.HIGHEST)
    lane = lax.broadcasted_iota(jnp.int32, (tq, hd), 1)
    blk = lane - NSA_SEL_OFF
    in_range = (blk >= 0) & (blk < n_blk)
    cur = t_col >> (B_SEL_LEN.bit_length() - 1)
    forced = (blk == 0) | (blk == cur) | (blk == cur - 1)
    imp = jnp.where(forced, imp + FORCE, imp)
    imp = jnp.where(in_range & (blk * B_SEL_LEN <= t_col), imp, NEG)
    imp_t = imp.T
    n_grp = -(-(NSA_SEL_OFF + n_blk) // SUBLANES)
    grp = [imp_t[SUBLANES * k:SUBLANES * (k + 1), :] for k in range(n_grp)]
    cnt = [jnp.zeros((SUBLANES, tq), F32) for _ in range(n_grp)]
    sub = lax.broadcasted_iota(jnp.int32, (SUBLANES, tq), 0)
    for mb in range(n_blk):
        slot = NSA_SEL_OFF + mb
        row = imp_t[slot:slot + 1, :]
        for k in range(n_grp):
            if SUBLANES * k > slot:
                beats = row >= grp[k]
            elif SUBLANES * (k + 1) <= slot:
                beats = row > grp[k]
            else:
                beats = (row > grp[k]) | ((row == grp[k]) & (sub > slot - SUBLANES * k))
            cnt[k] = cnt[k] + jnp.where(beats, 1.0, 0.0)
    cnt_t = jnp.concatenate(cnt + [jnp.zeros((hd - SUBLANES * n_grp, tq), F32)], axis=0)
    not_sel = jnp.where((cnt_t < n_top) & (imp_t > NEG / 2), 0.0, 1.0).T

    sel_bf = jnp.where(in_range, 1.0 - not_sel, 0.0).astype(BF16)
    per_slot = jnp.dot(jnp.ones((SUBLANES, tq), BF16), sel_bf, preferred_element_type=F32)
    slot_io = lax.broadcasted_iota(jnp.int32, (hd, hd), 0) - NSA_SEL_OFF
    kb_io = lax.broadcasted_iota(jnp.int32, (hd, hd), 1)
    kb_shift = (kb_sel // B_SEL_LEN).bit_length() - 1
    group = jnp.where((slot_io >= 0) & ((slot_io >> kb_shift) == kb_io), 1.0, 0.0).astype(BF16)
    per_kb = jnp.dot(per_slot.astype(BF16), group, preferred_element_type=F32)
    lane_s = lax.broadcasted_iota(jnp.int32, (SUBLANES, hd), 1)
    for kb in range(used.shape[0]):
        used[kb] = jnp.max(jnp.where(lane_s == kb, per_kb, 0.0)).astype(jnp.int32)

    for r in range(rep):
        piece = jnp.where((lane == 0) | (lane == 3), sp_ref[g, 3 * r],
                          jnp.where((lane == 1) | (lane == 4), sp_ref[g, 3 * r + 1],
                                    jnp.where((lane == 2) | (lane == 5), sp_ref[g, 3 * r + 2], 0.0)))
        q2[rows(r), hd:2 * hd] = jnp.where(in_range, not_sel, piece).astype(q2.dtype)

    acc[...] = jnp.zeros_like(acc)
    m_scr[...] = jnp.full(m_scr.shape, NEG, F32)
    n_chunk = kb_sel // hd

    def sel_block(start, bias):
        kblk = ks_ref[pl.ds(start, kb_sel), :]
        vblk = vs_ref[pl.ds(start, kb_sel), :]
        s_all = lax.dot_general(q2[...], kblk, (((1,), (1,)), ((), ())), preferred_element_type=F32)
        for r in range(rep):
            s = s_all[rows(r), :]
            if bias is not None:
                s = s + bias
            mx = functools.reduce(jnp.maximum, [s[:, c * hd:(c + 1) * hd] for c in range(n_chunk)])
            m_old = m_scr[rows(r), :]
            m_new = jnp.maximum(m_old, jnp.max(mx, axis=1, keepdims=True))
            alpha = jnp.exp(m_old - m_new)
            p = jnp.exp(s - jnp.concatenate([m_new] * n_chunk, axis=1)).astype(BF16)
            acc[rows(r), :] = (jnp.concatenate([alpha, alpha], axis=1) * acc[rows(r), :]
                               + jnp.dot(p, vblk, preferred_element_type=F32))
            m_scr[rows(r), :] = m_new

    def sel_step(kb, carry):
        @pl.when(used[kb] > 0)
        def _():
            sel_block(pl.multiple_of(kb * kb_sel, kb_sel), None)

        return carry

    n_full = lax.div(t0, kb_sel)
    lax.fori_loop(0, n_full, sel_step, 0)
    d_start = pl.multiple_of(n_full * kb_sel, kb_sel)
    d_pos = d_start + lax.broadcasted_iota(jnp.int32, (tq, kb_sel), 1)
    sel_block(d_start, jnp.where(d_pos <= t_col, 0.0, NEG))
    for r in range(rep):
        o_scr[rows(r), :] += gate(r, 1) * (acc[rows(r), 0:hd] / acc[rows(r), hd:2 * hd])

    n_w = B_WIN + tq
    w_start = pl.multiple_of(jnp.maximum(t0 - B_WIN, 0), tq)
    w_dist = t_col - (w_start + lax.broadcasted_iota(jnp.int32, (tq, n_w), 1))
    w_bias = jnp.where((w_dist >= 0) & (w_dist < B_WIN), 0.0, NEG)
    kblk = kw_ref[pl.ds(w_start, n_w), :]
    vblk = vw_ref[pl.ds(w_start, n_w), :]
    s_all = lax.dot_general(q2[...], kblk, (((1,), (1,)), ((), ())), preferred_element_type=F32)
    for r in range(rep):
        s = s_all[rows(r), :] + w_bias
        p = jnp.exp(s - jnp.max(s, axis=1, keepdims=True)).astype(BF16)
        res = jnp.dot(p, vblk, preferred_element_type=F32)
        o_scr[rows(r), :] += gate(r, 2) * (res[:, 0:hd] / res[:, hd:2 * hd])

    for r in range(rep):
        hs = slice(r * hd, (r + 1) * hd)
        y_ref[:, hs] = (o_scr[rows(r), :] * _silu(z_ref[:, hs])).astype(y_ref.dtype)


def _nsa_layer(x2d, hn, bsz, seq, w_in, cmp_pe, cmp_wk, cmp_wv, q_norm_w, k_norm_w, w_out):
    t, d = x2d.shape
    hd = q_norm_w.shape[-1]
    inner = w_out.shape[0]
    heads = inner // hd
    rep = heads // B_KV
    kvw = B_KV * hd
    n_main = 2 * inner + 6 * kvw
    assert B_CMP_LEN == 2 * B_CMP_STRIDE and seq % B_CMP_STRIDE == 0 and 3 * heads <= LANES
    n_cmp = seq // B_CMP_STRIDE
    n_blk = seq // B_SEL_LEN
    n_top = min(B_N_SEL, n_blk)
    tq = 128
    kb_sel = 512
    assert hd == LANES and NSA_SEL_OFF + n_blk <= hd and NSA_POS_SPLIT == hd
    assert kb_sel % tq == 0 and seq % kb_sel == 0 and B_WIN % tq == 0 and seq >= B_WIN + tq

    u = _matmul(hn, w_in, n=n_main, transposed=True, name="b_in").reshape(bsz, seq, n_main)
    gts = _matmul(hn, w_in, n=LANES, col0=n_main, transposed=True, name="b_gates")
    gts = gts[:, :3 * heads].reshape(bsz, seq, B_KV, 3 * rep)
    gts = jnp.transpose(gts, (0, 2, 1, 3))

    kv0 = 2 * inner // hd
    kvspec = lambda br, kv: pl.BlockSpec((None, seq, hd), lambda b, g: (b, 0, kv0 + (br * 2 + kv) * B_KV + g))
    cspec = pl.BlockSpec((None, None, n_cmp, hd), lambda b, g: (b, g, 0, 0))
    sspec = pl.BlockSpec((None, None, seq, 2 * hd), lambda b, g: (b, g, 0, 0))
    wspec = pl.BlockSpec((B_CMP_LEN, hd, hd), lambda b, g: (0, 0, 0))
    c_sh = jax.ShapeDtypeStruct((bsz, B_KV, n_cmp, hd), F32)
    s_sh = jax.ShapeDtypeStruct((bsz, B_KV, seq, 2 * hd), BF16)
    kc, vc, ks, vs, kw, vw = pl.pallas_call(
        functools.partial(_nsa_prep_body, stride=B_CMP_STRIDE),
        grid=(bsz, B_KV),
        in_specs=[kvspec(0, 0), kvspec(0, 1), kvspec(1, 0), kvspec(1, 1), kvspec(2, 0), kvspec(2, 1),
                  pl.BlockSpec((2, B_CMP_LEN, hd), lambda b, g: (0, 0, 0)), wspec, wspec,
                  pl.BlockSpec((3, hd), lambda b, g: (0, 0))],
        out_specs=[cspec, cspec, sspec, sspec, sspec, sspec],
        out_shape=[c_sh, c_sh, s_sh, s_sh, s_sh, s_sh],
        compiler_params=_params("parallel", "parallel"), name="b_prep",
    )(u, u, u, u, u, u, cmp_pe, cmp_wk.reshape(B_CMP_LEN, hd, hd).astype(BF16),
      cmp_wv.reshape(B_CMP_LEN, hd, hd).astype(BF16), k_norm_w)

    cmp_start = np.arange(n_cmp) * B_CMP_STRIDE
    sel_start = np.arange(n_blk) * B_SEL_LEN
    ov = np.clip(np.minimum(cmp_start[:, None] + B_CMP_LEN, sel_start[None, :] + B_SEL_LEN)
                 - np.maximum(cmp_start[:, None], sel_start[None, :]), 0, None) / B_CMP_LEN
    ov[cmp_start + B_CMP_LEN > seq] = 0.0
    ov = np.pad(ov, ((0, 0), (NSA_SEL_OFF, hd - NSA_SEL_OFF - n_blk)))
    slopes_np = (2.0 ** (-8.0 * np.arange(1, heads + 1) / heads)).astype(np.float32)
    slopes = jnp.asarray(slopes_np).reshape(B_KV, rep)
    to_bf16 = lambda v: v.astype(ml_dtypes.bfloat16).astype(np.float32)
    s_hi = to_bf16(slopes_np)
    s_mid = to_bf16(slopes_np - s_hi)
    s_lo = to_bf16(slopes_np - s_hi - s_mid)
    pieces = jnp.asarray(np.stack([s_hi, s_mid, s_lo], axis=-1).reshape(B_KV, 3 * rep))

    rw = rep * hd
    cspec3 = pl.BlockSpec((None, None, n_cmp, hd), lambda b, g, i: (b, g, 0, 0))
    sspec3 = pl.BlockSpec((None, None, seq, 2 * hd), lambda b, g, i: (b, g, 0, 0))
    y = pl.pallas_call(
        functools.partial(_nsa_body, rep=rep, hd=hd, n_top=n_top, n_blk=n_blk, kb_sel=kb_sel),
        grid=(bsz, B_KV, seq // tq),
        in_specs=[pl.BlockSpec(memory_space=pltpu.SMEM), pl.BlockSpec(memory_space=pltpu.SMEM),
                  pl.BlockSpec((None, tq, rw), lambda b, g, i: (b, i, g)),
                  pl.BlockSpec((None, tq, rw), lambda b, g, i: (b, i, inner // rw + g)),
                  pl.BlockSpec((None, None, tq, 3 * rep), lambda b, g, i: (b, g, i, 0)),
                  pl.BlockSpec((1, hd), lambda b, g, i: (0, 0)),
                  cspec3, cspec3, sspec3, sspec3, sspec3, sspec3,
                  pl.BlockSpec((n_cmp, hd), lambda b, g, i: (0, 0))],
        out_specs=pl.BlockSpec((None, tq, rw), lambda b, g, i: (b, i, g)),
        out_shape=jax.ShapeDtypeStruct((bsz, seq, inner), BF16),
        scratch_shapes=[pltpu.VMEM((rep * tq, 2 * hd), BF16), pltpu.VMEM((rep * tq, 2 * hd), F32),
                        pltpu.VMEM((rep * tq, hd), F32), pltpu.VMEM((rep * tq, hd), F32),
                        pltpu.SMEM((seq // kb_sel,), jnp.int32)],
        compiler_params=_params("parallel", "parallel", "arbitrary"), name="b_attn",
    )(slopes, pieces, u, u, gts, q_norm_w.reshape(1, hd), kc, vc, ks, vs, kw, vw, jnp.asarray(ov, dtype=F32))
    return _matmul(y.reshape(t, inner), w_out, resid=x2d, name="b_out")


def kernel(x, norm_w, a_w_in, a_conv_w, a_conv_b, a_gate_b, a_out_norm_w, a_w_out, b_w_in, b_cmp_pe, b_cmp_wk, b_cmp_wv, b_q_norm_w, b_k_norm_w, b_w_out, c_w_in, c_conv_w, c_conv_b, c_w_a, c_b_a, c_w_x, c_b_x, c_lambda, c_w_out, d_w_in, d_q_norm_w, d_k_norm_w, d_w_out):
    bsz, seq, d = x.shape
    depth = norm_w.shape[0]
    x2d = x.reshape(bsz * seq, d)
    for layer in range(depth):
        kind, j = layer % 4, layer // 4
        hn = _rmsnorm(x2d, norm_w[layer])
        if kind == 0:
            x2d = _mlstm_layer(x2d, hn, bsz, seq, a_w_in[j], a_conv_w[j], a_conv_b[j], a_gate_b[j],
                               a_out_norm_w[j], a_w_out[j])
        elif kind == 1:
            x2d = _nsa_layer(x2d, hn, bsz, seq, b_w_in[j], b_cmp_pe[j], b_cmp_wk[j], b_cmp_wv[j], b_q_norm_w[j],
                             b_k_norm_w[j], b_w_out[j])
        elif kind == 2:
            x2d = _rglru_layer(x2d, hn, bsz, seq, c_w_in[j], c_conv_w[j], c_conv_b[j], c_w_a[j], c_b_a[j],
                               c_w_x[j], c_b_x[j], c_lambda[j], c_w_out[j])
        else:
            x2d = _dilated_layer(x2d, hn, bsz, seq, d_w_in[j], d_q_norm_w[j], d_k_norm_w[j], d_w_out[j])
    return x2d.reshape(bsz, seq, d)
```

```python
import functools
import math

import ml_dtypes
import numpy as np
import jax
import jax.numpy as jnp
from jax import lax
from jax.experimental import pallas as pl
from jax.experimental.pallas import tpu as pltpu

F32 = jnp.float32
BF16 = jnp.bfloat16

EPS = 1e-6
NEG = -1e30
FORCE = 1e4
CONV_W = 4

LANES = 128
SUBLANES = 8
V7X_VMEM_LIMIT = 56 * 1024 * 1024

A_CHUNK = 128
B_CMP_LEN = 32
B_CMP_STRIDE = 16
B_SEL_LEN = 64
B_N_SEL = 16
B_WIN = 512
B_KV = 4
NSA_POS_SPLIT = 128
NSA_SEL_OFF = 6
C_BLOCKS = 16
C_EXP = 8.0
D_PATTERNS = ((128, 1), (512, 4), (2048, 16))
D_QBLK = 128


def _params(*sem):
    return pltpu.CompilerParams(dimension_semantics=sem, vmem_limit_bytes=V7X_VMEM_LIMIT)


def _sigmoid(x):
    return 0.5 * jnp.tanh(0.5 * x) + 0.5


def _silu(x):
    return x * _sigmoid(x)


def _softplus(x):
    return jnp.maximum(x, 0.0) + jnp.log1p(jnp.exp(-jnp.abs(x)))


def _head_rms(x, w):
    return x * lax.rsqrt(jnp.mean(x * x, axis=-1, keepdims=True) + EPS) * w


def _dot(a, b):
    return jnp.dot(a.astype(BF16), b.astype(BF16), preferred_element_type=F32)


def _dot_nt(a, b):
    return lax.dot_general(a.astype(BF16), b.astype(BF16), (((1,), (1,)), ((), ())),
                           preferred_element_type=F32)


def _dot_tn(a, b):
    return lax.dot_general(a.astype(BF16), b.astype(BF16), (((0,), (0,)), ((), ())),
                           preferred_element_type=F32)


def _pick(n, cands):
    for c in cands:
        if n % c == 0:
            return c
    raise ValueError(f"no tile for {n}")


def _rmsnorm_body(x_ref, w_ref, o_ref):
    x = x_ref[...]
    o_ref[...] = _head_rms(x, w_ref[...]).astype(o_ref.dtype)


def _rmsnorm(x2d, w):
    t, d = x2d.shape
    tm = _pick(t, (256, 128, 8))
    return pl.pallas_call(
        _rmsnorm_body, grid=(t // tm,),
        in_specs=[pl.BlockSpec((tm, d), lambda i: (i, 0)), pl.BlockSpec((1, d), lambda i: (0, 0))],
        out_specs=pl.BlockSpec((tm, d), lambda i: (i, 0)),
        out_shape=jax.ShapeDtypeStruct((t, d), BF16),
        compiler_params=_params("parallel"), name="rmsnorm")(x2d, w.reshape(1, d))


def _mm_body(a_ref, w_ref, *rest, transposed):
    r_ref, o_ref = rest if len(rest) == 2 else (None, rest[0])
    w = w_ref[...].astype(a_ref.dtype)
    if transposed:
        acc = lax.dot_general(a_ref[...], w, (((1,), (1,)), ((), ())), preferred_element_type=F32)
    else:
        acc = jnp.dot(a_ref[...], w, preferred_element_type=F32)
    if r_ref is not None:
        acc = r_ref[...] + acc
    if len(o_ref.shape) == 2:
        o_ref[...] = acc
    else:
        gw = o_ref.shape[2]
        for c in range(o_ref.shape[0]):
            o_ref[c] = acc[:, c * gw:(c + 1) * gw]


def _matmul(a, w, n=None, col0=0, resid=None, transposed=False, group=None, name="matmul"):
    m, k = a.shape
    n = w.shape[1] if n is None else n
    tm = _pick(m, (1024, 512, 256, 128))
    w_bytes = jnp.dtype(w.dtype).itemsize
    tn = _pick(n, tuple(c for c in (1024, 768, 512, 384, 256, 128)
                        if 2 * (k * c * w_bytes + tm * k * 2 + (2 if resid is not None else 1) * tm * c * 4)
                        + k * c * 2 <= V7X_VMEM_LIMIT - (4 << 20)))
    assert col0 % tn == 0
    cb = col0 // tn
    if transposed:
        w = jnp.swapaxes(w, 0, 1)
        w_spec = pl.BlockSpec((tn, k), lambda i, j: (cb + j, 0))
    else:
        w_spec = pl.BlockSpec((k, tn), lambda i, j: (0, cb + j))
    in_specs = [pl.BlockSpec((tm, k), lambda i, j: (i, 0)), w_spec]
    args = [a, w]
    body = functools.partial(_mm_body, transposed=transposed)
    if resid is not None:
        in_specs.append(pl.BlockSpec((tm, tn), lambda i, j: (i, j)))
        args.append(resid)
    if group is None:
        out_spec = pl.BlockSpec((tm, tn), lambda i, j: (i, j))
        out_shape = jax.ShapeDtypeStruct((m, n), F32)
    else:
        assert tn % group == 0 and group % LANES == 0 and resid is None
        out_spec = pl.BlockSpec((tn // group, tm, group), lambda i, j: (j, i, 0))
        out_shape = jax.ShapeDtypeStruct((n // group, m, group), F32)
    return pl.pallas_call(
        body, grid=(m // tm, n // tn), in_specs=in_specs, out_specs=out_spec, out_shape=out_shape,
        compiler_params=_params("parallel", "parallel"), name=name)(*args)


def _mlstm_body(gb_ref, q_ref, k_ref, v_ref, og_ref, z_ref, gcol_ref, grow_ref, cwq_ref, cwk_ref,
                cbq_ref, cbk_ref, onw_ref, y_ref, c_scr, n_scr, m_scr, extq, extk, *, hb, dqk, dv, interleave):
    c = pl.program_id(2)
    L = q_ref.shape[0]

    @pl.when(c == 0)
    def _():
        c_scr[...] = jnp.zeros_like(c_scr)
        n_scr[...] = jnp.zeros_like(n_scr)
        m_scr[...] = jnp.zeros_like(m_scr)
        extq[L:L + 8, :] = jnp.zeros((8, extq.shape[1]), F32)
        extk[L:L + 8, :] = jnp.zeros((8, extk.shape[1]), F32)

    def conv_silu(ext, x_ref, cw_ref, cb_ref):
        ext[0:8, :] = ext[L:L + 8, :]
        ext[8:L + 8, :] = x_ref[...]
        acc = cb_ref[...] + cw_ref[0:1, :] * ext[pl.ds(8 - CONV_W + 1, L), :]
        for j in range(1, CONV_W):
            acc = acc + cw_ref[j:j + 1, :] * ext[pl.ds(8 - CONV_W + 1 + j, L), :]
        return _silu(acc)

    q_all = conv_silu(extq, q_ref, cwq_ref, cbq_ref) * (dqk ** -0.5)
    k_all = conv_silu(extk, k_ref, cwk_ref, cbk_ref)
    t_io = lax.broadcasted_iota(jnp.int32, (L, L), 0)
    s_io = lax.broadcasted_iota(jnp.int32, (L, L), 1)
    causal = s_io <= t_io

    def head_stages(hh):
        h = pl.program_id(1) * hb + hh
        qs, vs = slice(hh * dqk, (hh + 1) * dqk), slice(hh * dv, (hh + 1) * dv)
        q, k, v = q_all[:, qs], k_all[:, qs], v_ref[:, vs]
        bi = gb_ref[0, h]
        bf = gb_ref[1, h]
        gcol = gcol_ref[hh]
        grow = grow_ref[hh]
        li_col = gcol[:, 0:1] + bi
        li_row = grow[0:1, :] + bi
        lf_col = -_softplus(-(gcol[:, 1:2] + bf))
        lf_row = -_softplus(-(grow[1:2, :] + bf))

        b_col = jnp.sum(jnp.where(causal, lf_row, 0.0), axis=1, keepdims=True)
        b_row = jnp.sum(jnp.where(t_io <= s_io, lf_col, 0.0), axis=0, keepdims=True)
        b_last = jnp.sum(lf_row, axis=1, keepdims=True)
        m_st = m_scr[hh]

        dmat = jnp.where(causal, b_col - b_row + li_row, NEG)
        inter = b_col + m_st
        m_t = jnp.maximum(inter, jnp.max(dmat, axis=1, keepdims=True))
        w_intra = jnp.exp(dmat - m_t)
        w_inter = jnp.exp(inter - m_t)
        c_st = c_scr[hh]
        n_st = n_scr[hh]
        yield
        qk = _dot_nt(q, k)
        q_c = _dot(q, c_st)
        yield
        a = w_intra * qk
        num = _dot(a, v) + w_inter * q_c
        qn = jnp.sum(a, axis=1, keepdims=True) + w_inter * jnp.sum(q * n_st, axis=1, keepdims=True)
        yield
        hc = num / jnp.maximum(jnp.abs(qn), jnp.exp(-m_t))

        g_row = b_last - b_row + li_row
        g_col = b_last - b_col + li_col
        m_new = jnp.maximum(b_last + m_st, jnp.max(g_row, axis=1, keepdims=True))
        ws_col = jnp.exp(g_col - m_new)
        decay = jnp.exp(b_last + m_st - m_new)
        kw = k * ws_col
        c_scr[hh] = decay * c_st + _dot_tn(kw, v)
        n_scr[hh] = decay * n_st + jnp.sum(kw, axis=0, keepdims=True)
        m_scr[hh] = m_new
        yield
        hn = _head_rms(hc, onw_ref[:, vs])
        y_ref[:, vs] = (hn * _sigmoid(og_ref[:, vs]) * _silu(z_ref[:, vs])).astype(y_ref.dtype)
        yield

    for h0 in range(0, hb, interleave):
        for _ in zip(*[head_stages(hh) for hh in range(h0, min(h0 + interleave, hb))]):
            pass


def _mlstm_layer(x2d, hn, bsz, seq, w_in, conv_w, conv_b, gate_b, out_norm_w, w_out):
    t, d = x2d.shape
    heads = gate_b.shape[-1]
    qk = conv_w.shape[-1] // 2
    inner = out_norm_w.shape[-1]
    dqk, dv = qk // heads, inner // heads
    n_main = 2 * qk + 3 * inner
    L = A_CHUNK
    nc = seq // L
    hb = heads
    assert dqk % LANES == 0 and dv % LANES == 0 and 2 * heads <= LANES and heads % hb == 0

    u = _matmul(hn, w_in, n=n_main, transposed=True, name="a_in").reshape(bsz, seq, n_main)
    gts = _matmul(hn, w_in, n=LANES, col0=n_main, transposed=True, name="a_gates")
    gts = gts[:, :2 * heads].reshape(bsz, nc, L, 2, heads)
    g_col = jnp.transpose(gts, (0, 4, 1, 2, 3))
    g_row = jnp.transpose(gts, (0, 4, 1, 3, 2))

    wq, wv = hb * dqk, hb * dv
    assert qk % wq == 0 and 2 * qk % wv == 0 and inner % wv == 0
    kq0, kv0, ko0, kz0 = qk // wq, 2 * qk // wv, (2 * qk + inner) // wv, (2 * qk + 2 * inner) // wv
    blk = lambda w, off: pl.BlockSpec((None, L, w), lambda b, h, c: (b, c, off + h))
    par = lambda r, w, off: pl.BlockSpec((r, w), lambda b, h, c: (0, off + h))
    y = pl.pallas_call(
        functools.partial(_mlstm_body, hb=hb, dqk=dqk, dv=dv, interleave=4),
        grid=(bsz, heads // hb, nc),
        in_specs=[pl.BlockSpec(memory_space=pltpu.SMEM),
                  blk(wq, 0), blk(wq, kq0), blk(wv, kv0), blk(wv, ko0), blk(wv, kz0),
                  pl.BlockSpec((None, hb, None, L, 2), lambda b, h, c: (b, h, c, 0, 0)),
                  pl.BlockSpec((None, hb, None, 2, L), lambda b, h, c: (b, h, c, 0, 0)),
                  par(CONV_W, wq, 0), par(CONV_W, wq, kq0), par(1, wq, 0), par(1, wq, kq0),
                  par(1, wv, 0)],
        out_specs=pl.BlockSpec((None, L, wv), lambda b, h, c: (b, c, h)),
        out_shape=jax.ShapeDtypeStruct((bsz, seq, inner), BF16),
        scratch_shapes=[pltpu.VMEM((hb, dqk, dv), F32), pltpu.VMEM((hb, 1, dqk), F32), pltpu.VMEM((hb, 1, 1), F32),
                        pltpu.VMEM((L + 8, wq), F32), pltpu.VMEM((L + 8, wq), F32)],
        compiler_params=_params("parallel", "parallel", "arbitrary"), name="a_mlstm",
    )(gate_b, u, u, u, u, u, g_col, g_row, conv_w, conv_w, conv_b.reshape(1, -1), conv_b.reshape(1, -1),
      out_norm_w.reshape(1, -1))
    return _matmul(y.reshape(t, inner), w_out, resid=x2d, name="a_out")


def _rglru_body(x_ref, z_ref, cw_ref, cb_ref, wa_ref, wx_ref, ba_ref, bx_ref, lam_ref, y_ref,
                ext, xc_scr, a_scr, b_scr, h_scr, *, starts, win, chunk, scan_chunk):
    s = pl.program_id(1)
    ts, width = x_ref.shape

    @pl.when(s == 0)
    def _():
        ext[ts:ts + 8, :] = jnp.zeros((8, width), F32)
        h_scr[...] = jnp.zeros_like(h_scr)

    ext[0:8, :] = ext[ts:ts + 8, :]
    ext[8:ts + 8, :] = x_ref[...]
    for c0 in range(0, width, chunk):
        cs = slice(c0, c0 + chunk)
        xc = cb_ref[:, cs] + cw_ref[0:1, cs] * ext[pl.ds(8 - CONV_W + 1, ts), cs]
        for j in range(1, CONV_W):
            xc = xc + cw_ref[j:j + 1, cs] * ext[pl.ds(8 - CONV_W + 1 + j, ts), cs]
        xc_scr[:, cs] = xc

    a_scr[...] = jnp.zeros_like(a_scr)
    b_scr[...] = jnp.zeros_like(b_scr)
    for n, a0 in enumerate(starts):
        ws = slice(a0, a0 + win)
        xw = xc_scr[:, ws].astype(BF16)
        a_scr[:, ws] += jnp.dot(xw, wa_ref[n], preferred_element_type=F32)
        b_scr[:, ws] += jnp.dot(xw, wx_ref[n], preferred_element_type=F32)

    for c0 in range(0, width, chunk):
        cs = slice(c0, c0 + chunk)
        r = _sigmoid(a_scr[:, cs] + ba_ref[:, cs])
        ig = _sigmoid(b_scr[:, cs] + bx_ref[:, cs])
        log_a = (-C_EXP) * _softplus(-lam_ref[:, cs]) * r
        a = jnp.exp(log_a)
        a_scr[:, cs] = a
        b_scr[:, cs] = jnp.sqrt(-jnp.tanh(log_a) * (a * a + 1.0)) * (ig * xc_scr[:, cs])

    for c0 in range(0, width, scan_chunk):
        cs = slice(c0, c0 + scan_chunk)

        def row(t, h, cs=cs):
            h = a_scr[pl.ds(t, 1), cs] * h + b_scr[pl.ds(t, 1), cs]
            b_scr[pl.ds(t, 1), cs] = h
            return h

        h_scr[:, cs] = lax.fori_loop(0, ts, row, h_scr[:, cs], unroll=8)

    y_ref[...] = (b_scr[...] * _silu(z_ref[...])).astype(y_ref.dtype)


def _rglru_layer(x2d, hn, bsz, seq, w_in, conv_w, conv_b, w_a, b_a, w_x, b_x, lam, w_out):
    t, d = x2d.shape
    width = conv_w.shape[-1]
    nblk = w_a.shape[0]
    blk = width // nblk
    assert width % LANES == 0
    win = LANES
    while True:
        starts = tuple(min(n * blk // LANES * LANES, width - win) for n in range(nblk))
        if all(n * blk - a0 + blk <= win for n, a0 in enumerate(starts)):
            break
        win += LANES

    def window_weights(w):
        return jnp.stack([jnp.pad(w[n], ((n * blk - a0, win - blk - (n * blk - a0)),) * 2)
                          for n, a0 in enumerate(starts)]).astype(BF16)

    u = _matmul(hn, w_in, name="c_in").reshape(bsz, seq, 2 * width)
    ts = _pick(seq, (128, 64, 8))
    chunk = _pick(width, (768, 512, 384, 256, 128))
    scan_chunk = _pick(width, (1792, 1536, 1024, 896, 768, 512, 384, 256, 128))
    full = lambda r: pl.BlockSpec((r, width), lambda b, s: (0, 0))
    wspec = pl.BlockSpec((nblk, win, win), lambda b, s: (0, 0, 0), pipeline_mode=pl.Buffered(1))
    row = lambda v: v.reshape(1, width)
    y = pl.pallas_call(
        functools.partial(_rglru_body, starts=starts, win=win, chunk=chunk, scan_chunk=scan_chunk),
        grid=(bsz, seq // ts),
        in_specs=[pl.BlockSpec((None, ts, width), lambda b, s: (b, s, 0)),
                  pl.BlockSpec((None, ts, width), lambda b, s: (b, s, 1)),
                  full(CONV_W), full(1), wspec, wspec, full(1), full(1), full(1)],
        out_specs=pl.BlockSpec((None, ts, width), lambda b, s: (b, s, 0)),
        out_shape=jax.ShapeDtypeStruct((bsz, seq, width), BF16),
        scratch_shapes=[pltpu.VMEM((ts + 8, width), F32), pltpu.VMEM((ts, width), F32),
                        pltpu.VMEM((ts, width), F32), pltpu.VMEM((ts, width), F32), pltpu.VMEM((1, width), F32)],
        compiler_params=_params("parallel", "arbitrary"), name="c_rglru",
    )(u, u, conv_w, row(conv_b), window_weights(w_a), window_weights(w_x), row(b_a), row(b_x), row(lam))
    return _matmul(y.reshape(t, width), w_out, resid=x2d, name="c_out")


def _dilated_body(sl_ref, *refs, patterns, tq, mix_rows, group):
    n_pat = len(patterns)
    qkv = [refs[5 * g:5 * g + 5] for g in range(n_pat)]
    z_ref, qw_ref, kw_ref, y_ref, knbuf, vbuf, obuf, lbuf = refs[5 * n_pat:]
    first_tile = pl.program_id(1) == 0
    slope = sl_ref[pl.program_id(2)]
    tt, hd = z_ref.shape
    scale = hd ** -0.5
    qi = lax.broadcasted_iota(jnp.int32, (tq, 2 * tq), 0)
    kj = lax.broadcasted_iota(jnp.int32, (tq, 2 * tq), 1)
    steps = qi + tq - kj

    for g, (window, dil) in enumerate(patterns):
        q_ref, kp_ref, kc_ref, vp_ref, vc_ref = qkv[g]
        halo = tq * dil
        knbuf[0:halo, :] = _head_rms(kp_ref[...], kw_ref[...])
        knbuf[halo:halo + tt, :] = _head_rms(kc_ref[...], kw_ref[...])
        vbuf[0:halo, :] = vp_ref[...]
        vbuf[halo:halo + tt, :] = vc_ref[...]
        bias = jnp.where((steps >= 0) & (steps <= window // dil), (-slope) * (steps * dil).astype(F32), NEG)
        bias_first = jnp.where((kj >= tq) | jnp.logical_not(first_tile), bias, NEG)
        subtiles = [(r + halo * j, j == 0) for r in range(dil) for j in range(tt // halo)]
        for i0 in range(0, len(subtiles), group):
            grp = subtiles[i0:i0 + group]
            rows = [(lambda n, st=st: pl.ds(st, n, stride=dil) if dil > 1 else pl.ds(st, n)) for st, _ in grp]
            ss = [_dot_nt(_head_rms(q_ref[rw(tq), :], qw_ref[...]) * scale, knbuf[rw(2 * tq), :])
                  + (bias_first if first else bias) for rw, (_, first) in zip(rows, grp)]
            ms = [jnp.max(s, axis=1, keepdims=True) for s in ss]
            ps = [jnp.exp(s - m) for s, m in zip(ss, ms)]
            dens = [jnp.sum(p, axis=1, keepdims=True) for p in ps]
            for rw, p, den, m in zip(rows, ps, dens, ms):
                obuf[g, rw(tq), :] = _dot(p / den, vbuf[rw(2 * tq), :])
                lbuf[g, rw(tq), :] = jnp.broadcast_to(m + jnp.log(den), (tq, hd))

    for c in range(0, tt, mix_rows):
        cs = slice(c, c + mix_rows)
        ls = [lbuf[g, cs, :] for g in range(n_pat)]
        mx = functools.reduce(jnp.maximum, ls)
        es = [jnp.exp(l - mx) for l in ls]
        tot = functools.reduce(lambda a, b: a + b, es)
        o = sum((e / tot) * obuf[g, cs, :] for g, e in enumerate(es))
        y_ref[cs, :] = (o * _silu(z_ref[cs, :])).astype(y_ref.dtype)


def _dilated_layer(x2d, hn, bsz, seq, w_in, q_norm_w, k_norm_w, w_out):
    t, d = x2d.shape
    hd = q_norm_w.shape[-1]
    inner = w_out.shape[0]
    heads = inner // hd
    n_pat = len(D_PATTERNS)
    cols = 3 * n_pat * inner + inner
    tq = D_QBLK
    max_dil = max(dil for _, dil in D_PATTERNS)
    tt = tq * max_dil
    assert hd == LANES and seq % tt == 0
    assert all(window // dil <= tq and tt % (tq * dil) == 0 for window, dil in D_PATTERNS)
    slopes = jnp.asarray(2.0 ** (-8.0 * np.arange(1, heads + 1) / heads), dtype=F32)
    u = _matmul(hn, w_in, group=hd, name="d_in").reshape(cols // hd, bsz, seq, hd)

    in_specs = [pl.BlockSpec(memory_space=pltpu.SMEM)]
    args = [slopes]
    for g, (window, dil) in enumerate(D_PATTERNS):
        halo = tq * dil
        cur = lambda part, g=g: pl.BlockSpec((None, None, tt, hd),
                                             lambda b, s, h: ((3 * g + part) * heads + h, b, s, 0))
        prev = lambda part, g=g, halo=halo: pl.BlockSpec(
            (None, None, halo, hd),
            lambda b, s, h: ((3 * g + part) * heads + h, b, jnp.maximum(s * (tt // halo) - 1, 0), 0))
        in_specs += [cur(0), prev(1), cur(1), prev(2), cur(2)]
        args += [u] * 5
    wspec = pl.BlockSpec((1, hd), lambda b, s, h: (0, 0))
    in_specs += [pl.BlockSpec((None, None, tt, hd), lambda b, s, h: (3 * n_pat * heads + h, b, s, 0)), wspec, wspec]
    args += [u, q_norm_w.reshape(1, hd), k_norm_w.reshape(1, hd)]
    y = pl.pallas_call(
        functools.partial(_dilated_body, patterns=D_PATTERNS, tq=tq, mix_rows=_pick(tt, (256, 128)), group=4),
        grid=(bsz, seq // tt, heads), in_specs=in_specs,
        out_specs=pl.BlockSpec((None, tt, hd), lambda b, s, h: (b, s, h)),
        out_shape=jax.ShapeDtypeStruct((bsz, seq, inner), BF16),
        scratch_shapes=[pltpu.VMEM((2 * tt, hd), F32), pltpu.VMEM((2 * tt, hd), F32),
                        pltpu.VMEM((n_pat, tt, hd), F32), pltpu.VMEM((n_pat, tt, hd), F32)],
        compiler_params=_params("parallel", "parallel", "parallel"), name="d_attn",
    )(*args)
    return _matmul(y.reshape(t, inner), w_out, resid=x2d, name="d_out")


def _nsa_prep_body(k0_ref, v0_ref, k1_ref, v1_ref, k2_ref, v2_ref, pe_ref, wk_ref, wv_ref, knw_ref,
                   kc_ref, vc_ref, ks_ref, vs_ref, kw_ref, vw_ref, *, stride):
    n_c = kc_ref.shape[0]

    def compress(x_ref, w_ref, pe):
        p1 = jnp.zeros(kc_ref.shape, F32)
        p2 = jnp.zeros(kc_ref.shape, F32)
        for l in range(stride):
            x = x_ref[pl.ds(l, n_c, stride=stride), :]
            p1 = p1 + _dot(x + pe[l:l + 1, :], w_ref[l])
            p2 = p2 + _dot(x + pe[stride + l:stride + l + 1, :], w_ref[stride + l])
        return p1 + pltpu.roll(p2, n_c - 1, 0)

    kc_ref[...] = _head_rms(compress(k0_ref, wk_ref, pe_ref[0]), knw_ref[0:1, :])
    vc_ref[...] = compress(v0_ref, wv_ref, pe_ref[1])

    seq, hd = k1_ref.shape
    pos = lax.broadcasted_iota(jnp.int32, (seq, hd), 0)
    lane = lax.broadcasted_iota(jnp.int32, (seq, hd), 1)
    split_shift = NSA_POS_SPLIT.bit_length() - 1
    pos_hi = ((pos >> split_shift) << split_shift).astype(F32)
    pos_lo = (pos & (NSA_POS_SPLIT - 1)).astype(F32)
    pos_cols = jnp.where(lane < 3, pos_hi, jnp.where(lane < NSA_SEL_OFF, pos_lo, 0.0))
    sel_blk = pos >> (B_SEL_LEN.bit_length() - 1)
    onehot = jnp.where(lane - NSA_SEL_OFF == sel_blk, NEG, 0.0)
    ones = jnp.ones((seq, hd), vs_ref.dtype)
    ks_ref[:, 0:hd] = _head_rms(k1_ref[...], knw_ref[1:2, :]).astype(ks_ref.dtype)
    ks_ref[:, hd:2 * hd] = (pos_cols + onehot).astype(ks_ref.dtype)
    vs_ref[:, 0:hd] = v1_ref[...].astype(vs_ref.dtype)
    vs_ref[:, hd:2 * hd] = ones
    kw_ref[:, 0:hd] = _head_rms(k2_ref[...], knw_ref[2:3, :]).astype(kw_ref.dtype)
    kw_ref[:, hd:2 * hd] = pos_cols.astype(kw_ref.dtype)
    vw_ref[:, 0:hd] = v2_ref[...].astype(vw_ref.dtype)
    vw_ref[:, hd:2 * hd] = ones


def _nsa_body(sl_ref, sp_ref, q_ref, z_ref, g_ref, qw_ref, kc_ref, vc_ref, ks_ref, vs_ref, kw_ref, vw_ref, ov_ref,
              y_ref, q2, acc, m_scr, o_scr, used, *, rep, hd, n_top, n_blk, kb_sel):
    g = pl.program_id(1)
    i = pl.program_id(2)
    tq = q_ref.shape[0]
    n_cmp = kc_ref.shape[0]
    t0 = i * tq
    scale = hd ** -0.5
    rows = lambda r: slice(r * tq, (r + 1) * tq)

    for r in range(rep):
        q2[rows(r), 0:hd] = (_head_rms(q_ref[:, r * hd:(r + 1) * hd], qw_ref[...]) * scale).astype(q2.dtype)
    gts = _sigmoid(g_ref[...])

    def gate(r, branch):
        return gts[:, 3 * r + branch:3 * r + branch + 1]

    t_col = t0 + lax.broadcasted_iota(jnp.int32, (tq, 1), 0)

    cmp_end = lax.broadcasted_iota(jnp.int32, (tq, n_cmp), 1) * B_CMP_STRIDE + (B_CMP_LEN - 1)
    dist_ci = t_col - cmp_end
    mask_c = dist_ci >= 0
    dist_c = dist_ci.astype(F32)
    s_all = _dot_nt(q2[:, 0:hd], kc_ref[...])
    ss = [jnp.where(mask_c, s_all[rows(r), :] - sl_ref[g, r] * dist_c, NEG) for r in range(rep)]
    ms = [jnp.max(s, axis=1, keepdims=True) for s in ss]
    es = [jnp.where(mask_c, jnp.exp(s - m), 0.0) for s, m in zip(ss, ms)]
    dens = [jnp.sum(e, axis=1, keepdims=True) for e in es]
    ps = [e / jnp.where(den > 0, den, 1.0) for e, den in zip(es, dens)]
    psum = functools.reduce(lambda a, b: a + b, ps)
    o_c = _dot(jnp.concatenate([p.astype(BF16) for p in ps], axis=0), vc_ref[...])
    for r in range(rep):
        o_scr[rows(r), :] = gate(r, 0) * o_c[rows(r), :]

    imp = jnp.dot(psum, ov_ref[...], preferred_element_type=F32, precision=lax.Precision.HIGHEST)
    lane = lax.broadcasted_iota(jnp.int32, (tq, hd), 1)
    blk = lane - NSA_SEL_OFF
    in_range = (blk >= 0) & (blk < n_blk)
    cur = t_col >> (B_SEL_LEN.bit_length() - 1)
    forced = (blk == 0) | (blk == cur) | (blk == cur - 1)
    imp = jnp.where(forced, imp + FORCE, imp)
    imp = jnp.where(in_range & (blk * B_SEL_LEN <= t_col), imp, NEG)
    imp_t = imp.T
    n_grp = -(-(NSA_SEL_OFF + n_blk) // SUBLANES)
    grp = [imp_t[SUBLANES * k:SUBLANES * (k + 1), :] for k in range(n_grp)]
    cnt = [jnp.zeros((SUBLANES, tq), F32) for _ in range(n_grp)]
    sub = lax.broadcasted_iota(jnp.int32, (SUBLANES, tq), 0)
    for mb in range(n_blk):
        slot = NSA_SEL_OFF + mb
        row = imp_t[slot:slot + 1, :]
        for k in range(n_grp):
            if SUBLANES * k > slot:
                beats = row >= grp[k]
            elif SUBLANES * (k + 1) <= slot:
                beats = row > grp[k]
            else:
                beats = (row > grp[k]) | ((row == grp[k]) & (sub > slot - SUBLANES * k))
            cnt[k] = cnt[k] + jnp.where(beats, 1.0, 0.0)
    cnt_t = jnp.concatenate(cnt + [jnp.zeros((hd - SUBLANES * n_grp, tq), F32)], axis=0)
    not_sel = jnp.where((cnt_t < n_top) & (imp_t > NEG / 2), 0.0, 1.0).T

    sel_bf = jnp.where(in_range, 1.0 - not_sel, 0.0).astype(BF16)
    per_slot = jnp.dot(jnp.ones((SUBLANES, tq), BF16), sel_bf, preferred_element_type=F32)
    slot_io = lax.broadcasted_iota(jnp.int32, (hd, hd), 0) - NSA_SEL_OFF
    kb_io = lax.broadcasted_iota(jnp.int32, (hd, hd), 1)
    kb_shift = (kb_sel // B_SEL_LEN).bit_length() - 1
    group = jnp.where((slot_io >= 0) & ((slot_io >> kb_shift) == kb_io), 1.0, 0.0).astype(BF16)
    per_kb = jnp.dot(per_slot.astype(BF16), group, preferred_element_type=F32)
    lane_s = lax.broadcasted_iota(jnp.int32, (SUBLANES, hd), 1)
    for kb in range(used.shape[0]):
        used[kb] = jnp.max(jnp.where(lane_s == kb, per_kb, 0.0)).astype(jnp.int32)

    for r in range(rep):
        piece = jnp.where((lane == 0) | (lane == 3), sp_ref[g, 3 * r],
                          jnp.where((lane == 1) | (lane == 4), sp_ref[g, 3 * r + 1],
                                    jnp.where((lane == 2) | (lane == 5), sp_ref[g, 3 * r + 2], 0.0)))
        q2[rows(r), hd:2 * hd] = jnp.where(in_range, not_sel, piece).astype(q2.dtype)

    acc[...] = jnp.zeros_like(acc)
    m_scr[...] = jnp.full(m_scr.shape, NEG, F32)
    n_chunk = kb_sel // hd

    def sel_block(start, bias):
        kblk = ks_ref[pl.ds(start, kb_sel), :]
        vblk = vs_ref[pl.ds(start, kb_sel), :]
        s_all = lax.dot_general(q2[...], kblk, (((1,), (1,)), ((), ())), preferred_element_type=F32)
        ss = [s_all[rows(r), :] if bias is None else s_all[rows(r), :] + bias for r in range(rep)]
        mxs = [functools.reduce(jnp.maximum, [s[:, c * hd:(c + 1) * hd] for c in range(n_chunk)]) for s in ss]
        m_olds = [m_scr[rows(r), :] for r in range(rep)]
        m_news = [jnp.maximum(m_old, jnp.max(mx, axis=1, keepdims=True))
                  for m_old, mx in zip(m_olds, mxs)]
        alphas = [jnp.exp(m_old - m_new) for m_old, m_new in zip(m_olds, m_news)]
        ps = [jnp.exp(s - jnp.concatenate([m_new] * n_chunk, axis=1)).astype(BF16) for s, m_new in zip(ss, m_news)]
        for r in range(rep):
            acc[rows(r), :] = (jnp.concatenate([alphas[r], alphas[r]], axis=1) * acc[rows(r), :]
                               + jnp.dot(ps[r], vblk, preferred_element_type=F32))
            m_scr[rows(r), :] = m_news[r]

    def sel_step(kb, carry):
        @pl.when(used[kb] > 0)
        def _():
            sel_block(pl.multiple_of(kb * kb_sel, kb_sel), None)

        return carry

    n_full = lax.div(t0, kb_sel)
    lax.fori_loop(0, n_full, sel_step, 0)
    d_start = pl.multiple_of(n_full * kb_sel, kb_sel)
    d_pos = d_start + lax.broadcasted_iota(jnp.int32, (tq, kb_sel), 1)
    sel_block(d_start, jnp.where(d_pos <= t_col, 0.0, NEG))
    for r in range(rep):
        o_scr[rows(r), :] += gate(r, 1) * (acc[rows(r), 0:hd] / acc[rows(r), hd:2 * hd])

    n_w = B_WIN + tq
    w_start = pl.multiple_of(jnp.maximum(t0 - B_WIN, 0), tq)
    w_dist = t_col - (w_start + lax.broadcasted_iota(jnp.int32, (tq, n_w), 1))
    w_bias = jnp.where((w_dist >= 0) & (w_dist < B_WIN), 0.0, NEG)
    kblk = kw_ref[pl.ds(w_start, n_w), :]
    vblk = vw_ref[pl.ds(w_start, n_w), :]
    s_all = lax.dot_general(q2[...], kblk, (((1,), (1,)), ((), ())), preferred_element_type=F32)
    half = rep // 2
    for r0 in (0, half):
        ss = [s_all[rows(r), :] + w_bias for r in range(r0, r0 + half)]
        ms = [jnp.max(s, axis=1, keepdims=True) for s in ss]
        ps = [jnp.exp(s - m).astype(BF16) for s, m in zip(ss, ms)]
        for r, p in zip(range(r0, r0 + half), ps):
            res = jnp.dot(p, vblk, preferred_element_type=F32)
            o_scr[rows(r), :] += gate(r, 2) * (res[:, 0:hd] / res[:, hd:2 * hd])

    for r in range(rep):
        hs = slice(r * hd, (r + 1) * hd)
        y_ref[:, hs] = (o_scr[rows(r), :] * _silu(z_ref[:, hs])).astype(y_ref.dtype)


def _nsa_layer(x2d, hn, bsz, seq, w_in, cmp_pe, cmp_wk, cmp_wv, q_norm_w, k_norm_w, w_out):
    t, d = x2d.shape
    hd = q_norm_w.shape[-1]
    inner = w_out.shape[0]
    heads = inner // hd
    rep = heads // B_KV
    kvw = B_KV * hd
    n_main = 2 * inner + 6 * kvw
    assert B_CMP_LEN == 2 * B_CMP_STRIDE and seq % B_CMP_STRIDE == 0 and 3 * heads <= LANES
    n_cmp = seq // B_CMP_STRIDE
    n_blk = seq // B_SEL_LEN
    n_top = min(B_N_SEL, n_blk)
    tq = 128
    kb_sel = 512
    assert hd == LANES and NSA_SEL_OFF + n_blk <= hd and NSA_POS_SPLIT == hd
    assert kb_sel % tq == 0 and seq % kb_sel == 0 and B_WIN % tq == 0 and seq >= B_WIN + tq

    u = _matmul(hn, w_in, n=n_main, transposed=True, name="b_in").reshape(bsz, seq, n_main)
    gts = _matmul(hn, w_in, n=LANES, col0=n_main, transposed=True, name="b_gates")
    gts = gts[:, :3 * heads].reshape(bsz, seq, B_KV, 3 * rep)
    gts = jnp.transpose(gts, (0, 2, 1, 3))

    kv0 = 2 * inner // hd
    kvspec = lambda br, kv: pl.BlockSpec((None, seq, hd), lambda b, g: (b, 0, kv0 + (br * 2 + kv) * B_KV + g))
    cspec = pl.BlockSpec((None, None, n_cmp, hd), lambda b, g: (b, g, 0, 0))
    sspec = pl.BlockSpec((None, None, seq, 2 * hd), lambda b, g: (b, g, 0, 0))
    wspec = pl.BlockSpec((B_CMP_LEN, hd, hd), lambda b, g: (0, 0, 0))
    c_sh = jax.ShapeDtypeStruct((bsz, B_KV, n_cmp, hd), F32)
    s_sh = jax.ShapeDtypeStruct((bsz, B_KV, seq, 2 * hd), BF16)
    kc, vc, ks, vs, kw, vw = pl.pallas_call(
        functools.partial(_nsa_prep_body, stride=B_CMP_STRIDE),
        grid=(bsz, B_KV),
        in_specs=[kvspec(0, 0), kvspec(0, 1), kvspec(1, 0), kvspec(1, 1), kvspec(2, 0), kvspec(2, 1),
                  pl.BlockSpec((2, B_CMP_LEN, hd), lambda b, g: (0, 0, 0)), wspec, wspec,
                  pl.BlockSpec((3, hd), lambda b, g: (0, 0))],
        out_specs=[cspec, cspec, sspec, sspec, sspec, sspec],
        out_shape=[c_sh, c_sh, s_sh, s_sh, s_sh, s_sh],
        compiler_params=_params("parallel", "parallel"), name="b_prep",
    )(u, u, u, u, u, u, cmp_pe, cmp_wk.reshape(B_CMP_LEN, hd, hd).astype(BF16),
      cmp_wv.reshape(B_CMP_LEN, hd, hd).astype(BF16), k_norm_w)

    cmp_start = np.arange(n_cmp) * B_CMP_STRIDE
    sel_start = np.arange(n_blk) * B_SEL_LEN
    ov = np.clip(np.minimum(cmp_start[:, None] + B_CMP_LEN, sel_start[None, :] + B_SEL_LEN)
                 - np.maximum(cmp_start[:, None], sel_start[None, :]), 0, None) / B_CMP_LEN
    ov[cmp_start + B_CMP_LEN > seq] = 0.0
    ov = np.pad(ov, ((0, 0), (NSA_SEL_OFF, hd - NSA_SEL_OFF - n_blk)))
    slopes_np = (2.0 ** (-8.0 * np.arange(1, heads + 1) / heads)).astype(np.float32)
    slopes = jnp.asarray(slopes_np).reshape(B_KV, rep)
    to_bf16 = lambda v: v.astype(ml_dtypes.bfloat16).astype(np.float32)
    s_hi = to_bf16(slopes_np)
    s_mid = to_bf16(slopes_np - s_hi)
    s_lo = to_bf16(slopes_np - s_hi - s_mid)
    pieces = jnp.asarray(np.stack([s_hi, s_mid, s_lo], axis=-1).reshape(B_KV, 3 * rep))

    rw = rep * hd
    cspec3 = pl.BlockSpec((None, None, n_cmp, hd), lambda b, g, i: (b, g, 0, 0))
    sspec3 = pl.BlockSpec((None, None, seq, 2 * hd), lambda b, g, i: (b, g, 0, 0))
    y = pl.pallas_call(
        functools.partial(_nsa_body, rep=rep, hd=hd, n_top=n_top, n_blk=n_blk, kb_sel=kb_sel),
        grid=(bsz, B_KV, seq // tq),
        in_specs=[pl.BlockSpec(memory_space=pltpu.SMEM), pl.BlockSpec(memory_space=pltpu.SMEM),
                  pl.BlockSpec((None, tq, rw), lambda b, g, i: (b, i, g)),
                  pl.BlockSpec((None, tq, rw), lambda b, g, i: (b, i, inner // rw + g)),
                  pl.BlockSpec((None, None, tq, 3 * rep), lambda b, g, i: (b, g, i, 0)),
                  pl.BlockSpec((1, hd), lambda b, g, i: (0, 0)),
                  cspec3, cspec3, sspec3, sspec3, sspec3, sspec3,
                  pl.BlockSpec((n_cmp, hd), lambda b, g, i: (0, 0))],
        out_specs=pl.BlockSpec((None, tq, rw), lambda b, g, i: (b, i, g)),
        out_shape=jax.ShapeDtypeStruct((bsz, seq, inner), BF16),
        scratch_shapes=[pltpu.VMEM((rep * tq, 2 * hd), BF16), pltpu.VMEM((rep * tq, 2 * hd), F32),
                        pltpu.VMEM((rep * tq, hd), F32), pltpu.VMEM((rep * tq, hd), F32),
                        pltpu.SMEM((seq // kb_sel,), jnp.int32)],
        compiler_params=_params("parallel", "parallel", "arbitrary"), name="b_attn",
    )(slopes, pieces, u, u, gts, q_norm_w.reshape(1, hd), kc, vc, ks, vs, kw, vw, jnp.asarray(ov, dtype=F32))
    return _matmul(y.reshape(t, inner), w_out, resid=x2d, name="b_out")


def kernel(x, norm_w, a_w_in, a_conv_w, a_conv_b, a_gate_b, a_out_norm_w, a_w_out, b_w_in, b_cmp_pe, b_cmp_wk, b_cmp_wv, b_q_norm_w, b_k_norm_w, b_w_out, c_w_in, c_conv_w, c_conv_b, c_w_a, c_b_a, c_w_x, c_b_x, c_lambda, c_w_out, d_w_in, d_q_norm_w, d_k_norm_w, d_w_out):
    bsz, seq, d = x.shape
    depth = norm_w.shape[0]
    x2d = x.reshape(bsz * seq, d)
    for layer in range(depth):
        kind, j = layer % 4, layer // 4
        hn = _rmsnorm(x2d, norm_w[layer])
        if kind == 0:
            x2d = _mlstm_layer(x2d, hn, bsz, seq, a_w_in[j], a_conv_w[j], a_conv_b[j], a_gate_b[j],
                               a_out_norm_w[j], a_w_out[j])
        elif kind == 1:
            x2d = _nsa_layer(x2d, hn, bsz, seq, b_w_in[j], b_cmp_pe[j], b_cmp_wk[j], b_cmp_wv[j], b_q_norm_w[j],
                             b_k_norm_w[j], b_w_out[j])
        elif kind == 2:
            x2d = _rglru_layer(x2d, hn, bsz, seq, c_w_in[j], c_conv_w[j], c_conv_b[j], c_w_a[j], c_b_a[j],
                               c_w_x[j], c_b_x[j], c_lambda[j], c_w_out[j])
        else:
            x2d = _dilated_layer(x2d, hn, bsz, seq, d_w_in[j], d_q_norm_w[j], d_k_norm_w[j], d_w_out[j])
    return x2d.reshape(bsz, seq, d)
```

```python
import functools
import math

import ml_dtypes
import numpy as np
import jax
import jax.numpy as jnp
from jax import lax
from jax.experimental import pallas as pl
from jax.experimental.pallas import tpu as pltpu

F32 = jnp.float32
BF16 = jnp.bfloat16

EPS = 1e-6
NEG = -1e30
FORCE = 1e4
CONV_W = 4

LANES = 128
SUBLANES = 8
V7X_VMEM_LIMIT = 56 * 1024 * 1024

A_CHUNK = 128
B_CMP_LEN = 32
B_CMP_STRIDE = 16
B_SEL_LEN = 64
B_N_SEL = 16
B_WIN = 512
B_KV = 4
NSA_POS_SPLIT = 128
NSA_SEL_OFF = 6
C_BLOCKS = 16
C_EXP = 8.0
D_PATTERNS = ((128, 1), (512, 4), (2048, 16))
D_QBLK = 128


def _params(*sem):
    return pltpu.CompilerParams(dimension_semantics=sem, vmem_limit_bytes=V7X_VMEM_LIMIT)


def _sigmoid(x):
    return 0.5 * jnp.tanh(0.5 * x) + 0.5


def _silu(x):
    return x * _sigmoid(x)


def _softplus(x):
    return jnp.maximum(x, 0.0) + jnp.log1p(jnp.exp(-jnp.abs(x)))


def _head_rms(x, w):
    return x * lax.rsqrt(jnp.mean(x * x, axis=-1, keepdims=True) + EPS) * w


def _dot(a, b):
    return jnp.dot(a.astype(BF16), b.astype(BF16), preferred_element_type=F32)


def _dot_nt(a, b):
    return lax.dot_general(a.astype(BF16), b.astype(BF16), (((1,), (1,)), ((), ())),
                           preferred_element_type=F32)


def _dot_tn(a, b):
    return lax.dot_general(a.astype(BF16), b.astype(BF16), (((0,), (0,)), ((), ())),
                           preferred_element_type=F32)


def _pick(n, cands):
    for c in cands:
        if n % c == 0:
            return c
    raise ValueError(f"no tile for {n}")


def _rmsnorm_body(x_ref, w_ref, o_ref):
    x = x_ref[...]
    o_ref[...] = _head_rms(x, w_ref[...]).astype(o_ref.dtype)


def _rmsnorm(x2d, w):
    t, d = x2d.shape
    tm = _pick(t, (256, 128, 8))
    return pl.pallas_call(
        _rmsnorm_body, grid=(t // tm,),
        in_specs=[pl.BlockSpec((tm, d), lambda i: (i, 0)), pl.BlockSpec((1, d), lambda i: (0, 0))],
        out_specs=pl.BlockSpec((tm, d), lambda i: (i, 0)),
        out_shape=jax.ShapeDtypeStruct((t, d), BF16),
        compiler_params=_params("parallel"), name="rmsnorm")(x2d, w.reshape(1, d))


def _mm_body(*refs, transposed, has_resid, has_ssq, emit_norm, k_dim):
    refs = list(refs)
    a_ref, w_ref = refs.pop(0), refs.pop(0)
    r_ref = refs.pop(0) if has_resid else None
    ssq_ref = refs.pop(0) if has_ssq else None
    nw_ref = refs.pop(0) if emit_norm else None
    o_ref = refs.pop(0)
    w = w_ref[...].astype(a_ref.dtype)
    if transposed:
        acc = lax.dot_general(a_ref[...], w, (((1,), (1,)), ((), ())), preferred_element_type=F32)
    else:
        acc = jnp.dot(a_ref[...], w, preferred_element_type=F32)
    if has_ssq:
        acc = acc * lax.rsqrt(ssq_ref[...] * (1.0 / k_dim) + EPS)
    if has_resid:
        acc = r_ref[...] + acc
    if emit_norm:
        xs_ref, ssq_out_ref = refs
        xs_ref[...] = (acc * nw_ref[...]).astype(xs_ref.dtype)
        part = jnp.sum(acc * acc, axis=1, keepdims=True)

        @pl.when(pl.program_id(1) == 0)
        def _():
            ssq_out_ref[...] = part

        @pl.when(pl.program_id(1) > 0)
        def _():
            ssq_out_ref[...] += part
    if len(o_ref.shape) == 2:
        o_ref[...] = acc
    else:
        gw = o_ref.shape[2]
        for c in range(o_ref.shape[0]):
            o_ref[c] = acc[:, c * gw:(c + 1) * gw]


def _matmul(a, w, n=None, col0=0, resid=None, transposed=False, group=None, ssq=None, next_norm_w=None,
            name="matmul"):
    m, k = a.shape
    n = w.shape[1] if n is None else n
    tm = _pick(m, (1024, 512, 256, 128))
    w_bytes = jnp.dtype(w.dtype).itemsize
    out_bytes = (8 if resid is not None else 4) + (2 if next_norm_w is not None else 0)
    col_blocks = (1 if ssq is not None else 0) + (1 if next_norm_w is not None else 0)
    tn = _pick(n, tuple(c for c in (1024, 768, 512, 384, 256, 128)
                        if 2 * (k * c * w_bytes + tm * k * 2 + tm * c * out_bytes + col_blocks * tm * LANES * 4)
                        + k * c * 2 <= V7X_VMEM_LIMIT - (4 << 20)))
    assert col0 % tn == 0
    cb = col0 // tn
    if transposed:
        w = jnp.swapaxes(w, 0, 1)
        w_spec = pl.BlockSpec((tn, k), lambda i, j: (cb + j, 0))
    else:
        w_spec = pl.BlockSpec((k, tn), lambda i, j: (0, cb + j))
    in_specs = [pl.BlockSpec((tm, k), lambda i, j: (i, 0)), w_spec]
    args = [a, w]
    if resid is not None:
        in_specs.append(pl.BlockSpec((tm, tn), lambda i, j: (i, j)))
        args.append(resid)
    if ssq is not None:
        in_specs.append(pl.BlockSpec((tm, 1), lambda i, j: (i, 0)))
        args.append(ssq)
    if next_norm_w is not None:
        in_specs.append(pl.BlockSpec((1, tn), lambda i, j: (0, j)))
        args.append(next_norm_w.reshape(1, n))
    if group is None:
        out_specs = [pl.BlockSpec((tm, tn), lambda i, j: (i, j))]
        out_shape = [jax.ShapeDtypeStruct((m, n), F32)]
    else:
        assert tn % group == 0 and group % LANES == 0 and resid is None
        out_specs = [pl.BlockSpec((tn // group, tm, group), lambda i, j: (j, i, 0))]
        out_shape = [jax.ShapeDtypeStruct((n // group, m, group), F32)]
    if next_norm_w is not None:
        out_specs += [pl.BlockSpec((tm, tn), lambda i, j: (i, j)), pl.BlockSpec((tm, 1), lambda i, j: (i, 0))]
        out_shape += [jax.ShapeDtypeStruct((m, n), BF16), jax.ShapeDtypeStruct((m, 1), F32)]
    body = functools.partial(_mm_body, transposed=transposed, has_resid=resid is not None, has_ssq=ssq is not None,
                             emit_norm=next_norm_w is not None, k_dim=k)
    col_sem = "arbitrary" if next_norm_w is not None else "parallel"
    out = pl.pallas_call(
        body, grid=(m // tm, n // tn), in_specs=in_specs, out_specs=out_specs, out_shape=out_shape,
        compiler_params=_params("parallel", col_sem), name=name)(*args)
    return out if next_norm_w is not None else out[0]


def _in_proj(act, w, **kw):
    return _matmul(act[0], w, ssq=act[1], **kw)


def _out_proj(y, w_out, x2d, next_w, name):
    if next_w is None:
        return _matmul(y, w_out, resid=x2d, name=name), None
    x_new, xs, part = _matmul(y, w_out, resid=x2d, next_norm_w=next_w, name=name)
    return x_new, (xs, part)


def _mlstm_body(gb_ref, q_ref, k_ref, v_ref, og_ref, z_ref, gcol_ref, grow_ref, cwq_ref, cwk_ref,
                cbq_ref, cbk_ref, onw_ref, y_ref, c_scr, n_scr, m_scr, extq, extk, *, hb, dqk, dv, interleave):
    c = pl.program_id(2)
    L = q_ref.shape[0]

    @pl.when(c == 0)
    def _():
        c_scr[...] = jnp.zeros_like(c_scr)
        n_scr[...] = jnp.zeros_like(n_scr)
        m_scr[...] = jnp.zeros_like(m_scr)
        extq[L:L + 8, :] = jnp.zeros((8, extq.shape[1]), F32)
        extk[L:L + 8, :] = jnp.zeros((8, extk.shape[1]), F32)

    def conv_silu(ext, x_ref, cw_ref, cb_ref):
        ext[0:8, :] = ext[L:L + 8, :]
        ext[8:L + 8, :] = x_ref[...]
        acc = cb_ref[...] + cw_ref[0:1, :] * ext[pl.ds(8 - CONV_W + 1, L), :]
        for j in range(1, CONV_W):
            acc = acc + cw_ref[j:j + 1, :] * ext[pl.ds(8 - CONV_W + 1 + j, L), :]
        return _silu(acc)

    q_all = conv_silu(extq, q_ref, cwq_ref, cbq_ref) * (dqk ** -0.5)
    k_all = conv_silu(extk, k_ref, cwk_ref, cbk_ref)
    t_io = lax.broadcasted_iota(jnp.int32, (L, L), 0)
    s_io = lax.broadcasted_iota(jnp.int32, (L, L), 1)
    causal = s_io <= t_io

    def head_stages(hh):
        h = pl.program_id(1) * hb + hh
        qs, vs = slice(hh * dqk, (hh + 1) * dqk), slice(hh * dv, (hh + 1) * dv)
        q, k, v = q_all[:, qs], k_all[:, qs], v_ref[:, vs]
        bi = gb_ref[0, h]
        bf = gb_ref[1, h]
        gcol = gcol_ref[hh]
        grow = grow_ref[hh]
        li_col = gcol[:, 0:1] + bi
        li_row = grow[0:1, :] + bi
        lf_col = -_softplus(-(gcol[:, 1:2] + bf))
        lf_row = -_softplus(-(grow[1:2, :] + bf))

        b_col = jnp.sum(jnp.where(causal, lf_row, 0.0), axis=1, keepdims=True)
        b_row = jnp.sum(jnp.where(t_io <= s_io, lf_col, 0.0), axis=0, keepdims=True)
        b_last = jnp.sum(lf_row, axis=1, keepdims=True)
        m_st = m_scr[hh]

        dmat = jnp.where(causal, b_col - b_row + li_row, NEG)
        inter = b_col + m_st
        m_t = jnp.maximum(inter, jnp.max(dmat, axis=1, keepdims=True))
        w_intra = jnp.exp(dmat - m_t)
        w_inter = jnp.exp(inter - m_t)
        c_st = c_scr[hh]
        n_st = n_scr[hh]
        yield
        qk = _dot_nt(q, k)
        q_c = _dot(q, c_st)
        yield
        a = w_intra * qk
        num = _dot(a, v) + w_inter * q_c
        qn = jnp.sum(a, axis=1, keepdims=True) + w_inter * jnp.sum(q * n_st, axis=1, keepdims=True)
        yield
        hc = num / jnp.maximum(jnp.abs(qn), jnp.exp(-m_t))

        g_row = b_last - b_row + li_row
        g_col = b_last - b_col + li_col
        m_new = jnp.maximum(b_last + m_st, jnp.max(g_row, axis=1, keepdims=True))
        ws_col = jnp.exp(g_col - m_new)
        decay = jnp.exp(b_last + m_st - m_new)
        kw = k * ws_col
        c_scr[hh] = decay * c_st + _dot_tn(kw, v)
        n_scr[hh] = decay * n_st + jnp.sum(kw, axis=0, keepdims=True)
        m_scr[hh] = m_new
        yield
        hn = _head_rms(hc, onw_ref[:, vs])
        y_ref[:, vs] = (hn * _sigmoid(og_ref[:, vs]) * _silu(z_ref[:, vs])).astype(y_ref.dtype)
        yield

    for h0 in range(0, hb, interleave):
        for _ in zip(*[head_stages(hh) for hh in range(h0, min(h0 + interleave, hb))]):
            pass


def _mlstm_layer(x2d, hn, next_w, bsz, seq, w_in, conv_w, conv_b, gate_b, out_norm_w, w_out):
    t, d = x2d.shape
    heads = gate_b.shape[-1]
    qk = conv_w.shape[-1] // 2
    inner = out_norm_w.shape[-1]
    dqk, dv = qk // heads, inner // heads
    n_main = 2 * qk + 3 * inner
    L = A_CHUNK
    nc = seq // L
    hb = heads
    assert dqk % LANES == 0 and dv % LANES == 0 and 2 * heads <= LANES and heads % hb == 0

    u = _in_proj(hn, w_in, n=n_main, transposed=True, name="a_in").reshape(bsz, seq, n_main)
    gts = _in_proj(hn, w_in, n=LANES, col0=n_main, transposed=True, name="a_gates")
    gts = gts[:, :2 * heads].reshape(bsz, nc, L, 2, heads)
    g_col = jnp.transpose(gts, (0, 4, 1, 2, 3))
    g_row = jnp.transpose(gts, (0, 4, 1, 3, 2))

    wq, wv = hb * dqk, hb * dv
    assert qk % wq == 0 and 2 * qk % wv == 0 and inner % wv == 0
    kq0, kv0, ko0, kz0 = qk // wq, 2 * qk // wv, (2 * qk + inner) // wv, (2 * qk + 2 * inner) // wv
    blk = lambda w, off: pl.BlockSpec((None, L, w), lambda b, h, c: (b, c, off + h))
    par = lambda r, w, off: pl.BlockSpec((r, w), lambda b, h, c: (0, off + h))
    y = pl.pallas_call(
        functools.partial(_mlstm_body, hb=hb, dqk=dqk, dv=dv, interleave=4),
        grid=(bsz, heads // hb, nc),
        in_specs=[pl.BlockSpec(memory_space=pltpu.SMEM),
                  blk(wq, 0), blk(wq, kq0), blk(wv, kv0), blk(wv, ko0), blk(wv, kz0),
                  pl.BlockSpec((None, hb, None, L, 2), lambda b, h, c: (b, h, c, 0, 0)),
                  pl.BlockSpec((None, hb, None, 2, L), lambda b, h, c: (b, h, c, 0, 0)),
                  par(CONV_W, wq, 0), par(CONV_W, wq, kq0), par(1, wq, 0), par(1, wq, kq0),
                  par(1, wv, 0)],
        out_specs=pl.BlockSpec((None, L, wv), lambda b, h, c: (b, c, h)),
        out_shape=jax.ShapeDtypeStruct((bsz, seq, inner), BF16),
        scratch_shapes=[pltpu.VMEM((hb, dqk, dv), F32), pltpu.VMEM((hb, 1, dqk), F32), pltpu.VMEM((hb, 1, 1), F32),
                        pltpu.VMEM((L + 8, wq), F32), pltpu.VMEM((L + 8, wq), F32)],
        compiler_params=_params("parallel", "parallel", "arbitrary"), name="a_mlstm",
    )(gate_b, u, u, u, u, u, g_col, g_row, conv_w, conv_w, conv_b.reshape(1, -1), conv_b.reshape(1, -1),
      out_norm_w.reshape(1, -1))
    return _out_proj(y.reshape(t, inner), w_out, x2d, next_w, "a_out")


def _rglru_body(x_ref, z_ref, cw_ref, cb_ref, wa_ref, wx_ref, ba_ref, bx_ref, lam_ref, y_ref,
                ext, xc_scr, a_scr, b_scr, h_scr, *, starts, win, chunk, scan_chunk):
    s = pl.program_id(1)
    ts, width = x_ref.shape

    @pl.when(s == 0)
    def _():
        ext[ts:ts + 8, :] = jnp.zeros((8, width), F32)
        h_scr[...] = jnp.zeros_like(h_scr)

    ext[0:8, :] = ext[ts:ts + 8, :]
    ext[8:ts + 8, :] = x_ref[...]
    for c0 in range(0, width, chunk):
        cs = slice(c0, c0 + chunk)
        xc = cb_ref[:, cs] + cw_ref[0:1, cs] * ext[pl.ds(8 - CONV_W + 1, ts), cs]
        for j in range(1, CONV_W):
            xc = xc + cw_ref[j:j + 1, cs] * ext[pl.ds(8 - CONV_W + 1 + j, ts), cs]
        xc_scr[:, cs] = xc

    a_scr[...] = jnp.zeros_like(a_scr)
    b_scr[...] = jnp.zeros_like(b_scr)
    for n, a0 in enumerate(starts):
        ws = slice(a0, a0 + win)
        xw = xc_scr[:, ws].astype(BF16)
        a_scr[:, ws] += jnp.dot(xw, wa_ref[n], preferred_element_type=F32)
        b_scr[:, ws] += jnp.dot(xw, wx_ref[n], preferred_element_type=F32)

    for c0 in range(0, width, chunk):
        cs = slice(c0, c0 + chunk)
        r = _sigmoid(a_scr[:, cs] + ba_ref[:, cs])
        ig = _sigmoid(b_scr[:, cs] + bx_ref[:, cs])
        log_a = (-C_EXP) * _softplus(-lam_ref[:, cs]) * r
        a = jnp.exp(log_a)
        a_scr[:, cs] = a
        b_scr[:, cs] = jnp.sqrt(-jnp.tanh(log_a) * (a * a + 1.0)) * (ig * xc_scr[:, cs])

    for c0 in range(0, width, scan_chunk):
        cs = slice(c0, c0 + scan_chunk)

        def row(t, h, cs=cs):
            h = a_scr[pl.ds(t, 1), cs] * h + b_scr[pl.ds(t, 1), cs]
            b_scr[pl.ds(t, 1), cs] = h
            return h

        h_scr[:, cs] = lax.fori_loop(0, ts, row, h_scr[:, cs], unroll=8)

    y_ref[...] = (b_scr[...] * _silu(z_ref[...])).astype(y_ref.dtype)


def _rglru_layer(x2d, hn, next_w, bsz, seq, w_in, conv_w, conv_b, w_a, b_a, w_x, b_x, lam, w_out):
    t, d = x2d.shape
    width = conv_w.shape[-1]
    nblk = w_a.shape[0]
    blk = width // nblk
    assert width % LANES == 0
    win = LANES
    while True:
        starts = tuple(min(n * blk // LANES * LANES, width - win) for n in range(nblk))
        if all(n * blk - a0 + blk <= win for n, a0 in enumerate(starts)):
            break
        win += LANES

    def window_weights(w):
        return jnp.stack([jnp.pad(w[n], ((n * blk - a0, win - blk - (n * blk - a0)),) * 2)
                          for n, a0 in enumerate(starts)]).astype(BF16)

    u = _in_proj(hn, w_in, name="c_in").reshape(bsz, seq, 2 * width)
    ts = _pick(seq, (128, 64, 8))
    chunk = _pick(width, (768, 512, 384, 256, 128))
    scan_chunk = _pick(width, (1792, 1536, 1024, 896, 768, 512, 384, 256, 128))
    full = lambda r: pl.BlockSpec((r, width), lambda b, s: (0, 0))
    wspec = pl.BlockSpec((nblk, win, win), lambda b, s: (0, 0, 0), pipeline_mode=pl.Buffered(1))
    row = lambda v: v.reshape(1, width)
    y = pl.pallas_call(
        functools.partial(_rglru_body, starts=starts, win=win, chunk=chunk, scan_chunk=scan_chunk),
        grid=(bsz, seq // ts),
        in_specs=[pl.BlockSpec((None, ts, width), lambda b, s: (b, s, 0)),
                  pl.BlockSpec((None, ts, width), lambda b, s: (b, s, 1)),
                  full(CONV_W), full(1), wspec, wspec, full(1), full(1), full(1)],
        out_specs=pl.BlockSpec((None, ts, width), lambda b, s: (b, s, 0)),
        out_shape=jax.ShapeDtypeStruct((bsz, seq, width), BF16),
        scratch_shapes=[pltpu.VMEM((ts + 8, width), F32), pltpu.VMEM((ts, width), F32),
                        pltpu.VMEM((ts, width), F32), pltpu.VMEM((ts, width), F32), pltpu.VMEM((1, width), F32)],
        compiler_params=_params("parallel", "arbitrary"), name="c_rglru",
    )(u, u, conv_w, row(conv_b), window_weights(w_a), window_weights(w_x), row(b_a), row(b_x), row(lam))
    return _out_proj(y.reshape(t, width), w_out, x2d, next_w, "c_out")


def _dilated_body(sl_ref, *refs, patterns, tq, mix_rows, group):
    n_pat = len(patterns)
    qkv = [refs[5 * g:5 * g + 5] for g in range(n_pat)]
    z_ref, qw_ref, kw_ref, y_ref, knbuf, vbuf, obuf, lbuf = refs[5 * n_pat:]
    first_tile = pl.program_id(1) == 0
    slope = sl_ref[pl.program_id(2)]
    tt, hd = z_ref.shape
    scale = hd ** -0.5
    qi = lax.broadcasted_iota(jnp.int32, (tq, 2 * tq), 0)
    kj = lax.broadcasted_iota(jnp.int32, (tq, 2 * tq), 1)
    steps = qi + tq - kj

    for g, (window, dil) in enumerate(patterns):
        q_ref, kp_ref, kc_ref, vp_ref, vc_ref = qkv[g]
        halo = tq * dil
        knbuf[0:halo, :] = _head_rms(kp_ref[...], kw_ref[...])
        knbuf[halo:halo + tt, :] = _head_rms(kc_ref[...], kw_ref[...])
        vbuf[0:halo, :] = vp_ref[...]
        vbuf[halo:halo + tt, :] = vc_ref[...]
        bias = jnp.where((steps >= 0) & (steps <= window // dil), (-slope) * (steps * dil).astype(F32), NEG)
        bias_first = jnp.where((kj >= tq) | jnp.logical_not(first_tile), bias, NEG)
        subtiles = [(r + halo * j, j == 0) for r in range(dil) for j in range(tt // halo)]
        for i0 in range(0, len(subtiles), group):
            grp = subtiles[i0:i0 + group]
            rows = [(lambda n, st=st: pl.ds(st, n, stride=dil) if dil > 1 else pl.ds(st, n)) for st, _ in grp]
            ss = [_dot_nt(_head_rms(q_ref[rw(tq), :], qw_ref[...]) * scale, knbuf[rw(2 * tq), :])
                  + (bias_first if first else bias) for rw, (_, first) in zip(rows, grp)]
            ms = [jnp.max(s, axis=1, keepdims=True) for s in ss]
            ps = [jnp.exp(s - m) for s, m in zip(ss, ms)]
            dens = [jnp.sum(p, axis=1, keepdims=True) for p in ps]
            for rw, p, den, m in zip(rows, ps, dens, ms):
                obuf[g, rw(tq), :] = _dot(p / den, vbuf[rw(2 * tq), :])
                lbuf[g, rw(tq), :] = jnp.broadcast_to(m + jnp.log(den), (tq, hd))

    for c in range(0, tt, mix_rows):
        cs = slice(c, c + mix_rows)
        ls = [lbuf[g, cs, :] for g in range(n_pat)]
        mx = functools.reduce(jnp.maximum, ls)
        es = [jnp.exp(l - mx) for l in ls]
        tot = functools.reduce(lambda a, b: a + b, es)
        o = sum((e / tot) * obuf[g, cs, :] for g, e in enumerate(es))
        y_ref[cs, :] = (o * _silu(z_ref[cs, :])).astype(y_ref.dtype)


def _dilated_layer(x2d, hn, next_w, bsz, seq, w_in, q_norm_w, k_norm_w, w_out):
    t, d = x2d.shape
    hd = q_norm_w.shape[-1]
    inner = w_out.shape[0]
    heads = inner // hd
    n_pat = len(D_PATTERNS)
    cols = 3 * n_pat * inner + inner
    tq = D_QBLK
    max_dil = max(dil for _, dil in D_PATTERNS)
    tt = tq * max_dil
    assert hd == LANES and seq % tt == 0
    assert all(window // dil <= tq and tt % (tq * dil) == 0 for window, dil in D_PATTERNS)
    slopes = jnp.asarray(2.0 ** (-8.0 * np.arange(1, heads + 1) / heads), dtype=F32)
    u = _in_proj(hn, w_in, group=hd, name="d_in").reshape(cols // hd, bsz, seq, hd)

    in_specs = [pl.BlockSpec(memory_space=pltpu.SMEM)]
    args = [slopes]
    for g, (window, dil) in enumerate(D_PATTERNS):
        halo = tq * dil
        cur = lambda part, g=g: pl.BlockSpec((None, None, tt, hd),
                                             lambda b, s, h: ((3 * g + part) * heads + h, b, s, 0))
        prev = lambda part, g=g, halo=halo: pl.BlockSpec(
            (None, None, halo, hd),
            lambda b, s, h: ((3 * g + part) * heads + h, b, jnp.maximum(s * (tt // halo) - 1, 0), 0))
        in_specs += [cur(0), prev(1), cur(1), prev(2), cur(2)]
        args += [u] * 5
    wspec = pl.BlockSpec((1, hd), lambda b, s, h: (0, 0))
    in_specs += [pl.BlockSpec((None, None, tt, hd), lambda b, s, h: (3 * n_pat * heads + h, b, s, 0)), wspec, wspec]
    args += [u, q_norm_w.reshape(1, hd), k_norm_w.reshape(1, hd)]
    y = pl.pallas_call(
        functools.partial(_dilated_body, patterns=D_PATTERNS, tq=tq, mix_rows=_pick(tt, (256, 128)), group=4),
        grid=(bsz, seq // tt, heads), in_specs=in_specs,
        out_specs=pl.BlockSpec((None, tt, hd), lambda b, s, h: (b, s, h)),
        out_shape=jax.ShapeDtypeStruct((bsz, seq, inner), BF16),
        scratch_shapes=[pltpu.VMEM((2 * tt, hd), F32), pltpu.VMEM((2 * tt, hd), F32),
                        pltpu.VMEM((n_pat, tt, hd), F32), pltpu.VMEM((n_pat, tt, hd), F32)],
        compiler_params=_params("parallel", "parallel", "parallel"), name="d_attn",
    )(*args)
    return _out_proj(y.reshape(t, inner), w_out, x2d, next_w, "d_out")


def _nsa_prep_body(k0_ref, v0_ref, k1_ref, v1_ref, k2_ref, v2_ref, pe_ref, wk_ref, wv_ref, knw_ref,
                   kc_ref, vc_ref, ks_ref, vs_ref, kw_ref, vw_ref, *, stride):
    n_c = kc_ref.shape[0]

    def compress(x_ref, w_ref, pe):
        p1 = jnp.zeros(kc_ref.shape, F32)
        p2 = jnp.zeros(kc_ref.shape, F32)
        for l in range(stride):
            x = x_ref[pl.ds(l, n_c, stride=stride), :]
            p1 = p1 + _dot(x + pe[l:l + 1, :], w_ref[l])
            p2 = p2 + _dot(x + pe[stride + l:stride + l + 1, :], w_ref[stride + l])
        return p1 + pltpu.roll(p2, n_c - 1, 0)

    kc_ref[...] = _head_rms(compress(k0_ref, wk_ref, pe_ref[0]), knw_ref[0:1, :])
    vc_ref[...] = compress(v0_ref, wv_ref, pe_ref[1])

    seq, hd = k1_ref.shape
    pos = lax.broadcasted_iota(jnp.int32, (seq, hd), 0)
    lane = lax.broadcasted_iota(jnp.int32, (seq, hd), 1)
    split_shift = NSA_POS_SPLIT.bit_length() - 1
    pos_hi = ((pos >> split_shift) << split_shift).astype(F32)
    pos_lo = (pos & (NSA_POS_SPLIT - 1)).astype(F32)
    pos_cols = jnp.where(lane < 3, pos_hi, jnp.where(lane < NSA_SEL_OFF, pos_lo, 0.0))
    sel_blk = pos >> (B_SEL_LEN.bit_length() - 1)
    onehot = jnp.where(lane - NSA_SEL_OFF == sel_blk, NEG, 0.0)
    ones = jnp.ones((seq, hd), vs_ref.dtype)
    ks_ref[:, 0:hd] = _head_rms(k1_ref[...], knw_ref[1:2, :]).astype(ks_ref.dtype)
    ks_ref[:, hd:2 * hd] = (pos_cols + onehot).astype(ks_ref.dtype)
    vs_ref[:, 0:hd] = v1_ref[...].astype(vs_ref.dtype)
    vs_ref[:, hd:2 * hd] = ones
    kw_ref[:, 0:hd] = _head_rms(k2_ref[...], knw_ref[2:3, :]).astype(kw_ref.dtype)
    kw_ref[:, hd:2 * hd] = pos_cols.astype(kw_ref.dtype)
    vw_ref[:, 0:hd] = v2_ref[...].astype(vw_ref.dtype)
    vw_ref[:, hd:2 * hd] = ones


def _nsa_body(sl_ref, sp_ref, q_ref, z_ref, g_ref, qw_ref, kc_ref, vc_ref, ks_ref, vs_ref, kw_ref, vw_ref, ov_ref,
              y_ref, q2, acc, m_scr, o_scr, used, *, rep, hd, n_top, n_blk, kb_sel):
    g = pl.program_id(1)
    i = pl.program_id(2)
    tq = q_ref.shape[0]
    n_cmp = kc_ref.shape[0]
    t0 = i * tq
    scale = hd ** -0.5
    rows = lambda r: slice(r * tq, (r + 1) * tq)

    for r in range(rep):
        q2[rows(r), 0:hd] = (_head_rms(q_ref[:, r * hd:(r + 1) * hd], qw_ref[...]) * scale).astype(q2.dtype)
    gts = _sigmoid(g_ref[...])

    def gate(r, branch):
        return gts[:, 3 * r + branch:3 * r + branch + 1]

    t_col = t0 + lax.broadcasted_iota(jnp.int32, (tq, 1), 0)
    lane = lax.broadcasted_iota(jnp.int32, (tq, hd), 1)
    pieces = []
    for r in range(rep):
        pieces.append(jnp.where((lane == 0) | (lane == 3), sp_ref[g, 3 * r],
                                jnp.where((lane == 1) | (lane == 4), sp_ref[g, 3 * r + 1],
                                          jnp.where((lane == 2) | (lane == 5), sp_ref[g, 3 * r + 2], 0.0))))
        q2[rows(r), hd:2 * hd] = pieces[r].astype(q2.dtype)

    n_w = B_WIN + tq
    w_start = pl.multiple_of(jnp.maximum(t0 - B_WIN, 0), tq)
    w_dist = t_col - (w_start + lax.broadcasted_iota(jnp.int32, (tq, n_w), 1))
    w_bias = jnp.where((w_dist >= 0) & (w_dist < B_WIN), 0.0, NEG)
    kblk = kw_ref[pl.ds(w_start, n_w), :]
    vblk = vw_ref[pl.ds(w_start, n_w), :]
    s_all = lax.dot_general(q2[...], kblk, (((1,), (1,)), ((), ())), preferred_element_type=F32)
    half = rep // 2
    for r0 in (0, half):
        ss = [s_all[rows(r), :] + w_bias for r in range(r0, r0 + half)]
        ms = [jnp.max(s, axis=1, keepdims=True) for s in ss]
        ps = [jnp.exp(s - m).astype(BF16) for s, m in zip(ss, ms)]
        for r, p in zip(range(r0, r0 + half), ps):
            res = jnp.dot(p, vblk, preferred_element_type=F32)
            o_scr[rows(r), :] = gate(r, 2) * (res[:, 0:hd] / res[:, hd:2 * hd])

    cmp_end = lax.broadcasted_iota(jnp.int32, (tq, n_cmp), 1) * B_CMP_STRIDE + (B_CMP_LEN - 1)
    dist_ci = t_col - cmp_end
    mask_c = dist_ci >= 0
    dist_c = dist_ci.astype(F32)
    s_all = _dot_nt(q2[:, 0:hd], kc_ref[...])
    ss = [jnp.where(mask_c, s_all[rows(r), :] - sl_ref[g, r] * dist_c, NEG) for r in range(rep)]
    ms = [jnp.max(s, axis=1, keepdims=True) for s in ss]
    es = [jnp.where(mask_c, jnp.exp(s - m), 0.0) for s, m in zip(ss, ms)]
    dens = [jnp.sum(e, axis=1, keepdims=True) for e in es]
    ps = [e / jnp.where(den > 0, den, 1.0) for e, den in zip(es, dens)]
    psum = functools.reduce(lambda a, b: a + b, ps)
    o_c = _dot(jnp.concatenate([p.astype(BF16) for p in ps], axis=0), vc_ref[...])
    for r in range(rep):
        o_scr[rows(r), :] += gate(r, 0) * o_c[rows(r), :]

    imp = jnp.dot(psum, ov_ref[...], preferred_element_type=F32, precision=lax.Precision.HIGHEST)
    blk = lane - NSA_SEL_OFF
    in_range = (blk >= 0) & (blk < n_blk)
    cur = t_col >> (B_SEL_LEN.bit_length() - 1)
    forced = (blk == 0) | (blk == cur) | (blk == cur - 1)
    imp = jnp.where(forced, imp + FORCE, imp)
    imp = jnp.where(in_range & (blk * B_SEL_LEN <= t_col), imp, NEG)
    imp_t = imp.T
    n_grp = -(-(NSA_SEL_OFF + n_blk) // SUBLANES)
    grp = [imp_t[SUBLANES * k:SUBLANES * (k + 1), :] for k in range(n_grp)]
    cnt = [jnp.zeros((SUBLANES, tq), F32) for _ in range(n_grp)]
    sub = lax.broadcasted_iota(jnp.int32, (SUBLANES, tq), 0)
    for mb in range(n_blk):
        slot = NSA_SEL_OFF + mb
        row = imp_t[slot:slot + 1, :]
        for k in range(n_grp):
            if SUBLANES * k > slot:
                beats = row >= grp[k]
            elif SUBLANES * (k + 1) <= slot:
                beats = row > grp[k]
            else:
                beats = (row > grp[k]) | ((row == grp[k]) & (sub > slot - SUBLANES * k))
            cnt[k] = cnt[k] + jnp.where(beats, 1.0, 0.0)
    cnt_t = jnp.concatenate(cnt + [jnp.zeros((hd - SUBLANES * n_grp, tq), F32)], axis=0)
    not_sel = jnp.where((cnt_t < n_top) & (imp_t > NEG / 2), 0.0, 1.0).T

    sel_bf = jnp.where(in_range, 1.0 - not_sel, 0.0).astype(BF16)
    per_slot = jnp.dot(jnp.ones((SUBLANES, tq), BF16), sel_bf, preferred_element_type=F32)
    slot_io = lax.broadcasted_iota(jnp.int32, (hd, hd), 0) - NSA_SEL_OFF
    kb_io = lax.broadcasted_iota(jnp.int32, (hd, hd), 1)
    kb_shift = (kb_sel // B_SEL_LEN).bit_length() - 1
    group = jnp.where((slot_io >= 0) & ((slot_io >> kb_shift) == kb_io), 1.0, 0.0).astype(BF16)
    per_kb = jnp.dot(per_slot.astype(BF16), group, preferred_element_type=F32)
    lane_s = lax.broadcasted_iota(jnp.int32, (SUBLANES, hd), 1)
    for kb in range(used.shape[0]):
        used[kb] = jnp.max(jnp.where(lane_s == kb, per_kb, 0.0)).astype(jnp.int32)

    for r in range(rep):
        q2[rows(r), hd:2 * hd] = jnp.where(in_range, not_sel, pieces[r]).astype(q2.dtype)

    acc[...] = jnp.zeros_like(acc)
    m_scr[...] = jnp.full(m_scr.shape, NEG, F32)
    n_chunk = kb_sel // hd

    def sel_block(start, bias):
        kblk = ks_ref[pl.ds(start, kb_sel), :]
        vblk = vs_ref[pl.ds(start, kb_sel), :]
        s_all = lax.dot_general(q2[...], kblk, (((1,), (1,)), ((), ())), preferred_element_type=F32)
        ss = [s_all[rows(r), :] if bias is None else s_all[rows(r), :] + bias for r in range(rep)]
        mxs = [functools.reduce(jnp.maximum, [s[:, c * hd:(c + 1) * hd] for c in range(n_chunk)]) for s in ss]
        m_olds = [m_scr[rows(r), :] for r in range(rep)]
        m_news = [jnp.maximum(m_old, jnp.max(mx, axis=1, keepdims=True))
                  for m_old, mx in zip(m_olds, mxs)]
        alphas = [jnp.exp(m_old - m_new) for m_old, m_new in zip(m_olds, m_news)]
        ps = [jnp.exp(s - jnp.concatenate([m_new] * n_chunk, axis=1)).astype(BF16) for s, m_new in zip(ss, m_news)]
        for r in range(rep):
            acc[rows(r), :] = (jnp.concatenate([alphas[r], alphas[r]], axis=1) * acc[rows(r), :]
                               + jnp.dot(ps[r], vblk, preferred_element_type=F32))
            m_scr[rows(r), :] = m_news[r]

    def sel_step(kb, carry):
        @pl.when(used[kb] > 0)
        def _():
            sel_block(pl.multiple_of(kb * kb_sel, kb_sel), None)

        return carry

    n_full = lax.div(t0, kb_sel)
    lax.fori_loop(0, n_full, sel_step, 0)
    d_start = pl.multiple_of(n_full * kb_sel, kb_sel)
    d_pos = d_start + lax.broadcasted_iota(jnp.int32, (tq, kb_sel), 1)
    sel_block(d_start, jnp.where(d_pos <= t_col, 0.0, NEG))
    for r in range(rep):
        o_scr[rows(r), :] += gate(r, 1) * (acc[rows(r), 0:hd] / acc[rows(r), hd:2 * hd])

    for r in range(rep):
        hs = slice(r * hd, (r + 1) * hd)
        y_ref[:, hs] = (o_scr[rows(r), :] * _silu(z_ref[:, hs])).astype(y_ref.dtype)


def _nsa_layer(x2d, hn, next_w, bsz, seq, w_in, cmp_pe, cmp_wk, cmp_wv, q_norm_w, k_norm_w, w_out):
    t, d = x2d.shape
    hd = q_norm_w.shape[-1]
    inner = w_out.shape[0]
    heads = inner // hd
    rep = heads // B_KV
    kvw = B_KV * hd
    n_main = 2 * inner + 6 * kvw
    assert B_CMP_LEN == 2 * B_CMP_STRIDE and seq % B_CMP_STRIDE == 0 and 3 * heads <= LANES
    n_cmp = seq // B_CMP_STRIDE
    n_blk = seq // B_SEL_LEN
    n_top = min(B_N_SEL, n_blk)
    tq = 128
    kb_sel = 512
    assert hd == LANES and NSA_SEL_OFF + n_blk <= hd and NSA_POS_SPLIT == hd
    assert kb_sel % tq == 0 and seq % kb_sel == 0 and B_WIN % tq == 0 and seq >= B_WIN + tq

    u = _in_proj(hn, w_in, n=n_main, transposed=True, name="b_in").reshape(bsz, seq, n_main)
    gts = _in_proj(hn, w_in, n=LANES, col0=n_main, transposed=True, name="b_gates")
    gts = gts[:, :3 * heads].reshape(bsz, seq, B_KV, 3 * rep)
    gts = jnp.transpose(gts, (0, 2, 1, 3))

    kv0 = 2 * inner // hd
    kvspec = lambda br, kv: pl.BlockSpec((None, seq, hd), lambda b, g: (b, 0, kv0 + (br * 2 + kv) * B_KV + g))
    cspec = pl.BlockSpec((None, None, n_cmp, hd), lambda b, g: (b, g, 0, 0))
    sspec = pl.BlockSpec((None, None, seq, 2 * hd), lambda b, g: (b, g, 0, 0))
    wspec = pl.BlockSpec((B_CMP_LEN, hd, hd), lambda b, g: (0, 0, 0))
    c_sh = jax.ShapeDtypeStruct((bsz, B_KV, n_cmp, hd), F32)
    s_sh = jax.ShapeDtypeStruct((bsz, B_KV, seq, 2 * hd), BF16)
    kc, vc, ks, vs, kw, vw = pl.pallas_call(
        functools.partial(_nsa_prep_body, stride=B_CMP_STRIDE),
        grid=(bsz, B_KV),
        in_specs=[kvspec(0, 0), kvspec(0, 1), kvspec(1, 0), kvspec(1, 1), kvspec(2, 0), kvspec(2, 1),
                  pl.BlockSpec((2, B_CMP_LEN, hd), lambda b, g: (0, 0, 0)), wspec, wspec,
                  pl.BlockSpec((3, hd), lambda b, g: (0, 0))],
        out_specs=[cspec, cspec, sspec, sspec, sspec, sspec],
        out_shape=[c_sh, c_sh, s_sh, s_sh, s_sh, s_sh],
        compiler_params=_params("parallel", "parallel"), name="b_prep",
    )(u, u, u, u, u, u, cmp_pe, cmp_wk.reshape(B_CMP_LEN, hd, hd).astype(BF16),
      cmp_wv.reshape(B_CMP_LEN, hd, hd).astype(BF16), k_norm_w)

    cmp_start = np.arange(n_cmp) * B_CMP_STRIDE
    sel_start = np.arange(n_blk) * B_SEL_LEN
    ov = np.clip(np.minimum(cmp_start[:, None] + B_CMP_LEN, sel_start[None, :] + B_SEL_LEN)
                 - np.maximum(cmp_start[:, None], sel_start[None, :]), 0, None) / B_CMP_LEN
    ov[cmp_start + B_CMP_LEN > seq] = 0.0
    ov = np.pad(ov, ((0, 0), (NSA_SEL_OFF, hd - NSA_SEL_OFF - n_blk)))
    slopes_np = (2.0 ** (-8.0 * np.arange(1, heads + 1) / heads)).astype(np.float32)
    slopes = jnp.asarray(slopes_np).reshape(B_KV, rep)
    to_bf16 = lambda v: v.astype(ml_dtypes.bfloat16).astype(np.float32)
    s_hi = to_bf16(slopes_np)
    s_mid = to_bf16(slopes_np - s_hi)
    s_lo = to_bf16(slopes_np - s_hi - s_mid)
    pieces = jnp.asarray(np.stack([s_hi, s_mid, s_lo], axis=-1).reshape(B_KV, 3 * rep))

    rw = rep * hd
    cspec3 = pl.BlockSpec((None, None, n_cmp, hd), lambda b, g, i: (b, g, 0, 0))
    sspec3 = pl.BlockSpec((None, None, seq, 2 * hd), lambda b, g, i: (b, g, 0, 0))
    y = pl.pallas_call(
        functools.partial(_nsa_body, rep=rep, hd=hd, n_top=n_top, n_blk=n_blk, kb_sel=kb_sel),
        grid=(bsz, B_KV, seq // tq),
        in_specs=[pl.BlockSpec(memory_space=pltpu.SMEM), pl.BlockSpec(memory_space=pltpu.SMEM),
                  pl.BlockSpec((None, tq, rw), lambda b, g, i: (b, i, g)),
                  pl.BlockSpec((None, tq, rw), lambda b, g, i: (b, i, inner // rw + g)),
                  pl.BlockSpec((None, None, tq, 3 * rep), lambda b, g, i: (b, g, i, 0)),
                  pl.BlockSpec((1, hd), lambda b, g, i: (0, 0)),
                  cspec3, cspec3, sspec3, sspec3, sspec3, sspec3,
                  pl.BlockSpec((n_cmp, hd), lambda b, g, i: (0, 0))],
        out_specs=pl.BlockSpec((None, tq, rw), lambda b, g, i: (b, i, g)),
        out_shape=jax.ShapeDtypeStruct((bsz, seq, inner), BF16),
        scratch_shapes=[pltpu.VMEM((rep * tq, 2 * hd), BF16), pltpu.VMEM((rep * tq, 2 * hd), F32),
                        pltpu.VMEM((rep * tq, hd), F32), pltpu.VMEM((rep * tq, hd), F32),
                        pltpu.SMEM((seq // kb_sel,), jnp.int32)],
        compiler_params=_params("parallel", "parallel", "arbitrary"), name="b_attn",
    )(slopes, pieces, u, u, gts, q_norm_w.reshape(1, hd), kc, vc, ks, vs, kw, vw, jnp.asarray(ov, dtype=F32))
    return _out_proj(y.reshape(t, inner), w_out, x2d, next_w, "b_out")


def kernel(x, norm_w, a_w_in, a_conv_w, a_conv_b, a_gate_b, a_out_norm_w, a_w_out, b_w_in, b_cmp_pe, b_cmp_wk, b_cmp_wv, b_q_norm_w, b_k_norm_w, b_w_out, c_w_in, c_conv_w, c_conv_b, c_w_a, c_b_a, c_w_x, c_b_x, c_lambda, c_w_out, d_w_in, d_q_norm_w, d_k_norm_w, d_w_out):
    bsz, seq, d = x.shape
    depth = norm_w.shape[0]
    x2d = x.reshape(bsz * seq, d)
    hn = (_rmsnorm(x2d, norm_w[0]), None)
    for layer in range(depth):
        kind, j = layer % 4, layer // 4
        next_w = norm_w[layer + 1] if layer + 1 < depth else None
        if kind == 0:
            x2d, hn = _mlstm_layer(x2d, hn, next_w, bsz, seq, a_w_in[j], a_conv_w[j], a_conv_b[j], a_gate_b[j],
                                   a_out_norm_w[j], a_w_out[j])
        elif kind == 1:
            x2d, hn = _nsa_layer(x2d, hn, next_w, bsz, seq, b_w_in[j], b_cmp_pe[j], b_cmp_wk[j], b_cmp_wv[j],
                                 b_q_norm_w[j], b_k_norm_w[j], b_w_out[j])
        elif kind == 2:
            x2d, hn = _rglru_layer(x2d, hn, next_w, bsz, seq, c_w_in[j], c_conv_w[j], c_conv_b[j], c_w_a[j], c_b_a[j],
                                   c_w_x[j], c_b_x[j], c_lambda[j], c_w_out[j])
        else:
            x2d, hn = _dilated_layer(x2d, hn, next_w, bsz, seq, d_w_in[j], d_q_norm_w[j], d_k_norm_w[j], d_w_out[j])
    return x2d.reshape(bsz, seq, d)
```

```python
import functools
import math

import ml_dtypes
import numpy as np
import jax
import jax.numpy as jnp
from jax import lax
from jax.experimental import pallas as pl
from jax.experimental.pallas import tpu as pltpu

F32 = jnp.float32
BF16 = jnp.bfloat16

EPS = 1e-6
NEG = -1e30
FORCE = 1e4
CONV_W = 4

LANES = 128
SUBLANES = 8
V7X_VMEM_LIMIT = 56 * 1024 * 1024

A_CHUNK = 128
B_CMP_LEN = 32
B_CMP_STRIDE = 16
B_SEL_LEN = 64
B_N_SEL = 16
B_WIN = 512
B_KV = 4
NSA_POS_SPLIT = 128
NSA_SEL_OFF = 6
C_BLOCKS = 16
C_EXP = 8.0
D_PATTERNS = ((128, 1), (512, 4), (2048, 16))
D_QBLK = 128


def _params(*sem):
    return pltpu.CompilerParams(dimension_semantics=sem, vmem_limit_bytes=V7X_VMEM_LIMIT)


def _sigmoid(x):
    return 0.5 * jnp.tanh(0.5 * x) + 0.5


def _silu(x):
    return x * _sigmoid(x)


def _softplus(x):
    return jnp.maximum(x, 0.0) + jnp.log1p(jnp.exp(-jnp.abs(x)))


def _head_rms(x, w):
    return x * lax.rsqrt(jnp.mean(x * x, axis=-1, keepdims=True) + EPS) * w


def _dot(a, b):
    return jnp.dot(a.astype(BF16), b.astype(BF16), preferred_element_type=F32)


def _dot_nt(a, b):
    return lax.dot_general(a.astype(BF16), b.astype(BF16), (((1,), (1,)), ((), ())),
                           preferred_element_type=F32)


def _dot_tn(a, b):
    return lax.dot_general(a.astype(BF16), b.astype(BF16), (((0,), (0,)), ((), ())),
                           preferred_element_type=F32)


def _pick(n, cands):
    for c in cands:
        if n % c == 0:
            return c
    raise ValueError(f"no tile for {n}")


def _rmsnorm_body(x_ref, w_ref, o_ref):
    x = x_ref[...]
    o_ref[...] = _head_rms(x, w_ref[...]).astype(o_ref.dtype)


def _rmsnorm(x2d, w):
    t, d = x2d.shape
    tm = _pick(t, (256, 128, 8))
    return pl.pallas_call(
        _rmsnorm_body, grid=(t // tm,),
        in_specs=[pl.BlockSpec((tm, d), lambda i: (i, 0)), pl.BlockSpec((1, d), lambda i: (0, 0))],
        out_specs=pl.BlockSpec((tm, d), lambda i: (i, 0)),
        out_shape=jax.ShapeDtypeStruct((t, d), BF16),
        compiler_params=_params("parallel"), name="rmsnorm")(x2d, w.reshape(1, d))


def _mm_body(*refs, transposed, has_resid, has_ssq, emit_norm, k_dim, n_valid):
    refs = list(refs)
    a_ref, w_ref = refs.pop(0), refs.pop(0)
    r_ref = refs.pop(0) if has_resid else None
    ssq_ref = refs.pop(0) if has_ssq else None
    nw_ref = refs.pop(0) if emit_norm else None
    o_ref = refs.pop(0)
    w = w_ref[...].astype(a_ref.dtype)
    if transposed:
        acc = lax.dot_general(a_ref[...], w, (((1,), (1,)), ((), ())), preferred_element_type=F32)
    else:
        acc = jnp.dot(a_ref[...], w, preferred_element_type=F32)
    if n_valid is not None:
        col = pl.program_id(1) * acc.shape[1] + lax.broadcasted_iota(jnp.int32, acc.shape, 1)
        acc = jnp.where(col < n_valid, acc, 0.0)
    if has_ssq:
        acc = acc * lax.rsqrt(ssq_ref[...] * (1.0 / k_dim) + EPS)
    if has_resid:
        acc = r_ref[...] + acc
    if emit_norm:
        xs_ref, ssq_out_ref = refs
        xs_ref[...] = (acc * nw_ref[...]).astype(xs_ref.dtype)
        part = jnp.sum(acc * acc, axis=1, keepdims=True)

        @pl.when(pl.program_id(1) == 0)
        def _():
            ssq_out_ref[...] = part

        @pl.when(pl.program_id(1) > 0)
        def _():
            ssq_out_ref[...] += part
    if len(o_ref.shape) == 2:
        o_ref[...] = acc
    else:
        gw = o_ref.shape[2]
        for c in range(o_ref.shape[0]):
            o_ref[c] = acc[:, c * gw:(c + 1) * gw]


def _matmul(a, w, n=None, col0=0, resid=None, transposed=False, group=None, ssq=None, next_norm_w=None,
            name="matmul"):
    m, k = a.shape
    w_cols = w.shape[1]
    n = w_cols if n is None else n
    w_bytes = jnp.dtype(w.dtype).itemsize
    tm = _pick(m, (1024, 512, 256, 128))
    out_bytes = (8 if resid is not None else 4) + (2 if next_norm_w is not None else 0)
    col_blocks = (1 if ssq is not None else 0) + (1 if next_norm_w is not None else 0)
    tn = _pick(n, tuple(c for c in (1024, 768, 512, 384, 256, 128)
                        if 2 * (k * c * w_bytes + tm * k * 2 + tm * c * out_bytes + col_blocks * tm * LANES * 4)
                        + k * c * 2 <= V7X_VMEM_LIMIT - (4 << 20)))
    assert col0 % tn == 0
    cb = col0 // tn
    spec = pl.BlockSpec
    if transposed:
        w = jnp.swapaxes(w, 0, 1)
        w_spec = spec((tn, k), lambda i, j: (cb + j, 0))
    else:
        w_spec = spec((k, tn), lambda i, j: (0, cb + j))
    in_specs = [spec((tm, k), lambda i, j: (i, 0)), w_spec]
    args = [a, w]
    if resid is not None:
        in_specs.append(spec((tm, tn), lambda i, j: (i, j)))
        args.append(resid)
    if ssq is not None:
        in_specs.append(spec((tm, 1), lambda i, j: (i, 0)))
        args.append(ssq)
    if next_norm_w is not None:
        in_specs.append(spec((1, tn), lambda i, j: (0, j)))
        args.append(next_norm_w.reshape(1, n))
    if group is None:
        out_specs = [spec((tm, tn), lambda i, j: (i, j))]
        out_shape = [jax.ShapeDtypeStruct((m, n), F32)]
    else:
        assert tn % group == 0 and group % LANES == 0 and resid is None
        out_specs = [spec((tn // group, tm, group), lambda i, j: (j, i, 0))]
        out_shape = [jax.ShapeDtypeStruct((n // group, m, group), F32)]
    if next_norm_w is not None:
        out_specs += [spec((tm, tn), lambda i, j: (i, j)), spec((tm, 1), lambda i, j: (i, 0))]
        out_shape += [jax.ShapeDtypeStruct((m, n), BF16), jax.ShapeDtypeStruct((m, 1), F32)]
    body = functools.partial(_mm_body, transposed=transposed, has_resid=resid is not None, has_ssq=ssq is not None,
                             emit_norm=next_norm_w is not None, k_dim=k,
                             n_valid=w_cols - col0 if w_cols - col0 < n else None)
    col_sem = "arbitrary" if next_norm_w is not None else "parallel"
    out = pl.pallas_call(
        body, grid=(m // tm, n // tn), in_specs=in_specs, out_specs=out_specs, out_shape=out_shape,
        compiler_params=_params("parallel", col_sem), name=name)(*args)
    return out if next_norm_w is not None else out[0]


def _in_proj(act, w, **kw):
    return _matmul(act[0], w, ssq=act[1], **kw)


def _out_proj(y, w_out, x2d, next_w, name):
    if next_w is None:
        return _matmul(y, w_out, resid=x2d, name=name), None
    x_new, xs, part = _matmul(y, w_out, resid=x2d, next_norm_w=next_w, name=name)
    return x_new, (xs, part)


def _mlstm_body(gb_ref, q_ref, k_ref, v_ref, og_ref, z_ref, gcol_ref, grow_ref, cwq_ref, cwk_ref,
                cbq_ref, cbk_ref, onw_ref, y_ref, c_scr, n_scr, m_scr, extq, extk, *, hb, dqk, dv, interleave):
    c = pl.program_id(2)
    L = q_ref.shape[0]

    @pl.when(c == 0)
    def _():
        c_scr[...] = jnp.zeros_like(c_scr)
        n_scr[...] = jnp.zeros_like(n_scr)
        m_scr[...] = jnp.zeros_like(m_scr)
        extq[L:L + 8, :] = jnp.zeros((8, extq.shape[1]), F32)
        extk[L:L + 8, :] = jnp.zeros((8, extk.shape[1]), F32)

    def conv_silu(ext, x_ref, cw_ref, cb_ref):
        ext[0:8, :] = ext[L:L + 8, :]
        ext[8:L + 8, :] = x_ref[...]
        acc = cb_ref[...] + cw_ref[0:1, :] * ext[pl.ds(8 - CONV_W + 1, L), :]
        for j in range(1, CONV_W):
            acc = acc + cw_ref[j:j + 1, :] * ext[pl.ds(8 - CONV_W + 1 + j, L), :]
        return _silu(acc)

    q_all = conv_silu(extq, q_ref, cwq_ref, cbq_ref) * (dqk ** -0.5)
    k_all = conv_silu(extk, k_ref, cwk_ref, cbk_ref)
    t_io = lax.broadcasted_iota(jnp.int32, (L, L), 0)
    s_io = lax.broadcasted_iota(jnp.int32, (L, L), 1)
    causal = s_io <= t_io

    def head_stages(hh):
        h = pl.program_id(1) * hb + hh
        qs, vs = slice(hh * dqk, (hh + 1) * dqk), slice(hh * dv, (hh + 1) * dv)
        q, k, v = q_all[:, qs], k_all[:, qs], v_ref[:, vs]
        bi = gb_ref[0, h]
        bf = gb_ref[1, h]
        gcol = gcol_ref[hh]
        grow = grow_ref[hh]
        li_col = gcol[:, 0:1] + bi
        li_row = grow[0:1, :] + bi
        lf_col = -_softplus(-(gcol[:, 1:2] + bf))
        lf_row = -_softplus(-(grow[1:2, :] + bf))

        b_col = jnp.sum(jnp.where(causal, lf_row, 0.0), axis=1, keepdims=True)
        b_row = jnp.sum(jnp.where(t_io <= s_io, lf_col, 0.0), axis=0, keepdims=True)
        b_last = jnp.sum(lf_row, axis=1, keepdims=True)
        m_st = m_scr[hh]

        dmat = jnp.where(causal, b_col - b_row + li_row, NEG)
        inter = b_col + m_st
        m_t = jnp.maximum(inter, jnp.max(dmat, axis=1, keepdims=True))
        w_intra = jnp.exp(dmat - m_t)
        w_inter = jnp.exp(inter - m_t)
        c_st = c_scr[hh]
        n_st = n_scr[hh]
        yield
        qk = _dot_nt(q, k)
        q_c = _dot(q, c_st)
        yield
        a = w_intra * qk
        num = _dot(a, v) + w_inter * q_c
        qn = jnp.sum(a, axis=1, keepdims=True) + w_inter * jnp.sum(q * n_st, axis=1, keepdims=True)
        yield
        hc = num / jnp.maximum(jnp.abs(qn), jnp.exp(-m_t))

        g_row = b_last - b_row + li_row
        g_col = b_last - b_col + li_col
        m_new = jnp.maximum(b_last + m_st, jnp.max(g_row, axis=1, keepdims=True))
        ws_col = jnp.exp(g_col - m_new)
        decay = jnp.exp(b_last + m_st - m_new)
        kw = k * ws_col
        c_scr[hh] = decay * c_st + _dot_tn(kw, v)
        n_scr[hh] = decay * n_st + jnp.sum(kw, axis=0, keepdims=True)
        m_scr[hh] = m_new
        yield
        hn = _head_rms(hc, onw_ref[:, vs])
        y_ref[:, vs] = (hn * _sigmoid(og_ref[:, vs]) * _silu(z_ref[:, vs])).astype(y_ref.dtype)
        yield

    for h0 in range(0, hb, interleave):
        for _ in zip(*[head_stages(hh) for hh in range(h0, min(h0 + interleave, hb))]):
            pass


def _mlstm_layer(x2d, hn, next_w, bsz, seq, w_in, conv_w, conv_b, gate_b, out_norm_w, w_out):
    t, d = x2d.shape
    heads = gate_b.shape[-1]
    qk = conv_w.shape[-1] // 2
    inner = out_norm_w.shape[-1]
    dqk, dv = qk // heads, inner // heads
    n_main = 2 * qk + 3 * inner
    L = A_CHUNK
    nc = seq // L
    hb = heads
    assert dqk % LANES == 0 and dv % LANES == 0 and 2 * heads <= LANES and heads % hb == 0

    u = _in_proj(hn, w_in, n=n_main, transposed=True, name="a_in").reshape(bsz, seq, n_main)
    gts = _in_proj(hn, w_in, n=LANES, col0=n_main, transposed=True, name="a_gates")
    gts = gts[:, :2 * heads].reshape(bsz, nc, L, 2, heads)
    g_col = jnp.transpose(gts, (0, 4, 1, 2, 3))
    g_row = jnp.transpose(gts, (0, 4, 1, 3, 2))

    wq, wv = hb * dqk, hb * dv
    assert qk % wq == 0 and 2 * qk % wv == 0 and inner % wv == 0
    kq0, kv0, ko0, kz0 = qk // wq, 2 * qk // wv, (2 * qk + inner) // wv, (2 * qk + 2 * inner) // wv
    blk = lambda w, off: pl.BlockSpec((None, L, w), lambda b, h, c: (b, c, off + h))
    par = lambda r, w, off: pl.BlockSpec((r, w), lambda b, h, c: (0, off + h))
    y = pl.pallas_call(
        functools.partial(_mlstm_body, hb=hb, dqk=dqk, dv=dv, interleave=4),
        grid=(bsz, heads // hb, nc),
        in_specs=[pl.BlockSpec(memory_space=pltpu.SMEM),
                  blk(wq, 0), blk(wq, kq0), blk(wv, kv0), blk(wv, ko0), blk(wv, kz0),
                  pl.BlockSpec((None, hb, None, L, 2), lambda b, h, c: (b, h, c, 0, 0)),
                  pl.BlockSpec((None, hb, None, 2, L), lambda b, h, c: (b, h, c, 0, 0)),
                  par(CONV_W, wq, 0), par(CONV_W, wq, kq0), par(1, wq, 0), par(1, wq, kq0),
                  par(1, wv, 0)],
        out_specs=pl.BlockSpec((None, L, wv), lambda b, h, c: (b, c, h)),
        out_shape=jax.ShapeDtypeStruct((bsz, seq, inner), BF16),
        scratch_shapes=[pltpu.VMEM((hb, dqk, dv), F32), pltpu.VMEM((hb, 1, dqk), F32), pltpu.VMEM((hb, 1, 1), F32),
                        pltpu.VMEM((L + 8, wq), F32), pltpu.VMEM((L + 8, wq), F32)],
        compiler_params=_params("parallel", "parallel", "arbitrary"), name="a_mlstm",
    )(gate_b, u, u, u, u, u, g_col, g_row, conv_w, conv_w, conv_b.reshape(1, -1), conv_b.reshape(1, -1),
      out_norm_w.reshape(1, -1))
    return _out_proj(y.reshape(t, inner), w_out, x2d, next_w, "a_out")


def _rglru_body(x_ref, z_ref, cw_ref, cb_ref, wa_ref, wx_ref, ba_ref, bx_ref, lam_ref, y_ref,
                ext, xc_scr, a_scr, b_scr, h_scr, *, starts, win, chunk, scan_chunk):
    s = pl.program_id(1)
    ts, width = x_ref.shape

    @pl.when(s == 0)
    def _():
        ext[ts:ts + 8, :] = jnp.zeros((8, width), F32)
        h_scr[...] = jnp.zeros_like(h_scr)

    ext[0:8, :] = ext[ts:ts + 8, :]
    ext[8:ts + 8, :] = x_ref[...]
    for c0 in range(0, width, chunk):
        cs = slice(c0, c0 + chunk)
        xc = cb_ref[:, cs] + cw_ref[0:1, cs] * ext[pl.ds(8 - CONV_W + 1, ts), cs]
        for j in range(1, CONV_W):
            xc = xc + cw_ref[j:j + 1, cs] * ext[pl.ds(8 - CONV_W + 1 + j, ts), cs]
        xc_scr[:, cs] = xc

    a_scr[...] = jnp.zeros_like(a_scr)
    b_scr[...] = jnp.zeros_like(b_scr)
    for n, a0 in enumerate(starts):
        ws = slice(a0, a0 + win)
        xw = xc_scr[:, ws].astype(BF16)
        a_scr[:, ws] += jnp.dot(xw, wa_ref[n], preferred_element_type=F32)
        b_scr[:, ws] += jnp.dot(xw, wx_ref[n], preferred_element_type=F32)

    for c0 in range(0, width, chunk):
        cs = slice(c0, c0 + chunk)
        r = _sigmoid(a_scr[:, cs] + ba_ref[:, cs])
        ig = _sigmoid(b_scr[:, cs] + bx_ref[:, cs])
        log_a = (-C_EXP) * _softplus(-lam_ref[:, cs]) * r
        a = jnp.exp(log_a)
        a_scr[:, cs] = a
        b_scr[:, cs] = jnp.sqrt(-jnp.tanh(log_a) * (a * a + 1.0)) * (ig * xc_scr[:, cs])

    for c0 in range(0, width, scan_chunk):
        cs = slice(c0, c0 + scan_chunk)

        def row(t, h, cs=cs):
            h = a_scr[pl.ds(t, 1), cs] * h + b_scr[pl.ds(t, 1), cs]
            b_scr[pl.ds(t, 1), cs] = h
            return h

        h_scr[:, cs] = lax.fori_loop(0, ts, row, h_scr[:, cs], unroll=8)

    y_ref[...] = (b_scr[...] * _silu(z_ref[...])).astype(y_ref.dtype)


def _rglru_layer(x2d, hn, next_w, bsz, seq, w_in, conv_w, conv_b, w_a, b_a, w_x, b_x, lam, w_out):
    t, d = x2d.shape
    width = conv_w.shape[-1]
    nblk = w_a.shape[0]
    blk = width // nblk
    assert width % LANES == 0
    win = LANES
    while True:
        starts = tuple(min(n * blk // LANES * LANES, width - win) for n in range(nblk))
        if all(n * blk - a0 + blk <= win for n, a0 in enumerate(starts)):
            break
        win += LANES

    def window_weights(w):
        return jnp.stack([jnp.pad(w[n], ((n * blk - a0, win - blk - (n * blk - a0)),) * 2)
                          for n, a0 in enumerate(starts)]).astype(BF16)

    u = _in_proj(hn, w_in, name="c_in").reshape(bsz, seq, 2 * width)
    ts = _pick(seq, (128, 64, 8))
    chunk = _pick(width, (768, 512, 384, 256, 128))
    scan_chunk = _pick(width, (1792, 1536, 1024, 896, 768, 512, 384, 256, 128))
    full = lambda r: pl.BlockSpec((r, width), lambda b, s: (0, 0))
    wspec = pl.BlockSpec((nblk, win, win), lambda b, s: (0, 0, 0), pipeline_mode=pl.Buffered(1))
    row = lambda v: v.reshape(1, width)
    y = pl.pallas_call(
        functools.partial(_rglru_body, starts=starts, win=win, chunk=chunk, scan_chunk=scan_chunk),
        grid=(bsz, seq // ts),
        in_specs=[pl.BlockSpec((None, ts, width), lambda b, s: (b, s, 0)),
                  pl.BlockSpec((None, ts, width), lambda b, s: (b, s, 1)),
                  full(CONV_W), full(1), wspec, wspec, full(1), full(1), full(1)],
        out_specs=pl.BlockSpec((None, ts, width), lambda b, s: (b, s, 0)),
        out_shape=jax.ShapeDtypeStruct((bsz, seq, width), BF16),
        scratch_shapes=[pltpu.VMEM((ts + 8, width), F32), pltpu.VMEM((ts, width), F32),
                        pltpu.VMEM((ts, width), F32), pltpu.VMEM((ts, width), F32), pltpu.VMEM((1, width), F32)],
        compiler_params=_params("parallel", "arbitrary"), name="c_rglru",
    )(u, u, conv_w, row(conv_b), window_weights(w_a), window_weights(w_x), row(b_a), row(b_x), row(lam))
    return _out_proj(y.reshape(t, width), w_out, x2d, next_w, "c_out")


def _dilated_body(sl_ref, *refs, patterns, tq, mix_rows, group):
    n_pat = len(patterns)
    qkv = [refs[5 * g:5 * g + 5] for g in range(n_pat)]
    z_ref, qw_ref, kw_ref, y_ref, knbuf, vbuf, obuf, lbuf = refs[5 * n_pat:]
    first_tile = pl.program_id(1) == 0
    slope = sl_ref[pl.program_id(2)]
    tt, hd = z_ref.shape
    scale = hd ** -0.5
    qi = lax.broadcasted_iota(jnp.int32, (tq, 2 * tq), 0)
    kj = lax.broadcasted_iota(jnp.int32, (tq, 2 * tq), 1)
    steps = qi + tq - kj

    for g, (window, dil) in enumerate(patterns):
        q_ref, kp_ref, kc_ref, vp_ref, vc_ref = qkv[g]
        halo = tq * dil
        knbuf[0:halo, :] = _head_rms(kp_ref[...], kw_ref[...])
        knbuf[halo:halo + tt, :] = _head_rms(kc_ref[...], kw_ref[...])
        vbuf[0:halo, :] = vp_ref[...]
        vbuf[halo:halo + tt, :] = vc_ref[...]
        bias = jnp.where((steps >= 0) & (steps <= window // dil), (-slope) * (steps * dil).astype(F32), NEG)
        bias_first = jnp.where((kj >= tq) | jnp.logical_not(first_tile), bias, NEG)
        subtiles = [(r + halo * j, j == 0) for r in range(dil) for j in range(tt // halo)]
        for i0 in range(0, len(subtiles), group):
            grp = subtiles[i0:i0 + group]
            rows = [(lambda n, st=st: pl.ds(st, n, stride=dil) if dil > 1 else pl.ds(st, n)) for st, _ in grp]
            ss = [_dot_nt(_head_rms(q_ref[rw(tq), :], qw_ref[...]) * scale, knbuf[rw(2 * tq), :])
                  + (bias_first if first else bias) for rw, (_, first) in zip(rows, grp)]
            ms = [jnp.max(s, axis=1, keepdims=True) for s in ss]
            ps = [jnp.exp(s - m) for s, m in zip(ss, ms)]
            dens = [jnp.sum(p, axis=1, keepdims=True) for p in ps]
            for rw, p, den, m in zip(rows, ps, dens, ms):
                obuf[g, rw(tq), :] = _dot(p / den, vbuf[rw(2 * tq), :])
                lbuf[g, rw(tq), :] = jnp.broadcast_to(m + jnp.log(den), (tq, hd))

    for c in range(0, tt, mix_rows):
        cs = slice(c, c + mix_rows)
        ls = [lbuf[g, cs, :] for g in range(n_pat)]
        mx = functools.reduce(jnp.maximum, ls)
        es = [jnp.exp(l - mx) for l in ls]
        tot = functools.reduce(lambda a, b: a + b, es)
        o = sum((e / tot) * obuf[g, cs, :] for g, e in enumerate(es))
        y_ref[cs, :] = (o * _silu(z_ref[cs, :])).astype(y_ref.dtype)


def _dilated_layer(x2d, hn, next_w, bsz, seq, w_in, q_norm_w, k_norm_w, w_out):
    t, d = x2d.shape
    hd = q_norm_w.shape[-1]
    inner = w_out.shape[0]
    heads = inner // hd
    n_pat = len(D_PATTERNS)
    cols = 3 * n_pat * inner + inner
    tq = D_QBLK
    max_dil = max(dil for _, dil in D_PATTERNS)
    tt = tq * max_dil
    assert hd == LANES and seq % tt == 0
    assert all(window // dil <= tq and tt % (tq * dil) == 0 for window, dil in D_PATTERNS)
    slopes = jnp.asarray(2.0 ** (-8.0 * np.arange(1, heads + 1) / heads), dtype=F32)
    u = _in_proj(hn, w_in, group=hd, name="d_in").reshape(cols // hd, bsz, seq, hd)

    in_specs = [pl.BlockSpec(memory_space=pltpu.SMEM)]
    args = [slopes]
    for g, (window, dil) in enumerate(D_PATTERNS):
        halo = tq * dil
        cur = lambda part, g=g: pl.BlockSpec((None, None, tt, hd),
                                             lambda b, s, h: ((3 * g + part) * heads + h, b, s, 0))
        prev = lambda part, g=g, halo=halo: pl.BlockSpec(
            (None, None, halo, hd),
            lambda b, s, h: ((3 * g + part) * heads + h, b, jnp.maximum(s * (tt // halo) - 1, 0), 0))
        in_specs += [cur(0), prev(1), cur(1), prev(2), cur(2)]
        args += [u] * 5
    wspec = pl.BlockSpec((1, hd), lambda b, s, h: (0, 0))
    in_specs += [pl.BlockSpec((None, None, tt, hd), lambda b, s, h: (3 * n_pat * heads + h, b, s, 0)), wspec, wspec]
    args += [u, q_norm_w.reshape(1, hd), k_norm_w.reshape(1, hd)]
    y = pl.pallas_call(
        functools.partial(_dilated_body, patterns=D_PATTERNS, tq=tq, mix_rows=_pick(tt, (256, 128)), group=4),
        grid=(bsz, seq // tt, heads), in_specs=in_specs,
        out_specs=pl.BlockSpec((None, tt, hd), lambda b, s, h: (b, s, h)),
        out_shape=jax.ShapeDtypeStruct((bsz, seq, inner), BF16),
        scratch_shapes=[pltpu.VMEM((2 * tt, hd), F32), pltpu.VMEM((2 * tt, hd), F32),
                        pltpu.VMEM((n_pat, tt, hd), F32), pltpu.VMEM((n_pat, tt, hd), F32)],
        compiler_params=_params("parallel", "parallel", "parallel"), name="d_attn",
    )(*args)
    return _out_proj(y.reshape(t, inner), w_out, x2d, next_w, "d_out")


def _nsa_prep_body(k0_ref, v0_ref, k1_ref, v1_ref, k2_ref, v2_ref, pe_ref, wk_ref, wv_ref, knw_ref,
                   kc_ref, vc_ref, ks_ref, vs_ref, kw_ref, vw_ref, *, stride):
    n_c = kc_ref.shape[0]

    def compress(x_ref, w_ref, pe):
        p1 = jnp.zeros(kc_ref.shape, F32)
        p2 = jnp.zeros(kc_ref.shape, F32)
        for l in range(stride):
            x = x_ref[pl.ds(l, n_c, stride=stride), :]
            p1 = p1 + _dot(x + pe[l:l + 1, :], w_ref[l])
            p2 = p2 + _dot(x + pe[stride + l:stride + l + 1, :], w_ref[stride + l])
        return p1 + pltpu.roll(p2, n_c - 1, 0)

    kc_ref[...] = _head_rms(compress(k0_ref, wk_ref, pe_ref[0]), knw_ref[0:1, :])
    vc_ref[...] = compress(v0_ref, wv_ref, pe_ref[1])

    seq, hd = k1_ref.shape
    pos = lax.broadcasted_iota(jnp.int32, (seq, hd), 0)
    lane = lax.broadcasted_iota(jnp.int32, (seq, hd), 1)
    split_shift = NSA_POS_SPLIT.bit_length() - 1
    pos_hi = ((pos >> split_shift) << split_shift).astype(F32)
    pos_lo = (pos & (NSA_POS_SPLIT - 1)).astype(F32)
    pos_cols = jnp.where(lane < 3, pos_hi, jnp.where(lane < NSA_SEL_OFF, pos_lo, 0.0))
    sel_blk = pos >> (B_SEL_LEN.bit_length() - 1)
    onehot = jnp.where(lane - NSA_SEL_OFF == sel_blk, NEG, 0.0)
    ones = jnp.ones((seq, hd), vs_ref.dtype)
    ks_ref[:, 0:hd] = _head_rms(k1_ref[...], knw_ref[1:2, :]).astype(ks_ref.dtype)
    ks_ref[:, hd:2 * hd] = (pos_cols + onehot).astype(ks_ref.dtype)
    vs_ref[:, 0:hd] = v1_ref[...].astype(vs_ref.dtype)
    vs_ref[:, hd:2 * hd] = ones
    kw_ref[:, 0:hd] = _head_rms(k2_ref[...], knw_ref[2:3, :]).astype(kw_ref.dtype)
    kw_ref[:, hd:2 * hd] = pos_cols.astype(kw_ref.dtype)
    vw_ref[:, 0:hd] = v2_ref[...].astype(vw_ref.dtype)
    vw_ref[:, hd:2 * hd] = ones


def _nsa_body(sl_ref, sp_ref, q_ref, z_ref, g_ref, qw_ref, kc_ref, vc_ref, ks_ref, vs_ref, kw_ref, vw_ref, ov_ref,
              y_ref, q2, acc, m_scr, o_scr, used, *, rep, hd, n_top, n_blk, kb_sel):
    g = pl.program_id(1)
    i = pl.program_id(2)
    tq = q_ref.shape[0]
    n_cmp = kc_ref.shape[0]
    t0 = i * tq
    scale = hd ** -0.5
    rows = lambda r: slice(r * tq, (r + 1) * tq)

    for r in range(rep):
        q2[rows(r), 0:hd] = (_head_rms(q_ref[:, r * hd:(r + 1) * hd], qw_ref[...]) * scale).astype(q2.dtype)
    gts = _sigmoid(g_ref[...])

    def gate(r, branch):
        return gts[:, 3 * r + branch:3 * r + branch + 1]

    t_col = t0 + lax.broadcasted_iota(jnp.int32, (tq, 1), 0)
    lane = lax.broadcasted_iota(jnp.int32, (tq, hd), 1)
    pieces = []
    for r in range(rep):
        pieces.append(jnp.where((lane == 0) | (lane == 3), sp_ref[g, 3 * r],
                                jnp.where((lane == 1) | (lane == 4), sp_ref[g, 3 * r + 1],
                                          jnp.where((lane == 2) | (lane == 5), sp_ref[g, 3 * r + 2], 0.0))))
        q2[rows(r), hd:2 * hd] = pieces[r].astype(q2.dtype)

    n_w = B_WIN + tq
    w_start = pl.multiple_of(jnp.maximum(t0 - B_WIN, 0), tq)
    w_dist = t_col - (w_start + lax.broadcasted_iota(jnp.int32, (tq, n_w), 1))
    w_bias = jnp.where((w_dist >= 0) & (w_dist < B_WIN), 0.0, NEG)
    kblk = kw_ref[pl.ds(w_start, n_w), :]
    vblk = vw_ref[pl.ds(w_start, n_w), :]
    s_all = lax.dot_general(q2[...], kblk, (((1,), (1,)), ((), ())), preferred_element_type=F32)
    half = rep // 2
    for r0 in (0, half):
        ss = [s_all[rows(r), :] + w_bias for r in range(r0, r0 + half)]
        ms = [jnp.max(s, axis=1, keepdims=True) for s in ss]
        ps = [jnp.exp(s - m).astype(BF16) for s, m in zip(ss, ms)]
        for r, p in zip(range(r0, r0 + half), ps):
            res = jnp.dot(p, vblk, preferred_element_type=F32)
            o_scr[rows(r), :] = gate(r, 2) * (res[:, 0:hd] / res[:, hd:2 * hd])

    cmp_end = lax.broadcasted_iota(jnp.int32, (tq, n_cmp), 1) * B_CMP_STRIDE + (B_CMP_LEN - 1)
    dist_ci = t_col - cmp_end
    mask_c = dist_ci >= 0
    dist_c = dist_ci.astype(F32)
    s_all = _dot_nt(q2[:, 0:hd], kc_ref[...])
    ss = [jnp.where(mask_c, s_all[rows(r), :] - sl_ref[g, r] * dist_c, NEG) for r in range(rep)]
    ms = [jnp.max(s, axis=1, keepdims=True) for s in ss]
    es = [jnp.where(mask_c, jnp.exp(s - m), 0.0) for s, m in zip(ss, ms)]
    dens = [jnp.sum(e, axis=1, keepdims=True) for e in es]
    ps = [e / jnp.where(den > 0, den, 1.0) for e, den in zip(es, dens)]
    psum = functools.reduce(lambda a, b: a + b, ps)
    o_c = _dot(jnp.concatenate([p.astype(BF16) for p in ps], axis=0), vc_ref[...])
    for r in range(rep):
        o_scr[rows(r), :] += gate(r, 0) * o_c[rows(r), :]

    imp = jnp.dot(psum, ov_ref[...], preferred_element_type=F32, precision=lax.Precision.HIGHEST)
    blk = lane - NSA_SEL_OFF
    in_range = (blk >= 0) & (blk < n_blk)
    cur = t_col >> (B_SEL_LEN.bit_length() - 1)
    forced = (blk == 0) | (blk == cur) | (blk == cur - 1)
    imp = jnp.where(forced, imp + FORCE, imp)
    imp = jnp.where(in_range & (blk * B_SEL_LEN <= t_col), imp, NEG)
    imp_t = imp.T
    n_grp = -(-(NSA_SEL_OFF + n_blk) // SUBLANES)
    grp = [imp_t[SUBLANES * k:SUBLANES * (k + 1), :] for k in range(n_grp)]
    cnt = [jnp.zeros((SUBLANES, tq), F32) for _ in range(n_grp)]
    sub = lax.broadcasted_iota(jnp.int32, (SUBLANES, tq), 0)
    for mb in range(n_blk):
        slot = NSA_SEL_OFF + mb
        row = imp_t[slot:slot + 1, :]
        for k in range(n_grp):
            if SUBLANES * k > slot:
                beats = row >= grp[k]
            elif SUBLANES * (k + 1) <= slot:
                beats = row > grp[k]
            else:
                beats = (row > grp[k]) | ((row == grp[k]) & (sub > slot - SUBLANES * k))
            cnt[k] = cnt[k] + jnp.where(beats, 1.0, 0.0)
    cnt_t = jnp.concatenate(cnt + [jnp.zeros((hd - SUBLANES * n_grp, tq), F32)], axis=0)
    not_sel = jnp.where((cnt_t < n_top) & (imp_t > NEG / 2), 0.0, 1.0).T

    sel_bf = jnp.where(in_range, 1.0 - not_sel, 0.0).astype(BF16)
    per_slot = jnp.dot(jnp.ones((SUBLANES, tq), BF16), sel_bf, preferred_element_type=F32)
    slot_io = lax.broadcasted_iota(jnp.int32, (hd, hd), 0) - NSA_SEL_OFF
    kb_io = lax.broadcasted_iota(jnp.int32, (hd, hd), 1)
    kb_shift = (kb_sel // B_SEL_LEN).bit_length() - 1
    group = jnp.where((slot_io >= 0) & ((slot_io >> kb_shift) == kb_io), 1.0, 0.0).astype(BF16)
    per_kb = jnp.dot(per_slot.astype(BF16), group, preferred_element_type=F32)
    lane_s = lax.broadcasted_iota(jnp.int32, (SUBLANES, hd), 1)
    for kb in range(used.shape[0]):
        used[kb] = jnp.max(jnp.where(lane_s == kb, per_kb, 0.0)).astype(jnp.int32)

    for r in range(rep):
        q2[rows(r), hd:2 * hd] = jnp.where(in_range, not_sel, pieces[r]).astype(q2.dtype)

    acc[...] = jnp.zeros_like(acc)
    m_scr[...] = jnp.full(m_scr.shape, NEG, F32)
    n_chunk = kb_sel // hd

    def sel_block(start, bias):
        kblk = ks_ref[pl.ds(start, kb_sel), :]
        vblk = vs_ref[pl.ds(start, kb_sel), :]
        s_all = lax.dot_general(q2[...], kblk, (((1,), (1,)), ((), ())), preferred_element_type=F32)
        ss = [s_all[rows(r), :] if bias is None else s_all[rows(r), :] + bias for r in range(rep)]
        mxs = [functools.reduce(jnp.maximum, [s[:, c * hd:(c + 1) * hd] for c in range(n_chunk)]) for s in ss]
        m_olds = [m_scr[rows(r), :] for r in range(rep)]
        m_news = [jnp.maximum(m_old, jnp.max(mx, axis=1, keepdims=True))
                  for m_old, mx in zip(m_olds, mxs)]
        alphas = [jnp.exp(m_old - m_new) for m_old, m_new in zip(m_olds, m_news)]
        ps = [jnp.exp(s - jnp.concatenate([m_new] * n_chunk, axis=1)).astype(BF16) for s, m_new in zip(ss, m_news)]
        for r in range(rep):
            acc[rows(r), :] = (jnp.concatenate([alphas[r], alphas[r]], axis=1) * acc[rows(r), :]
                               + jnp.dot(ps[r], vblk, preferred_element_type=F32))
            m_scr[rows(r), :] = m_news[r]

    def sel_step(kb, carry):
        @pl.when(used[kb] > 0)
        def _():
            sel_block(pl.multiple_of(kb * kb_sel, kb_sel), None)

        return carry

    n_full = lax.div(t0, kb_sel)
    lax.fori_loop(0, n_full, sel_step, 0)
    d_start = pl.multiple_of(n_full * kb_sel, kb_sel)
    d_pos = d_start + lax.broadcasted_iota(jnp.int32, (tq, kb_sel), 1)
    sel_block(d_start, jnp.where(d_pos <= t_col, 0.0, NEG))
    for r in range(rep):
        o_scr[rows(r), :] += gate(r, 1) * (acc[rows(r), 0:hd] / acc[rows(r), hd:2 * hd])

    for r in range(rep):
        hs = slice(r * hd, (r + 1) * hd)
        y_ref[:, hs] = (o_scr[rows(r), :] * _silu(z_ref[:, hs])).astype(y_ref.dtype)


def _nsa_layer(x2d, hn, next_w, bsz, seq, w_in, cmp_pe, cmp_wk, cmp_wv, q_norm_w, k_norm_w, w_out):
    t, d = x2d.shape
    hd = q_norm_w.shape[-1]
    inner = w_out.shape[0]
    heads = inner // hd
    rep = heads // B_KV
    kvw = B_KV * hd
    n_main = 2 * inner + 6 * kvw
    assert B_CMP_LEN == 2 * B_CMP_STRIDE and seq % B_CMP_STRIDE == 0 and 3 * heads <= LANES
    n_cmp = seq // B_CMP_STRIDE
    n_blk = seq // B_SEL_LEN
    n_top = min(B_N_SEL, n_blk)
    tq = 128
    kb_sel = 512
    assert hd == LANES and NSA_SEL_OFF + n_blk <= hd and NSA_POS_SPLIT == hd
    assert kb_sel % tq == 0 and seq % kb_sel == 0 and B_WIN % tq == 0 and seq >= B_WIN + tq

    u = _in_proj(hn, w_in, n=n_main, transposed=True, name="b_in").reshape(bsz, seq, n_main)
    gts = _in_proj(hn, w_in, n=LANES, col0=n_main, transposed=True, name="b_gates")
    gts = gts[:, :3 * heads].reshape(bsz, seq, B_KV, 3 * rep)
    gts = jnp.transpose(gts, (0, 2, 1, 3))

    kv0 = 2 * inner // hd
    kvspec = lambda br, kv: pl.BlockSpec((None, seq, hd), lambda b, g: (b, 0, kv0 + (br * 2 + kv) * B_KV + g))
    cspec = pl.BlockSpec((None, None, n_cmp, hd), lambda b, g: (b, g, 0, 0))
    sspec = pl.BlockSpec((None, None, seq, 2 * hd), lambda b, g: (b, g, 0, 0))
    wspec = pl.BlockSpec((B_CMP_LEN, hd, hd), lambda b, g: (0, 0, 0))
    c_sh = jax.ShapeDtypeStruct((bsz, B_KV, n_cmp, hd), F32)
    s_sh = jax.ShapeDtypeStruct((bsz, B_KV, seq, 2 * hd), BF16)
    kc, vc, ks, vs, kw, vw = pl.pallas_call(
        functools.partial(_nsa_prep_body, stride=B_CMP_STRIDE),
        grid=(bsz, B_KV),
        in_specs=[kvspec(0, 0), kvspec(0, 1), kvspec(1, 0), kvspec(1, 1), kvspec(2, 0), kvspec(2, 1),
                  pl.BlockSpec((2, B_CMP_LEN, hd), lambda b, g: (0, 0, 0)), wspec, wspec,
                  pl.BlockSpec((3, hd), lambda b, g: (0, 0))],
        out_specs=[cspec, cspec, sspec, sspec, sspec, sspec],
        out_shape=[c_sh, c_sh, s_sh, s_sh, s_sh, s_sh],
        compiler_params=_params("parallel", "parallel"), name="b_prep",
    )(u, u, u, u, u, u, cmp_pe, cmp_wk.reshape(B_CMP_LEN, hd, hd).astype(BF16),
      cmp_wv.reshape(B_CMP_LEN, hd, hd).astype(BF16), k_norm_w)

    cmp_start = np.arange(n_cmp) * B_CMP_STRIDE
    sel_start = np.arange(n_blk) * B_SEL_LEN
    ov = np.clip(np.minimum(cmp_start[:, None] + B_CMP_LEN, sel_start[None, :] + B_SEL_LEN)
                 - np.maximum(cmp_start[:, None], sel_start[None, :]), 0, None) / B_CMP_LEN
    ov[cmp_start + B_CMP_LEN > seq] = 0.0
    ov = np.pad(ov, ((0, 0), (NSA_SEL_OFF, hd - NSA_SEL_OFF - n_blk)))
    slopes_np = (2.0 ** (-8.0 * np.arange(1, heads + 1) / heads)).astype(np.float32)
    slopes = jnp.asarray(slopes_np).reshape(B_KV, rep)
    to_bf16 = lambda v: v.astype(ml_dtypes.bfloat16).astype(np.float32)
    s_hi = to_bf16(slopes_np)
    s_mid = to_bf16(slopes_np - s_hi)
    s_lo = to_bf16(slopes_np - s_hi - s_mid)
    pieces = jnp.asarray(np.stack([s_hi, s_mid, s_lo], axis=-1).reshape(B_KV, 3 * rep))

    rw = rep * hd
    cspec3 = pl.BlockSpec((None, None, n_cmp, hd), lambda b, g, i: (b, g, 0, 0))
    sspec3 = pl.BlockSpec((None, None, seq, 2 * hd), lambda b, g, i: (b, g, 0, 0))
    y = pl.pallas_call(
        functools.partial(_nsa_body, rep=rep, hd=hd, n_top=n_top, n_blk=n_blk, kb_sel=kb_sel),
        grid=(bsz, B_KV, seq // tq),
        in_specs=[pl.BlockSpec(memory_space=pltpu.SMEM), pl.BlockSpec(memory_space=pltpu.SMEM),
                  pl.BlockSpec((None, tq, rw), lambda b, g, i: (b, i, g)),
                  pl.BlockSpec((None, tq, rw), lambda b, g, i: (b, i, inner // rw + g)),
                  pl.BlockSpec((None, None, tq, 3 * rep), lambda b, g, i: (b, g, i, 0)),
                  pl.BlockSpec((1, hd), lambda b, g, i: (0, 0)),
                  cspec3, cspec3, sspec3, sspec3, sspec3, sspec3,
                  pl.BlockSpec((n_cmp, hd), lambda b, g, i: (0, 0))],
        out_specs=pl.BlockSpec((None, tq, rw), lambda b, g, i: (b, i, g)),
        out_shape=jax.ShapeDtypeStruct((bsz, seq, inner), BF16),
        scratch_shapes=[pltpu.VMEM((rep * tq, 2 * hd), BF16), pltpu.VMEM((rep * tq, 2 * hd), F32),
                        pltpu.VMEM((rep * tq, hd), F32), pltpu.VMEM((rep * tq, hd), F32),
                        pltpu.SMEM((seq // kb_sel,), jnp.int32)],
        compiler_params=_params("parallel", "parallel", "arbitrary"), name="b_attn",
    )(slopes, pieces, u, u, gts, q_norm_w.reshape(1, hd), kc, vc, ks, vs, kw, vw, jnp.asarray(ov, dtype=F32))
    return _out_proj(y.reshape(t, inner), w_out, x2d, next_w, "b_out")


def kernel(x, norm_w, a_w_in, a_conv_w, a_conv_b, a_gate_b, a_out_norm_w, a_w_out, b_w_in, b_cmp_pe, b_cmp_wk, b_cmp_wv, b_q_norm_w, b_k_norm_w, b_w_out, c_w_in, c_conv_w, c_conv_b, c_w_a, c_b_a, c_w_x, c_b_x, c_lambda, c_w_out, d_w_in, d_q_norm_w, d_k_norm_w, d_w_out):
    bsz, seq, d = x.shape
    depth = norm_w.shape[0]
    x2d = x.reshape(bsz * seq, d)
    hn = (_rmsnorm(x2d, norm_w[0]), None)
    for layer in range(depth):
        kind, j = layer % 4, layer // 4
        next_w = norm_w[layer + 1] if layer + 1 < depth else None
        if kind == 0:
            x2d, hn = _mlstm_layer(x2d, hn, next_w, bsz, seq, a_w_in[j], a_conv_w[j], a_conv_b[j], a_gate_b[j],
                                   a_out_norm_w[j], a_w_out[j])
        elif kind == 1:
            x2d, hn = _nsa_layer(x2d, hn, next_w, bsz, seq, b_w_in[j], b_cmp_pe[j], b_cmp_wk[j], b_cmp_wv[j],
                                 b_q_norm_w[j], b_k_norm_w[j], b_w_out[j])
        elif kind == 2:
            x2d, hn = _rglru_layer(x2d, hn, next_w, bsz, seq, c_w_in[j], c_conv_w[j], c_conv_b[j], c_w_a[j], c_b_a[j],
                                   c_w_x[j], c_b_x[j], c_lambda[j], c_w_out[j])
        else:
            x2d, hn = _dilated_layer(x2d, hn, next_w, bsz, seq, d_w_in[j], d_q_norm_w[j], d_k_norm_w[j], d_w_out[j])
    return x2d.reshape(bsz, seq, d)
```

```python
import functools
import math

import ml_dtypes
import numpy as np
import jax
import jax.numpy as jnp
from jax import lax
from jax.experimental import pallas as pl
from jax.experimental.pallas import tpu as pltpu

F32 = jnp.float32
BF16 = jnp.bfloat16

EPS = 1e-6
NEG = -1e30
FORCE = 1e4
CONV_W = 4

LANES = 128
SUBLANES = 8
V7X_VMEM_LIMIT = 56 * 1024 * 1024

A_CHUNK = 128
B_CMP_LEN = 32
B_CMP_STRIDE = 16
B_SEL_LEN = 64
B_N_SEL = 16
B_WIN = 512
B_KV = 4
NSA_POS_SPLIT = 128
NSA_SEL_OFF = 6
C_BLOCKS = 16
C_EXP = 8.0
D_PATTERNS = ((128, 1), (512, 4), (2048, 16))
D_QBLK = 128


def _params(*sem):
    return pltpu.CompilerParams(dimension_semantics=sem, vmem_limit_bytes=V7X_VMEM_LIMIT)


def _sigmoid(x):
    return 0.5 * jnp.tanh(0.5 * x) + 0.5


def _silu(x):
    return x * _sigmoid(x)


def _softplus(x):
    return jnp.maximum(x, 0.0) + jnp.log1p(jnp.exp(-jnp.abs(x)))


def _head_rms(x, w):
    return x * lax.rsqrt(jnp.mean(x * x, axis=-1, keepdims=True) + EPS) * w


def _dot(a, b):
    return jnp.dot(a.astype(BF16), b.astype(BF16), preferred_element_type=F32)


def _dot_nt(a, b):
    return lax.dot_general(a.astype(BF16), b.astype(BF16), (((1,), (1,)), ((), ())),
                           preferred_element_type=F32)


def _dot_tn(a, b):
    return lax.dot_general(a.astype(BF16), b.astype(BF16), (((0,), (0,)), ((), ())),
                           preferred_element_type=F32)


def _pick(n, cands):
    for c in cands:
        if n % c == 0:
            return c
    raise ValueError(f"no tile for {n}")


def _rmsnorm_body(x_ref, w_ref, o_ref):
    x = x_ref[...]
    o_ref[...] = _head_rms(x, w_ref[...]).astype(o_ref.dtype)


def _rmsnorm(x2d, w):
    t, d = x2d.shape
    tm = _pick(t, (256, 128, 8))
    return pl.pallas_call(
        _rmsnorm_body, grid=(t // tm,),
        in_specs=[pl.BlockSpec((tm, d), lambda i: (i, 0)), pl.BlockSpec((1, d), lambda i: (0, 0))],
        out_specs=pl.BlockSpec((tm, d), lambda i: (i, 0)),
        out_shape=jax.ShapeDtypeStruct((t, d), BF16),
        compiler_params=_params("parallel"), name="rmsnorm")(x2d, w.reshape(1, d))


def _mm_body(*refs, transposed, has_resid, has_ssq, emit_norm, k_dim, n_valid):
    refs = list(refs)
    a_ref, w_ref = refs.pop(0), refs.pop(0)
    r_ref = refs.pop(0) if has_resid else None
    ssq_ref = refs.pop(0) if has_ssq else None
    nw_ref = refs.pop(0) if emit_norm else None
    o_ref = refs.pop(0)
    w = w_ref[...].astype(a_ref.dtype)
    if transposed:
        acc = lax.dot_general(a_ref[...], w, (((1,), (1,)), ((), ())), preferred_element_type=F32)
    else:
        acc = jnp.dot(a_ref[...], w, preferred_element_type=F32)
    if n_valid is not None:
        col = pl.program_id(1) * acc.shape[1] + lax.broadcasted_iota(jnp.int32, acc.shape, 1)
        acc = jnp.where(col < n_valid, acc, 0.0)
    if has_ssq:
        acc = acc * lax.rsqrt(ssq_ref[...] * (1.0 / k_dim) + EPS)
    if has_resid:
        acc = r_ref[...] + acc
    if emit_norm:
        xs_ref, ssq_out_ref = refs
        xs_ref[...] = (acc * nw_ref[...]).astype(xs_ref.dtype)
        part = jnp.sum(acc * acc, axis=1, keepdims=True)

        @pl.when(pl.program_id(1) == 0)
        def _():
            ssq_out_ref[...] = part

        @pl.when(pl.program_id(1) > 0)
        def _():
            ssq_out_ref[...] += part
    if len(o_ref.shape) == 2:
        o_ref[...] = acc
    else:
        gw = o_ref.shape[2]
        for c in range(o_ref.shape[0]):
            o_ref[c] = acc[:, c * gw:(c + 1) * gw]


def _matmul(a, w, n=None, col0=0, resid=None, transposed=False, group=None, ssq=None, next_norm_w=None,
            name="matmul"):
    m, k = a.shape
    w_cols = w.shape[1]
    n = w_cols if n is None else n
    w_bytes = jnp.dtype(w.dtype).itemsize
    tm = _pick(m, (1024, 512, 256, 128))
    out_bytes = (8 if resid is not None else 4) + (2 if next_norm_w is not None else 0)
    col_blocks = (1 if ssq is not None else 0) + (1 if next_norm_w is not None else 0)
    tn = _pick(n, tuple(c for c in (1024, 768, 512, 384, 256, 128)
                        if 2 * (k * c * w_bytes + tm * k * 2 + tm * c * out_bytes + col_blocks * tm * LANES * 4)
                        + k * c * 2 <= V7X_VMEM_LIMIT - (4 << 20)))
    assert col0 % tn == 0
    cb = col0 // tn
    spec = pl.BlockSpec
    if transposed:
        w = jnp.swapaxes(w, 0, 1)
        w_spec = spec((tn, k), lambda i, j: (cb + j, 0))
    else:
        w_spec = spec((k, tn), lambda i, j: (0, cb + j))
    in_specs = [spec((tm, k), lambda i, j: (i, 0)), w_spec]
    args = [a, w]
    if resid is not None:
        in_specs.append(spec((tm, tn), lambda i, j: (i, j)))
        args.append(resid)
    if ssq is not None:
        in_specs.append(spec((tm, 1), lambda i, j: (i, 0)))
        args.append(ssq)
    if next_norm_w is not None:
        in_specs.append(spec((1, tn), lambda i, j: (0, j)))
        args.append(next_norm_w.reshape(1, n))
    if group is None:
        out_specs = [spec((tm, tn), lambda i, j: (i, j))]
        out_shape = [jax.ShapeDtypeStruct((m, n), F32)]
    else:
        assert tn % group == 0 and group % LANES == 0 and resid is None
        out_specs = [spec((tn // group, tm, group), lambda i, j: (j, i, 0))]
        out_shape = [jax.ShapeDtypeStruct((n // group, m, group), F32)]
    if next_norm_w is not None:
        out_specs += [spec((tm, tn), lambda i, j: (i, j)), spec((tm, 1), lambda i, j: (i, 0))]
        out_shape += [jax.ShapeDtypeStruct((m, n), BF16), jax.ShapeDtypeStruct((m, 1), F32)]
    body = functools.partial(_mm_body, transposed=transposed, has_resid=resid is not None, has_ssq=ssq is not None,
                             emit_norm=next_norm_w is not None, k_dim=k,
                             n_valid=w_cols - col0 if w_cols - col0 < n else None)
    col_sem = "arbitrary" if next_norm_w is not None else "parallel"
    out = pl.pallas_call(
        body, grid=(m // tm, n // tn), in_specs=in_specs, out_specs=out_specs, out_shape=out_shape,
        compiler_params=_params("parallel", col_sem), name=name)(*args)
    return out if next_norm_w is not None else out[0]


def _in_proj(act, w, **kw):
    return _matmul(act[0], w, ssq=act[1], **kw)


def _out_proj(y, w_out, x2d, next_w, name):
    if next_w is None:
        return _matmul(y, w_out, resid=x2d, name=name), None
    x_new, xs, part = _matmul(y, w_out, resid=x2d, next_norm_w=next_w, name=name)
    return x_new, (xs, part)


def _mlstm_body(gb_ref, q_ref, k_ref, v_ref, og_ref, z_ref, gcol_ref, grow_ref, cwq_ref, cwk_ref,
                cbq_ref, cbk_ref, onw_ref, y_ref, c_scr, n_scr, m_scr, extq, extk, *, hb, dqk, dv, interleave):
    c = pl.program_id(2)
    L = q_ref.shape[0]

    @pl.when(c == 0)
    def _():
        c_scr[...] = jnp.zeros_like(c_scr)
        n_scr[...] = jnp.zeros_like(n_scr)
        m_scr[...] = jnp.zeros_like(m_scr)
        extq[L:L + 8, :] = jnp.zeros((8, extq.shape[1]), F32)
        extk[L:L + 8, :] = jnp.zeros((8, extk.shape[1]), F32)

    def conv_silu(ext, x_ref, cw_ref, cb_ref):
        ext[0:8, :] = ext[L:L + 8, :]
        ext[8:L + 8, :] = x_ref[...]
        acc = cb_ref[...] + cw_ref[0:1, :] * ext[pl.ds(8 - CONV_W + 1, L), :]
        for j in range(1, CONV_W):
            acc = acc + cw_ref[j:j + 1, :] * ext[pl.ds(8 - CONV_W + 1 + j, L), :]
        return _silu(acc)

    q_all = conv_silu(extq, q_ref, cwq_ref, cbq_ref) * (dqk ** -0.5)
    k_all = conv_silu(extk, k_ref, cwk_ref, cbk_ref)
    t_io = lax.broadcasted_iota(jnp.int32, (L, L), 0)
    s_io = lax.broadcasted_iota(jnp.int32, (L, L), 1)
    causal = s_io <= t_io

    def head_stages(hh):
        h = pl.program_id(1) * hb + hh
        qs, vs = slice(hh * dqk, (hh + 1) * dqk), slice(hh * dv, (hh + 1) * dv)
        q, k, v = q_all[:, qs], k_all[:, qs], v_ref[:, vs]
        bi = gb_ref[0, h]
        bf = gb_ref[1, h]
        gcol = gcol_ref[hh]
        grow = grow_ref[hh]
        li_col = gcol[:, 0:1] + bi
        li_row = grow[0:1, :] + bi
        lf_col = -_softplus(-(gcol[:, 1:2] + bf))
        lf_row = -_softplus(-(grow[1:2, :] + bf))

        b_col = jnp.sum(jnp.where(causal, lf_row, 0.0), axis=1, keepdims=True)
        b_row = jnp.sum(jnp.where(t_io <= s_io, lf_col, 0.0), axis=0, keepdims=True)
        b_last = jnp.sum(lf_row, axis=1, keepdims=True)
        m_st = m_scr[hh]

        dmat = jnp.where(causal, b_col - b_row + li_row, NEG)
        inter = b_col + m_st
        m_t = jnp.maximum(inter, jnp.max(dmat, axis=1, keepdims=True))
        w_intra = jnp.exp(dmat - m_t)
        w_inter = jnp.exp(inter - m_t)
        c_st = c_scr[hh]
        n_st = n_scr[hh]
        yield
        qk = _dot_nt(q, k)
        q_c = _dot(q, c_st)
        yield
        a = w_intra * qk
        num = _dot(a, v) + w_inter * q_c
        qn = jnp.sum(a, axis=1, keepdims=True) + w_inter * jnp.sum(q * n_st, axis=1, keepdims=True)
        yield
        hc = num / jnp.maximum(jnp.abs(qn), jnp.exp(-m_t))

        g_row = b_last - b_row + li_row
        g_col = b_last - b_col + li_col
        m_new = jnp.maximum(b_last + m_st, jnp.max(g_row, axis=1, keepdims=True))
        ws_col = jnp.exp(g_col - m_new)
        decay = jnp.exp(b_last + m_st - m_new)
        kw = k * ws_col
        c_scr[hh] = decay * c_st + _dot_tn(kw, v)
        n_scr[hh] = decay * n_st + jnp.sum(kw, axis=0, keepdims=True)
        m_scr[hh] = m_new
        yield
        hn = _head_rms(hc, onw_ref[:, vs])
        y_ref[:, vs] = (hn * _sigmoid(og_ref[:, vs]) * _silu(z_ref[:, vs])).astype(y_ref.dtype)
        yield

    for h0 in range(0, hb, interleave):
        for _ in zip(*[head_stages(hh) for hh in range(h0, min(h0 + interleave, hb))]):
            pass


def _mlstm_layer(x2d, hn, next_w, bsz, seq, w_in, conv_w, conv_b, gate_b, out_norm_w, w_out):
    t, d = x2d.shape
    heads = gate_b.shape[-1]
    qk = conv_w.shape[-1] // 2
    inner = out_norm_w.shape[-1]
    dqk, dv = qk // heads, inner // heads
    n_main = 2 * qk + 3 * inner
    L = A_CHUNK
    nc = seq // L
    hb = heads
    assert dqk % LANES == 0 and dv % LANES == 0 and 2 * heads <= LANES and heads % hb == 0

    u = _in_proj(hn, w_in, n=n_main, transposed=True, name="a_in").reshape(bsz, seq, n_main)
    gts = _in_proj(hn, w_in, n=LANES, col0=n_main, transposed=True, name="a_gates")
    gts = gts[:, :2 * heads].reshape(bsz, nc, L, 2, heads)
    g_col = jnp.transpose(gts, (0, 4, 1, 2, 3))
    g_row = jnp.transpose(gts, (0, 4, 1, 3, 2))

    wq, wv = hb * dqk, hb * dv
    assert qk % wq == 0 and 2 * qk % wv == 0 and inner % wv == 0
    kq0, kv0, ko0, kz0 = qk // wq, 2 * qk // wv, (2 * qk + inner) // wv, (2 * qk + 2 * inner) // wv
    blk = lambda w, off: pl.BlockSpec((None, L, w), lambda b, h, c: (b, c, off + h))
    par = lambda r, w, off: pl.BlockSpec((r, w), lambda b, h, c: (0, off + h))
    y = pl.pallas_call(
        functools.partial(_mlstm_body, hb=hb, dqk=dqk, dv=dv, interleave=4),
        grid=(bsz, heads // hb, nc),
        in_specs=[pl.BlockSpec(memory_space=pltpu.SMEM),
                  blk(wq, 0), blk(wq, kq0), blk(wv, kv0), blk(wv, ko0), blk(wv, kz0),
                  pl.BlockSpec((None, hb, None, L, 2), lambda b, h, c: (b, h, c, 0, 0)),
                  pl.BlockSpec((None, hb, None, 2, L), lambda b, h, c: (b, h, c, 0, 0)),
                  par(CONV_W, wq, 0), par(CONV_W, wq, kq0), par(1, wq, 0), par(1, wq, kq0),
                  par(1, wv, 0)],
        out_specs=pl.BlockSpec((None, L, wv), lambda b, h, c: (b, c, h)),
        out_shape=jax.ShapeDtypeStruct((bsz, seq, inner), BF16),
        scratch_shapes=[pltpu.VMEM((hb, dqk, dv), F32), pltpu.VMEM((hb, 1, dqk), F32), pltpu.VMEM((hb, 1, 1), F32),
                        pltpu.VMEM((L + 8, wq), F32), pltpu.VMEM((L + 8, wq), F32)],
        compiler_params=_params("parallel", "parallel", "arbitrary"), name="a_mlstm",
    )(gate_b, u, u, u, u, u, g_col, g_row, conv_w, conv_w, conv_b.reshape(1, -1), conv_b.reshape(1, -1),
      out_norm_w.reshape(1, -1))
    return _out_proj(y.reshape(t, inner), w_out, x2d, next_w, "a_out")


def _rglru_body(x_ref, z_ref, cw_ref, cb_ref, wa_ref, wx_ref, ba_ref, bx_ref, lam_ref, y_ref,
                ext, xc_scr, a_scr, b_scr, h_scr, *, starts, win, chunk, scan_chunk):
    s = pl.program_id(1)
    ts, width = x_ref.shape

    @pl.when(s == 0)
    def _():
        ext[ts:ts + 8, :] = jnp.zeros((8, width), F32)
        h_scr[...] = jnp.zeros_like(h_scr)

    ext[0:8, :] = ext[ts:ts + 8, :]
    ext[8:ts + 8, :] = x_ref[...]
    for c0 in range(0, width, chunk):
        cs = slice(c0, c0 + chunk)
        xc = cb_ref[:, cs] + cw_ref[0:1, cs] * ext[pl.ds(8 - CONV_W + 1, ts), cs]
        for j in range(1, CONV_W):
            xc = xc + cw_ref[j:j + 1, cs] * ext[pl.ds(8 - CONV_W + 1 + j, ts), cs]
        xc_scr[:, cs] = xc

    a_scr[...] = jnp.zeros_like(a_scr)
    b_scr[...] = jnp.zeros_like(b_scr)
    for n, a0 in enumerate(starts):
        ws = slice(a0, a0 + win)
        xw = xc_scr[:, ws].astype(BF16)
        a_scr[:, ws] += jnp.dot(xw, wa_ref[n], preferred_element_type=F32)
        b_scr[:, ws] += jnp.dot(xw, wx_ref[n], preferred_element_type=F32)

    for c0 in range(0, width, chunk):
        cs = slice(c0, c0 + chunk)
        r = _sigmoid(a_scr[:, cs] + ba_ref[:, cs])
        ig = _sigmoid(b_scr[:, cs] + bx_ref[:, cs])
        log_a = (-C_EXP) * _softplus(-lam_ref[:, cs]) * r
        a = jnp.exp(log_a)
        a_scr[:, cs] = a
        b_scr[:, cs] = jnp.sqrt(-jnp.tanh(log_a) * (a * a + 1.0)) * (ig * xc_scr[:, cs])

    for c0 in range(0, width, scan_chunk):
        cs = slice(c0, c0 + scan_chunk)

        def row(t, h, cs=cs):
            h = a_scr[pl.ds(t, 1), cs] * h + b_scr[pl.ds(t, 1), cs]
            b_scr[pl.ds(t, 1), cs] = h
            return h

        h_scr[:, cs] = lax.fori_loop(0, ts, row, h_scr[:, cs], unroll=8)

    y_ref[...] = (b_scr[...] * _silu(z_ref[...])).astype(y_ref.dtype)


def _rglru_layer(x2d, hn, next_w, bsz, seq, w_in, conv_w, conv_b, w_a, b_a, w_x, b_x, lam, w_out):
    t, d = x2d.shape
    width = conv_w.shape[-1]
    nblk = w_a.shape[0]
    blk = width // nblk
    assert width % LANES == 0
    win = LANES
    while True:
        starts = tuple(min(n * blk // LANES * LANES, width - win) for n in range(nblk))
        if all(n * blk - a0 + blk <= win for n, a0 in enumerate(starts)):
            break
        win += LANES

    def window_weights(w):
        return jnp.stack([jnp.pad(w[n], ((n * blk - a0, win - blk - (n * blk - a0)),) * 2)
                          for n, a0 in enumerate(starts)]).astype(BF16)

    u = _in_proj(hn, w_in, name="c_in").reshape(bsz, seq, 2 * width)
    ts = _pick(seq, (128, 64, 8))
    chunk = _pick(width, (768, 512, 384, 256, 128))
    scan_chunk = _pick(width, (1792, 1536, 1024, 896, 768, 512, 384, 256, 128))
    full = lambda r: pl.BlockSpec((r, width), lambda b, s: (0, 0))
    wspec = pl.BlockSpec((nblk, win, win), lambda b, s: (0, 0, 0), pipeline_mode=pl.Buffered(1))
    row = lambda v: v.reshape(1, width)
    y = pl.pallas_call(
        functools.partial(_rglru_body, starts=starts, win=win, chunk=chunk, scan_chunk=scan_chunk),
        grid=(bsz, seq // ts),
        in_specs=[pl.BlockSpec((None, ts, width), lambda b, s: (b, s, 0)),
                  pl.BlockSpec((None, ts, width), lambda b, s: (b, s, 1)),
                  full(CONV_W), full(1), wspec, wspec, full(1), full(1), full(1)],
        out_specs=pl.BlockSpec((None, ts, width), lambda b, s: (b, s, 0)),
        out_shape=jax.ShapeDtypeStruct((bsz, seq, width), BF16),
        scratch_shapes=[pltpu.VMEM((ts + 8, width), F32), pltpu.VMEM((ts, width), F32),
                        pltpu.VMEM((ts, width), F32), pltpu.VMEM((ts, width), F32), pltpu.VMEM((1, width), F32)],
        compiler_params=_params("parallel", "arbitrary"), name="c_rglru",
    )(u, u, conv_w, row(conv_b), window_weights(w_a), window_weights(w_x), row(b_a), row(b_x), row(lam))
    return _out_proj(y.reshape(t, width), w_out, x2d, next_w, "c_out")


def _dilated_body(sl_ref, *refs, patterns, tq, mix_rows, group):
    n_pat = len(patterns)
    qkv = [refs[5 * g:5 * g + 5] for g in range(n_pat)]
    z_ref, qw_ref, kw_ref, y_ref, knbuf, vbuf, obuf, lbuf = refs[5 * n_pat:]
    first_tile = pl.program_id(1) == 0
    slope = sl_ref[pl.program_id(2)]
    tt, hd = z_ref.shape
    scale = hd ** -0.5
    qi = lax.broadcasted_iota(jnp.int32, (tq, 2 * tq), 0)
    kj = lax.broadcasted_iota(jnp.int32, (tq, 2 * tq), 1)
    steps = qi + tq - kj

    for g, (window, dil) in enumerate(patterns):
        q_ref, kp_ref, kc_ref, vp_ref, vc_ref = qkv[g]
        halo = tq * dil
        knbuf[0:halo, :] = _head_rms(kp_ref[...], kw_ref[...])
        knbuf[halo:halo + tt, :] = _head_rms(kc_ref[...], kw_ref[...])
        vbuf[0:halo, :] = vp_ref[...]
        vbuf[halo:halo + tt, :] = vc_ref[...]
        bias = jnp.where((steps >= 0) & (steps <= window // dil), (-slope) * (steps * dil).astype(F32), NEG)
        bias_first = jnp.where((kj >= tq) | jnp.logical_not(first_tile), bias, NEG)
        subtiles = [(r + halo * j, j == 0) for r in range(dil) for j in range(tt // halo)]
        for i0 in range(0, len(subtiles), group):
            grp = subtiles[i0:i0 + group]
            rows = [(lambda n, st=st: pl.ds(st, n, stride=dil) if dil > 1 else pl.ds(st, n)) for st, _ in grp]
            ss = [_dot_nt(_head_rms(q_ref[rw(tq), :], qw_ref[...]) * scale, knbuf[rw(2 * tq), :])
                  + (bias_first if first else bias) for rw, (_, first) in zip(rows, grp)]
            ms = [jnp.max(s, axis=1, keepdims=True) for s in ss]
            ps = [jnp.exp(s - m) for s, m in zip(ss, ms)]
            dens = [jnp.sum(p, axis=1, keepdims=True) for p in ps]
            for rw, p, den, m in zip(rows, ps, dens, ms):
                obuf[g, rw(tq), :] = _dot(p / den, vbuf[rw(2 * tq), :])
                lbuf[g, rw(tq), :] = jnp.broadcast_to(m + jnp.log(den), (tq, hd))

    for c in range(0, tt, mix_rows):
        cs = slice(c, c + mix_rows)
        ls = [lbuf[g, cs, :] for g in range(n_pat)]
        mx = functools.reduce(jnp.maximum, ls)
        es = [jnp.exp(l - mx) for l in ls]
        tot = functools.reduce(lambda a, b: a + b, es)
        o = sum((e / tot) * obuf[g, cs, :] for g, e in enumerate(es))
        y_ref[cs, :] = (o * _silu(z_ref[cs, :])).astype(y_ref.dtype)


def _dilated_layer(x2d, hn, next_w, bsz, seq, w_in, q_norm_w, k_norm_w, w_out):
    t, d = x2d.shape
    hd = q_norm_w.shape[-1]
    inner = w_out.shape[0]
    heads = inner // hd
    n_pat = len(D_PATTERNS)
    cols = 3 * n_pat * inner + inner
    tq = D_QBLK
    max_dil = max(dil for _, dil in D_PATTERNS)
    tt = tq * max_dil
    assert hd == LANES and seq % tt == 0
    assert all(window // dil <= tq and tt % (tq * dil) == 0 for window, dil in D_PATTERNS)
    slopes = jnp.asarray(2.0 ** (-8.0 * np.arange(1, heads + 1) / heads), dtype=F32)
    u = _in_proj(hn, w_in, group=hd, name="d_in").reshape(cols // hd, bsz, seq, hd)

    in_specs = [pl.BlockSpec(memory_space=pltpu.SMEM)]
    args = [slopes]
    for g, (window, dil) in enumerate(D_PATTERNS):
        halo = tq * dil
        cur = lambda part, g=g: pl.BlockSpec((None, None, tt, hd),
                                             lambda b, s, h: ((3 * g + part) * heads + h, b, s, 0))
        prev = lambda part, g=g, halo=halo: pl.BlockSpec(
            (None, None, halo, hd),
            lambda b, s, h: ((3 * g + part) * heads + h, b, jnp.maximum(s * (tt // halo) - 1, 0), 0))
        in_specs += [cur(0), prev(1), cur(1), prev(2), cur(2)]
        args += [u] * 5
    wspec = pl.BlockSpec((1, hd), lambda b, s, h: (0, 0))
    in_specs += [pl.BlockSpec((None, None, tt, hd), lambda b, s, h: (3 * n_pat * heads + h, b, s, 0)), wspec, wspec]
    args += [u, q_norm_w.reshape(1, hd), k_norm_w.reshape(1, hd)]
    y = pl.pallas_call(
        functools.partial(_dilated_body, patterns=D_PATTERNS, tq=tq, mix_rows=_pick(tt, (256, 128)), group=4),
        grid=(bsz, seq // tt, heads), in_specs=in_specs,
        out_specs=pl.BlockSpec((None, tt, hd), lambda b, s, h: (b, s, h)),
        out_shape=jax.ShapeDtypeStruct((bsz, seq, inner), BF16),
        scratch_shapes=[pltpu.VMEM((2 * tt, hd), F32), pltpu.VMEM((2 * tt, hd), F32),
                        pltpu.VMEM((n_pat, tt, hd), F32), pltpu.VMEM((n_pat, tt, hd), F32)],
        compiler_params=_params("parallel", "parallel", "parallel"), name="d_attn",
    )(*args)
    return _out_proj(y.reshape(t, inner), w_out, x2d, next_w, "d_out")


def _nsa_prep_body(k0_ref, v0_ref, k1_ref, v1_ref, k2_ref, v2_ref, pe_ref, wk_ref, wv_ref, knw_ref,
                   kc_ref, vc_ref, ks_ref, vs_ref, kw_ref, vw_ref, *, stride):
    n_c = kc_ref.shape[0]

    def compress(x_ref, w_ref, pe):
        p1 = jnp.zeros(kc_ref.shape, F32)
        p2 = jnp.zeros(kc_ref.shape, F32)
        for l in range(stride):
            x = x_ref[pl.ds(l, n_c, stride=stride), :]
            p1 = p1 + _dot(x + pe[l:l + 1, :], w_ref[l])
            p2 = p2 + _dot(x + pe[stride + l:stride + l + 1, :], w_ref[stride + l])
        return p1 + pltpu.roll(p2, n_c - 1, 0)

    kc_ref[...] = _head_rms(compress(k0_ref, wk_ref, pe_ref[0]), knw_ref[0:1, :])
    vc_ref[...] = compress(v0_ref, wv_ref, pe_ref[1])

    seq, hd = k1_ref.shape
    pos = lax.broadcasted_iota(jnp.int32, (seq, hd), 0)
    lane = lax.broadcasted_iota(jnp.int32, (seq, hd), 1)
    split_shift = NSA_POS_SPLIT.bit_length() - 1
    pos_hi = ((pos >> split_shift) << split_shift).astype(F32)
    pos_lo = (pos & (NSA_POS_SPLIT - 1)).astype(F32)
    pos_cols = jnp.where(lane < 3, pos_hi, jnp.where(lane < NSA_SEL_OFF, pos_lo, 0.0))
    sel_blk = pos >> (B_SEL_LEN.bit_length() - 1)
    onehot = jnp.where(lane - NSA_SEL_OFF == sel_blk, NEG, 0.0)
    ones = jnp.ones((seq, hd), vs_ref.dtype)
    ks_ref[:, 0:hd] = _head_rms(k1_ref[...], knw_ref[1:2, :]).astype(ks_ref.dtype)
    ks_ref[:, hd:2 * hd] = (pos_cols + onehot).astype(ks_ref.dtype)
    vs_ref[:, 0:hd] = v1_ref[...].astype(vs_ref.dtype)
    vs_ref[:, hd:2 * hd] = ones
    kw_ref[:, 0:hd] = _head_rms(k2_ref[...], knw_ref[2:3, :]).astype(kw_ref.dtype)
    kw_ref[:, hd:2 * hd] = pos_cols.astype(kw_ref.dtype)
    vw_ref[:, 0:hd] = v2_ref[...].astype(vw_ref.dtype)
    vw_ref[:, hd:2 * hd] = ones


def _nsa_body(sl_ref, sp_ref, q_ref, z_ref, g_ref, qw_ref, kc_ref, vc_ref, ks_ref, vs_ref, kw_ref, vw_ref, ov_ref,
              y_ref, q2, acc, m_scr, o_scr, used, *, rep, hd, n_top, n_blk, kb_sel):
    g = pl.program_id(1)
    i = pl.program_id(2)
    tq = q_ref.shape[0]
    n_cmp = kc_ref.shape[0]
    t0 = i * tq
    scale = hd ** -0.5
    rows = lambda r: slice(r * tq, (r + 1) * tq)

    for r in range(rep):
        q2[rows(r), 0:hd] = (_head_rms(q_ref[:, r * hd:(r + 1) * hd], qw_ref[...]) * scale).astype(q2.dtype)
    gts = _sigmoid(g_ref[...])

    def gate(r, branch):
        return gts[:, 3 * r + branch:3 * r + branch + 1]

    t_col = t0 + lax.broadcasted_iota(jnp.int32, (tq, 1), 0)
    lane = lax.broadcasted_iota(jnp.int32, (tq, hd), 1)
    pieces = []
    for r in range(rep):
        pieces.append(jnp.where((lane == 0) | (lane == 3), sp_ref[g, 3 * r],
                                jnp.where((lane == 1) | (lane == 4), sp_ref[g, 3 * r + 1],
                                          jnp.where((lane == 2) | (lane == 5), sp_ref[g, 3 * r + 2], 0.0))))
        q2[rows(r), hd:2 * hd] = pieces[r].astype(q2.dtype)

    n_w = B_WIN + tq
    w_start = pl.multiple_of(jnp.maximum(t0 - B_WIN, 0), tq)
    w_dist = t_col - (w_start + lax.broadcasted_iota(jnp.int32, (tq, n_w), 1))
    w_bias = jnp.where((w_dist >= 0) & (w_dist < B_WIN), 0.0, NEG)
    kblk = kw_ref[pl.ds(w_start, n_w), :]
    vblk = vw_ref[pl.ds(w_start, n_w), :]
    s_all = lax.dot_general(q2[...], kblk, (((1,), (1,)), ((), ())), preferred_element_type=F32)
    half = rep // 2
    for r0 in (0, half):
        ss = [s_all[rows(r), :] + w_bias for r in range(r0, r0 + half)]
        ms = [jnp.max(s, axis=1, keepdims=True) for s in ss]
        ps = [jnp.exp(s - m).astype(BF16) for s, m in zip(ss, ms)]
        for r, p in zip(range(r0, r0 + half), ps):
            res = jnp.dot(p, vblk, preferred_element_type=F32)
            o_scr[rows(r), :] = gate(r, 2) * (res[:, 0:hd] / res[:, hd:2 * hd])

    cmp_end = lax.broadcasted_iota(jnp.int32, (tq, n_cmp), 1) * B_CMP_STRIDE + (B_CMP_LEN - 1)
    dist_ci = t_col - cmp_end
    mask_c = dist_ci >= 0
    dist_c = dist_ci.astype(F32)
    s_all = _dot_nt(q2[:, 0:hd], kc_ref[...])
    ss = [jnp.where(mask_c, s_all[rows(r), :] - sl_ref[g, r] * dist_c, NEG) for r in range(rep)]
    ms = [jnp.max(s, axis=1, keepdims=True) for s in ss]
    es = [jnp.where(mask_c, jnp.exp(s - m), 0.0) for s, m in zip(ss, ms)]
    dens = [jnp.sum(e, axis=1, keepdims=True) for e in es]
    ps = [e / jnp.where(den > 0, den, 1.0) for e, den in zip(es, dens)]
    psum = functools.reduce(lambda a, b: a + b, ps)
    o_c = _dot(jnp.concatenate([p.astype(BF16) for p in ps], axis=0), vc_ref[...])
    for r in range(rep):
        o_scr[rows(r), :] += gate(r, 0) * o_c[rows(r), :]

    imp = jnp.dot(psum, ov_ref[...], preferred_element_type=F32, precision=lax.Precision.HIGHEST)
    blk = lane - NSA_SEL_OFF
    in_range = (blk >= 0) & (blk < n_blk)
    cur = t_col >> (B_SEL_LEN.bit_length() - 1)
    forced = (blk == 0) | (blk == cur) | (blk == cur - 1)
    imp = jnp.where(forced, imp + FORCE, imp)
    imp = jnp.where(in_range & (blk * B_SEL_LEN <= t_col), imp, NEG)
    imp_t = imp.T
    n_grp = -(-(NSA_SEL_OFF + n_blk) // SUBLANES)
    grp = [imp_t[SUBLANES * k:SUBLANES * (k + 1), :] for k in range(n_grp)]
    cnt = [jnp.zeros((SUBLANES, tq), F32) for _ in range(n_grp)]
    sub = lax.broadcasted_iota(jnp.int32, (SUBLANES, tq), 0)
    for mb in range(n_blk):
        slot = NSA_SEL_OFF + mb
        row = imp_t[slot:slot + 1, :]
        for k in range(n_grp):
            if SUBLANES * k > slot:
                beats = row >= grp[k]
            elif SUBLANES * (k + 1) <= slot:
                beats = row > grp[k]
            else:
                beats = (row > grp[k]) | ((row == grp[k]) & (sub > slot - SUBLANES * k))
            cnt[k] = cnt[k] + jnp.where(beats, 1.0, 0.0)
    cnt_t = jnp.concatenate(cnt + [jnp.zeros((hd - SUBLANES * n_grp, tq), F32)], axis=0)
    not_sel = jnp.where((cnt_t < n_top) & (imp_t > NEG / 2), 0.0, 1.0).T

    sel_bf = jnp.where(in_range, 1.0 - not_sel, 0.0).astype(BF16)
    per_slot = jnp.dot(jnp.ones((SUBLANES, tq), BF16), sel_bf, preferred_element_type=F32)
    slot_io = lax.broadcasted_iota(jnp.int32, (hd, hd), 0) - NSA_SEL_OFF
    kb_io = lax.broadcasted_iota(jnp.int32, (hd, hd), 1)
    kb_shift = (kb_sel // B_SEL_LEN).bit_length() - 1
    group = jnp.where((slot_io >= 0) & ((slot_io >> kb_shift) == kb_io), 1.0, 0.0).astype(BF16)
    per_kb = jnp.dot(per_slot.astype(BF16), group, preferred_element_type=F32)
    lane_s = lax.broadcasted_iota(jnp.int32, (SUBLANES, hd), 1)
    for kb in range(used.shape[0]):
        used[kb] = jnp.max(jnp.where(lane_s == kb, per_kb, 0.0)).astype(jnp.int32)

    for r in range(rep):
        q2[rows(r), hd:2 * hd] = jnp.where(in_range, not_sel, pieces[r]).astype(q2.dtype)

    acc[...] = jnp.zeros_like(acc)
    m_scr[...] = jnp.full(m_scr.shape, NEG, F32)
    n_chunk = kb_sel // hd

    def sel_block(start, bias):
        kblk = ks_ref[pl.ds(start, kb_sel), :]
        vblk = vs_ref[pl.ds(start, kb_sel), :]
        s_all = lax.dot_general(q2[...], kblk, (((1,), (1,)), ((), ())), preferred_element_type=F32)
        ss = [s_all[rows(r), :] if bias is None else s_all[rows(r), :] + bias for r in range(rep)]
        mxs = [functools.reduce(jnp.maximum, [s[:, c * hd:(c + 1) * hd] for c in range(n_chunk)]) for s in ss]
        m_olds = [m_scr[rows(r), :] for r in range(rep)]
        m_news = [jnp.maximum(m_old, jnp.max(mx, axis=1, keepdims=True))
                  for m_old, mx in zip(m_olds, mxs)]
        alphas = [jnp.exp(m_old - m_new) for m_old, m_new in zip(m_olds, m_news)]
        ps = [jnp.exp(s - jnp.concatenate([m_new] * n_chunk, axis=1)).astype(BF16) for s, m_new in zip(ss, m_news)]
        for r in range(rep):
            acc[rows(r), :] = (jnp.concatenate([alphas[r], alphas[r]], axis=1) * acc[rows(r), :]
                               + jnp.dot(ps[r], vblk, preferred_element_type=F32))
            m_scr[rows(r), :] = m_news[r]

    def sel_step(kb, carry):
        @pl.when(used[kb] > 0)
        def _():
            sel_block(pl.multiple_of(kb * kb_sel, kb_sel), None)

        return carry

    n_full = lax.div(t0, kb_sel)
    lax.fori_loop(0, n_full, sel_step, 0)
    d_start = pl.multiple_of(n_full * kb_sel, kb_sel)
    d_pos = d_start + lax.broadcasted_iota(jnp.int32, (tq, kb_sel), 1)
    sel_block(d_start, jnp.where(d_pos <= t_col, 0.0, NEG))
    for r in range(rep):
        o_scr[rows(r), :] += gate(r, 1) * (acc[rows(r), 0:hd] / acc[rows(r), hd:2 * hd])

    for r in range(rep):
        hs = slice(r * hd, (r + 1) * hd)
        y_ref[:, hs] = (o_scr[rows(r), :] * _silu(z_ref[:, hs])).astype(y_ref.dtype)


def _nsa_layer(x2d, hn, next_w, bsz, seq, w_in, cmp_pe, cmp_wk, cmp_wv, q_norm_w, k_norm_w, w_out):
    t, d = x2d.shape
    hd = q_norm_w.shape[-1]
    inner = w_out.shape[0]
    heads = inner // hd
    rep = heads // B_KV
    kvw = B_KV * hd
    n_main = 2 * inner + 6 * kvw
    assert B_CMP_LEN == 2 * B_CMP_STRIDE and seq % B_CMP_STRIDE == 0 and 3 * heads <= LANES
    n_cmp = seq // B_CMP_STRIDE
    n_blk = seq // B_SEL_LEN
    n_top = min(B_N_SEL, n_blk)
    tq = 256
    kb_sel = 512
    assert hd == LANES and NSA_SEL_OFF + n_blk <= hd and NSA_POS_SPLIT == hd
    assert kb_sel % tq == 0 and seq % kb_sel == 0 and B_WIN % tq == 0 and seq >= B_WIN + tq

    u = _in_proj(hn, w_in, n=n_main, transposed=True, name="b_in").reshape(bsz, seq, n_main)
    gts = _in_proj(hn, w_in, n=LANES, col0=n_main, transposed=True, name="b_gates")
    gts = gts[:, :3 * heads].reshape(bsz, seq, B_KV, 3 * rep)
    gts = jnp.transpose(gts, (0, 2, 1, 3))

    kv0 = 2 * inner // hd
    kvspec = lambda br, kv: pl.BlockSpec((None, seq, hd), lambda b, g: (b, 0, kv0 + (br * 2 + kv) * B_KV + g))
    cspec = pl.BlockSpec((None, None, n_cmp, hd), lambda b, g: (b, g, 0, 0))
    sspec = pl.BlockSpec((None, None, seq, 2 * hd), lambda b, g: (b, g, 0, 0))
    wspec = pl.BlockSpec((B_CMP_LEN, hd, hd), lambda b, g: (0, 0, 0))
    c_sh = jax.ShapeDtypeStruct((bsz, B_KV, n_cmp, hd), F32)
    s_sh = jax.ShapeDtypeStruct((bsz, B_KV, seq, 2 * hd), BF16)
    kc, vc, ks, vs, kw, vw = pl.pallas_call(
        functools.partial(_nsa_prep_body, stride=B_CMP_STRIDE),
        grid=(bsz, B_KV),
        in_specs=[kvspec(0, 0), kvspec(0, 1), kvspec(1, 0), kvspec(1, 1), kvspec(2, 0), kvspec(2, 1),
                  pl.BlockSpec((2, B_CMP_LEN, hd), lambda b, g: (0, 0, 0)), wspec, wspec,
                  pl.BlockSpec((3, hd), lambda b, g: (0, 0))],
        out_specs=[cspec, cspec, sspec, sspec, sspec, sspec],
        out_shape=[c_sh, c_sh, s_sh, s_sh, s_sh, s_sh],
        compiler_params=_params("parallel", "parallel"), name="b_prep",
    )(u, u, u, u, u, u, cmp_pe, cmp_wk.reshape(B_CMP_LEN, hd, hd).astype(BF16),
      cmp_wv.reshape(B_CMP_LEN, hd, hd).astype(BF16), k_norm_w)

    cmp_start = np.arange(n_cmp) * B_CMP_STRIDE
    sel_start = np.arange(n_blk) * B_SEL_LEN
    ov = np.clip(np.minimum(cmp_start[:, None] + B_CMP_LEN, sel_start[None, :] + B_SEL_LEN)
                 - np.maximum(cmp_start[:, None], sel_start[None, :]), 0, None) / B_CMP_LEN
    ov[cmp_start + B_CMP_LEN > seq] = 0.0
    ov = np.pad(ov, ((0, 0), (NSA_SEL_OFF, hd - NSA_SEL_OFF - n_blk)))
    slopes_np = (2.0 ** (-8.0 * np.arange(1, heads + 1) / heads)).astype(np.float32)
    slopes = jnp.asarray(slopes_np).reshape(B_KV, rep)
    to_bf16 = lambda v: v.astype(ml_dtypes.bfloat16).astype(np.float32)
    s_hi = to_bf16(slopes_np)
    s_mid = to_bf16(slopes_np - s_hi)
    s_lo = to_bf16(slopes_np - s_hi - s_mid)
    pieces = jnp.asarray(np.stack([s_hi, s_mid, s_lo], axis=-1).reshape(B_KV, 3 * rep))

    rw = rep * hd
    cspec3 = pl.BlockSpec((None, None, n_cmp, hd), lambda b, g, i: (b, g, 0, 0))
    sspec3 = pl.BlockSpec((None, None, seq, 2 * hd), lambda b, g, i: (b, g, 0, 0))
    y = pl.pallas_call(
        functools.partial(_nsa_body, rep=rep, hd=hd, n_top=n_top, n_blk=n_blk, kb_sel=kb_sel),
        grid=(bsz, B_KV, seq // tq),
        in_specs=[pl.BlockSpec(memory_space=pltpu.SMEM), pl.BlockSpec(memory_space=pltpu.SMEM),
                  pl.BlockSpec((None, tq, rw), lambda b, g, i: (b, i, g)),
                  pl.BlockSpec((None, tq, rw), lambda b, g, i: (b, i, inner // rw + g)),
                  pl.BlockSpec((None, None, tq, 3 * rep), lambda b, g, i: (b, g, i, 0)),
                  pl.BlockSpec((1, hd), lambda b, g, i: (0, 0)),
                  cspec3, cspec3, sspec3, sspec3, sspec3, sspec3,
                  pl.BlockSpec((n_cmp, hd), lambda b, g, i: (0, 0))],
        out_specs=pl.BlockSpec((None, tq, rw), lambda b, g, i: (b, i, g)),
        out_shape=jax.ShapeDtypeStruct((bsz, seq, inner), BF16),
        scratch_shapes=[pltpu.VMEM((rep * tq, 2 * hd), BF16), pltpu.VMEM((rep * tq, 2 * hd), F32),
                        pltpu.VMEM((rep * tq, hd), F32), pltpu.VMEM((rep * tq, hd), F32),
                        pltpu.SMEM((seq // kb_sel,), jnp.int32)],
        compiler_params=_params("parallel", "parallel", "arbitrary"), name="b_attn",
    )(slopes, pieces, u, u, gts, q_norm_w.reshape(1, hd), kc, vc, ks, vs, kw, vw, jnp.asarray(ov, dtype=F32))
    return _out_proj(y.reshape(t, inner), w_out, x2d, next_w, "b_out")


def kernel(x, norm_w, a_w_in, a_conv_w, a_conv_b, a_gate_b, a_out_norm_w, a_w_out, b_w_in, b_cmp_pe, b_cmp_wk, b_cmp_wv, b_q_norm_w, b_k_norm_w, b_w_out, c_w_in, c_conv_w, c_conv_b, c_w_a, c_b_a, c_w_x, c_b_x, c_lambda, c_w_out, d_w_in, d_q_norm_w, d_k_norm_w, d_w_out):
    bsz, seq, d = x.shape
    depth = norm_w.shape[0]
    x2d = x.reshape(bsz * seq, d)
    hn = (_rmsnorm(x2d, norm_w[0]), None)
    for layer in range(depth):
        kind, j = layer % 4, layer // 4
        next_w = norm_w[layer + 1] if layer + 1 < depth else None
        if kind == 0:
            x2d, hn = _mlstm_layer(x2d, hn, next_w, bsz, seq, a_w_in[j], a_conv_w[j], a_conv_b[j], a_gate_b[j],
                                   a_out_norm_w[j], a_w_out[j])
        elif kind == 1:
            x2d, hn = _nsa_layer(x2d, hn, next_w, bsz, seq, b_w_in[j], b_cmp_pe[j], b_cmp_wk[j], b_cmp_wv[j],
                                 b_q_norm_w[j], b_k_norm_w[j], b_w_out[j])
        elif kind == 2:
            x2d, hn = _rglru_layer(x2d, hn, next_w, bsz, seq, c_w_in[j], c_conv_w[j], c_conv_b[j], c_w_a[j], c_b_a[j],
                                   c_w_x[j], c_b_x[j], c_lambda[j], c_w_out[j])
        else:
            x2d, hn = _dilated_layer(x2d, hn, next_w, bsz, seq, d_w_in[j], d_q_norm_w[j], d_k_norm_w[j], d_w_out[j])
    return x2d.reshape(bsz, seq, d)
```

```python
import functools

import ml_dtypes
import numpy as np
import jax
import jax.numpy as jnp
from jax import lax
from jax.experimental import pallas as pl
from jax.experimental.pallas import tpu as pltpu

F32 = jnp.float32
BF16 = jnp.bfloat16

EPS = 1e-6
NEG = -1e30
FORCE = 1e4
CONV_W = 4

LANES = 128
SUBLANES = 8
CONV_HALO = SUBLANES
assert CONV_HALO >= CONV_W - 1
V7X_VMEM_LIMIT = 56 * 1024 * 1024

A_CHUNK = 128
B_CMP_LEN = 32
B_CMP_STRIDE = 16
B_SEL_LEN = 64
B_N_SEL = 16
B_WIN = 512
B_KV = 4
NSA_POS_SPLIT = 128
NSA_SEL_OFF = 6
C_EXP = 8.0
D_PATTERNS = ((128, 1), (512, 4), (2048, 16))
D_QBLK = 128


def _params(*sem):
    return pltpu.CompilerParams(dimension_semantics=sem, vmem_limit_bytes=V7X_VMEM_LIMIT)


def _sigmoid(x):
    return 0.5 * jnp.tanh(0.5 * x) + 0.5


def _silu(x):
    return x * _sigmoid(x)


def _softplus(x):
    return jnp.maximum(x, 0.0) + jnp.log1p(jnp.exp(-jnp.abs(x)))


def _head_rms(x, w):
    return x * lax.rsqrt(jnp.mean(x * x, axis=-1, keepdims=True) + EPS) * w


def _dot(a, b):
    return jnp.dot(a.astype(BF16), b.astype(BF16), preferred_element_type=F32)


def _dot_nt(a, b):
    return lax.dot_general(a.astype(BF16), b.astype(BF16), (((1,), (1,)), ((), ())),
                           preferred_element_type=F32)


def _dot_tn(a, b):
    return lax.dot_general(a.astype(BF16), b.astype(BF16), (((0,), (0,)), ((), ())),
                           preferred_element_type=F32)


def _pick(n, cands):
    for c in cands:
        if n % c == 0:
            return c
    raise ValueError(f"no tile for {n}")


def _rmsnorm_body(x_ref, w_ref, o_ref):
    x = x_ref[...]
    o_ref[...] = _head_rms(x, w_ref[...]).astype(o_ref.dtype)


def _rmsnorm(x2d, w):
    t, d = x2d.shape
    tm = _pick(t, (256, 128, 8))
    return pl.pallas_call(
        _rmsnorm_body, grid=(t // tm,),
        in_specs=[pl.BlockSpec((tm, d), lambda i: (i, 0)), pl.BlockSpec((1, d), lambda i: (0, 0))],
        out_specs=pl.BlockSpec((tm, d), lambda i: (i, 0)),
        out_shape=jax.ShapeDtypeStruct((t, d), BF16),
        compiler_params=_params("parallel"), name="rmsnorm")(x2d, w.reshape(1, d))


def _mm_body(*refs, transposed, has_resid, has_ssq, emit_norm, k_dim, n_valid):
    refs = list(refs)
    a_ref, w_ref = refs.pop(0), refs.pop(0)
    r_ref = refs.pop(0) if has_resid else None
    ssq_ref = refs.pop(0) if has_ssq else None
    nw_ref = refs.pop(0) if emit_norm else None
    o_ref = refs.pop(0)
    w = w_ref[...].astype(a_ref.dtype)
    if transposed:
        acc = lax.dot_general(a_ref[...], w, (((1,), (1,)), ((), ())), preferred_element_type=F32)
    else:
        acc = jnp.dot(a_ref[...], w, preferred_element_type=F32)
    if n_valid is not None:
        col = pl.program_id(1) * acc.shape[1] + lax.broadcasted_iota(jnp.int32, acc.shape, 1)
        acc = jnp.where(col < n_valid, acc, 0.0)
    if has_ssq:
        acc = acc * lax.rsqrt(ssq_ref[...] * (1.0 / k_dim) + EPS)
    if has_resid:
        acc = r_ref[...] + acc
    if emit_norm:
        xs_ref, ssq_out_ref = refs
        xs_ref[...] = (acc * nw_ref[...]).astype(xs_ref.dtype)
        part = jnp.sum(acc * acc, axis=1, keepdims=True)

        @pl.when(pl.program_id(1) == 0)
        def _():
            ssq_out_ref[...] = part

        @pl.when(pl.program_id(1) > 0)
        def _():
            ssq_out_ref[...] += part
    if len(o_ref.shape) == 2:
        o_ref[...] = acc
    else:
        gw = o_ref.shape[2]
        for c in range(o_ref.shape[0]):
            o_ref[c] = acc[:, c * gw:(c + 1) * gw]


def _matmul(a, w, n=None, col0=0, resid=None, transposed=False, group=None, ssq=None, next_norm_w=None,
            name="matmul"):
    m, k = a.shape
    w_cols = w.shape[1]
    n = w_cols if n is None else n
    w_bytes = jnp.dtype(w.dtype).itemsize
    tm = _pick(m, (1024, 512, 256, 128))
    out_bytes = (8 if resid is not None else 4) + (2 if next_norm_w is not None else 0)
    col_blocks = (1 if ssq is not None else 0) + (1 if next_norm_w is not None else 0)
    tn = _pick(n, tuple(c for c in (1024, 768, 512, 384, 256, 128)
                        if 2 * (k * c * w_bytes + tm * k * 2 + tm * c * out_bytes + col_blocks * tm * LANES * 4)
                        + k * c * 2 <= V7X_VMEM_LIMIT - (4 << 20)))
    assert col0 % tn == 0
    cb = col0 // tn
    spec = pl.BlockSpec
    if transposed:
        w = jnp.swapaxes(w, 0, 1)
        w_spec = spec((tn, k), lambda i, j: (cb + j, 0))
    else:
        w_spec = spec((k, tn), lambda i, j: (0, cb + j))
    in_specs = [spec((tm, k), lambda i, j: (i, 0)), w_spec]
    args = [a, w]
    if resid is not None:
        in_specs.append(spec((tm, tn), lambda i, j: (i, j)))
        args.append(resid)
    if ssq is not None:
        in_specs.append(spec((tm, 1), lambda i, j: (i, 0)))
        args.append(ssq)
    if next_norm_w is not None:
        in_specs.append(spec((1, tn), lambda i, j: (0, j)))
        args.append(next_norm_w.reshape(1, n))
    if group is None:
        out_specs = [spec((tm, tn), lambda i, j: (i, j))]
        out_shape = [jax.ShapeDtypeStruct((m, n), F32)]
    else:
        assert tn % group == 0 and group % LANES == 0 and resid is None
        out_specs = [spec((tn // group, tm, group), lambda i, j: (j, i, 0))]
        out_shape = [jax.ShapeDtypeStruct((n // group, m, group), F32)]
    if next_norm_w is not None:
        out_specs += [spec((tm, tn), lambda i, j: (i, j)), spec((tm, 1), lambda i, j: (i, 0))]
        out_shape += [jax.ShapeDtypeStruct((m, n), BF16), jax.ShapeDtypeStruct((m, 1), F32)]
    body = functools.partial(_mm_body, transposed=transposed, has_resid=resid is not None, has_ssq=ssq is not None,
                             emit_norm=next_norm_w is not None, k_dim=k,
                             n_valid=w_cols - col0 if w_cols - col0 < n else None)
    col_sem = "arbitrary" if next_norm_w is not None else "parallel"
    out = pl.pallas_call(
        body, grid=(m // tm, n // tn), in_specs=in_specs, out_specs=out_specs, out_shape=out_shape,
        compiler_params=_params("parallel", col_sem), name=name)(*args)
    return out if next_norm_w is not None else out[0]


def _in_proj(act, w, **kw):
    return _matmul(act[0], w, ssq=act[1], **kw)


def _out_proj(y, w_out, x2d, next_w, name):
    if next_w is None:
        return _matmul(y, w_out, resid=x2d, name=name), None
    x_new, xs, part = _matmul(y, w_out, resid=x2d, next_norm_w=next_w, name=name)
    return x_new, (xs, part)


def _mlstm_body(gb_ref, q_ref, k_ref, v_ref, og_ref, z_ref, gcol_ref, grow_ref, cwq_ref, cwk_ref,
                cbq_ref, cbk_ref, onw_ref, y_ref, c_scr, n_scr, m_scr, extq, extk, *, hb, dqk, dv, interleave):
    c = pl.program_id(2)
    L = q_ref.shape[0]

    @pl.when(c == 0)
    def _():
        c_scr[...] = jnp.zeros_like(c_scr)
        n_scr[...] = jnp.zeros_like(n_scr)
        m_scr[...] = jnp.zeros_like(m_scr)
        extq[L:L + CONV_HALO, :] = jnp.zeros((CONV_HALO, extq.shape[1]), F32)
        extk[L:L + CONV_HALO, :] = jnp.zeros((CONV_HALO, extk.shape[1]), F32)

    def conv_silu(ext, x_ref, cw_ref, cb_ref):
        ext[0:CONV_HALO, :] = ext[L:L + CONV_HALO, :]
        ext[CONV_HALO:L + CONV_HALO, :] = x_ref[...]
        acc = cb_ref[...] + cw_ref[0:1, :] * ext[pl.ds(CONV_HALO - CONV_W + 1, L), :]
        for j in range(1, CONV_W):
            acc = acc + cw_ref[j:j + 1, :] * ext[pl.ds(CONV_HALO - CONV_W + 1 + j, L), :]
        return _silu(acc)

    q_all = conv_silu(extq, q_ref, cwq_ref, cbq_ref) * (dqk ** -0.5)
    k_all = conv_silu(extk, k_ref, cwk_ref, cbk_ref)
    t_io = lax.broadcasted_iota(jnp.int32, (L, L), 0)
    s_io = lax.broadcasted_iota(jnp.int32, (L, L), 1)
    causal = s_io <= t_io

    def head_stages(hh):
        h = pl.program_id(1) * hb + hh
        qs, vs = slice(hh * dqk, (hh + 1) * dqk), slice(hh * dv, (hh + 1) * dv)
        q, k, v = q_all[:, qs], k_all[:, qs], v_ref[:, vs]
        bi = gb_ref[0, h]
        bf = gb_ref[1, h]
        gcol = gcol_ref[hh]
        grow = grow_ref[hh]
        li_col = gcol[:, 0:1] + bi
        li_row = grow[0:1, :] + bi
        lf_col = -_softplus(-(gcol[:, 1:2] + bf))
        lf_row = -_softplus(-(grow[1:2, :] + bf))

        b_col = jnp.sum(jnp.where(causal, lf_row, 0.0), axis=1, keepdims=True)
        b_row = jnp.sum(jnp.where(t_io <= s_io, lf_col, 0.0), axis=0, keepdims=True)
        b_last = jnp.sum(lf_row, axis=1, keepdims=True)
        m_st = m_scr[hh]

        dmat = jnp.where(causal, b_col - b_row + li_row, NEG)
        inter = b_col + m_st
        m_t = jnp.maximum(inter, jnp.max(dmat, axis=1, keepdims=True))
        w_intra = jnp.exp(dmat - m_t)
        w_inter = jnp.exp(inter - m_t)
        c_st = c_scr[hh]
        n_st = n_scr[hh]
        yield
        qk = _dot_nt(q, k)
        q_c = _dot(q, c_st)
        yield
        a = w_intra * qk
        num = _dot(a, v) + w_inter * q_c
        qn = jnp.sum(a, axis=1, keepdims=True) + w_inter * jnp.sum(q * n_st, axis=1, keepdims=True)
        yield
        hc = num / jnp.maximum(jnp.abs(qn), jnp.exp(-m_t))

        g_row = b_last - b_row + li_row
        g_col = b_last - b_col + li_col
        m_new = jnp.maximum(b_last + m_st, jnp.max(g_row, axis=1, keepdims=True))
        ws_col = jnp.exp(g_col - m_new)
        decay = jnp.exp(b_last + m_st - m_new)
        kw = k * ws_col
        c_scr[hh] = decay * c_st + _dot_tn(kw, v)
        n_scr[hh] = decay * n_st + jnp.sum(kw, axis=0, keepdims=True)
        m_scr[hh] = m_new
        yield
        hn = _head_rms(hc, onw_ref[:, vs])
        y_ref[:, vs] = (hn * _sigmoid(og_ref[:, vs]) * _silu(z_ref[:, vs])).astype(y_ref.dtype)
        yield

    for h0 in range(0, hb, interleave):
        for _ in zip(*[head_stages(hh) for hh in range(h0, min(h0 + interleave, hb))]):
            pass


def _mlstm_layer(x2d, hn, next_w, bsz, seq, w_in, conv_w, conv_b, gate_b, out_norm_w, w_out):
    t, d = x2d.shape
    heads = gate_b.shape[-1]
    qk = conv_w.shape[-1] // 2
    inner = out_norm_w.shape[-1]
    dqk, dv = qk // heads, inner // heads
    n_main = 2 * qk + 3 * inner
    L = A_CHUNK
    nc = seq // L
    hb = heads
    assert dqk % LANES == 0 and dv % LANES == 0 and 2 * heads <= LANES and heads % hb == 0

    u = _in_proj(hn, w_in, n=n_main, transposed=True, name="a_in").reshape(bsz, seq, n_main)
    gts = _in_proj(hn, w_in, n=LANES, col0=n_main, transposed=True, name="a_gates")
    gts = gts[:, :2 * heads].reshape(bsz, nc, L, 2, heads)
    g_col = jnp.transpose(gts, (0, 4, 1, 2, 3))
    g_row = jnp.transpose(gts, (0, 4, 1, 3, 2))

    wq, wv = hb * dqk, hb * dv
    assert qk % wq == 0 and 2 * qk % wv == 0 and inner % wv == 0
    kq0, kv0, ko0, kz0 = qk // wq, 2 * qk // wv, (2 * qk + inner) // wv, (2 * qk + 2 * inner) // wv
    blk = lambda w, off: pl.BlockSpec((None, L, w), lambda b, h, c: (b, c, off + h))
    par = lambda r, w, off: pl.BlockSpec((r, w), lambda b, h, c: (0, off + h))
    y = pl.pallas_call(
        functools.partial(_mlstm_body, hb=hb, dqk=dqk, dv=dv, interleave=4),
        grid=(bsz, heads // hb, nc),
        in_specs=[pl.BlockSpec(memory_space=pltpu.SMEM),
                  blk(wq, 0), blk(wq, kq0), blk(wv, kv0), blk(wv, ko0), blk(wv, kz0),
                  pl.BlockSpec((None, hb, None, L, 2), lambda b, h, c: (b, h, c, 0, 0)),
                  pl.BlockSpec((None, hb, None, 2, L), lambda b, h, c: (b, h, c, 0, 0)),
                  par(CONV_W, wq, 0), par(CONV_W, wq, kq0), par(1, wq, 0), par(1, wq, kq0),
                  par(1, wv, 0)],
        out_specs=pl.BlockSpec((None, L, wv), lambda b, h, c: (b, c, h)),
        out_shape=jax.ShapeDtypeStruct((bsz, seq, inner), BF16),
        scratch_shapes=[pltpu.VMEM((hb, dqk, dv), F32), pltpu.VMEM((hb, 1, dqk), F32), pltpu.VMEM((hb, 1, 1), F32),
                        pltpu.VMEM((L + CONV_HALO, wq), F32), pltpu.VMEM((L + CONV_HALO, wq), F32)],
        compiler_params=_params("parallel", "parallel", "arbitrary"), name="a_mlstm",
    )(gate_b, u, u, u, u, u, g_col, g_row, conv_w, conv_w, conv_b.reshape(1, -1), conv_b.reshape(1, -1),
      out_norm_w.reshape(1, -1))
    return _out_proj(y.reshape(t, inner), w_out, x2d, next_w, "a_out")


def _rglru_body(x_ref, z_ref, cw_ref, cb_ref, wa_ref, wx_ref, ba_ref, bx_ref, lam_ref, y_ref,
                ext, xc_scr, a_scr, b_scr, h_scr, *, starts, win, chunk, scan_chunk):
    s = pl.program_id(1)
    ts, width = x_ref.shape

    @pl.when(s == 0)
    def _():
        ext[ts:ts + CONV_HALO, :] = jnp.zeros((CONV_HALO, width), F32)
        h_scr[...] = jnp.zeros_like(h_scr)

    ext[0:CONV_HALO, :] = ext[ts:ts + CONV_HALO, :]
    ext[CONV_HALO:ts + CONV_HALO, :] = x_ref[...]
    for c0 in range(0, width, chunk):
        cs = slice(c0, c0 + chunk)
        xc = cb_ref[:, cs] + cw_ref[0:1, cs] * ext[pl.ds(CONV_HALO - CONV_W + 1, ts), cs]
        for j in range(1, CONV_W):
            xc = xc + cw_ref[j:j + 1, cs] * ext[pl.ds(CONV_HALO - CONV_W + 1 + j, ts), cs]
        xc_scr[:, cs] = xc

    a_scr[...] = jnp.zeros_like(a_scr)
    b_scr[...] = jnp.zeros_like(b_scr)
    for n, a0 in enumerate(starts):
        ws = slice(a0, a0 + win)
        xw = xc_scr[:, ws].astype(BF16)
        a_scr[:, ws] += jnp.dot(xw, wa_ref[n], preferred_element_type=F32)
        b_scr[:, ws] += jnp.dot(xw, wx_ref[n], preferred_element_type=F32)

    for c0 in range(0, width, chunk):
        cs = slice(c0, c0 + chunk)
        r = _sigmoid(a_scr[:, cs] + ba_ref[:, cs])
        ig = _sigmoid(b_scr[:, cs] + bx_ref[:, cs])
        log_a = (-C_EXP) * _softplus(-lam_ref[:, cs]) * r
        a = jnp.exp(log_a)
        a_scr[:, cs] = a
        b_scr[:, cs] = jnp.sqrt(-jnp.tanh(log_a) * (a * a + 1.0)) * (ig * xc_scr[:, cs])

    for c0 in range(0, width, scan_chunk):
        cs = slice(c0, c0 + scan_chunk)

        def row(t, h, cs=cs):
            h = a_scr[pl.ds(t, 1), cs] * h + b_scr[pl.ds(t, 1), cs]
            b_scr[pl.ds(t, 1), cs] = h
            return h

        h_scr[:, cs] = lax.fori_loop(0, ts, row, h_scr[:, cs], unroll=8)

    y_ref[...] = (b_scr[...] * _silu(z_ref[...])).astype(y_ref.dtype)


def _rglru_layer(x2d, hn, next_w, bsz, seq, w_in, conv_w, conv_b, w_a, b_a, w_x, b_x, lam, w_out):
    t, d = x2d.shape
    width = conv_w.shape[-1]
    nblk = w_a.shape[0]
    blk = width // nblk
    assert width % LANES == 0
    win = LANES
    while True:
        starts = tuple(min(n * blk // LANES * LANES, width - win) for n in range(nblk))
        if all(n * blk - a0 + blk <= win for n, a0 in enumerate(starts)):
            break
        win += LANES

    def window_weights(w):
        return jnp.stack([jnp.pad(w[n], ((n * blk - a0, win - blk - (n * blk - a0)),) * 2)
                          for n, a0 in enumerate(starts)]).astype(BF16)

    u = _in_proj(hn, w_in, name="c_in").reshape(bsz, seq, 2 * width)
    ts = _pick(seq, (128, 64, 8))
    chunk = _pick(width, (768, 512, 384, 256, 128))
    scan_chunk = _pick(width, (1792, 1536, 1024, 896, 768, 512, 384, 256, 128))
    full = lambda r: pl.BlockSpec((r, width), lambda b, s: (0, 0))
    wspec = pl.BlockSpec((nblk, win, win), lambda b, s: (0, 0, 0), pipeline_mode=pl.Buffered(1))
    row = lambda v: v.reshape(1, width)
    y = pl.pallas_call(
        functools.partial(_rglru_body, starts=starts, win=win, chunk=chunk, scan_chunk=scan_chunk),
        grid=(bsz, seq // ts),
        in_specs=[pl.BlockSpec((None, ts, width), lambda b, s: (b, s, 0)),
                  pl.BlockSpec((None, ts, width), lambda b, s: (b, s, 1)),
                  full(CONV_W), full(1), wspec, wspec, full(1), full(1), full(1)],
        out_specs=pl.BlockSpec((None, ts, width), lambda b, s: (b, s, 0)),
        out_shape=jax.ShapeDtypeStruct((bsz, seq, width), BF16),
        scratch_shapes=[pltpu.VMEM((ts + CONV_HALO, width), F32), pltpu.VMEM((ts, width), F32),
                        pltpu.VMEM((ts, width), F32), pltpu.VMEM((ts, width), F32), pltpu.VMEM((1, width), F32)],
        compiler_params=_params("parallel", "arbitrary"), name="c_rglru",
    )(u, u, conv_w, row(conv_b), window_weights(w_a), window_weights(w_x), row(b_a), row(b_x), row(lam))
    return _out_proj(y.reshape(t, width), w_out, x2d, next_w, "c_out")


def _dilated_body(sl_ref, *refs, patterns, tq, mix_rows, group):
    n_pat = len(patterns)
    qkv = [refs[5 * g:5 * g + 5] for g in range(n_pat)]
    z_ref, qw_ref, kw_ref, y_ref, knbuf, vbuf, obuf, lbuf = refs[5 * n_pat:]
    first_tile = pl.program_id(1) == 0
    slope = sl_ref[pl.program_id(2)]
    tt, hd = z_ref.shape
    scale = hd ** -0.5
    qi = lax.broadcasted_iota(jnp.int32, (tq, 2 * tq), 0)
    kj = lax.broadcasted_iota(jnp.int32, (tq, 2 * tq), 1)
    steps = qi + tq - kj

    for g, (window, dil) in enumerate(patterns):
        q_ref, kp_ref, kc_ref, vp_ref, vc_ref = qkv[g]
        halo = tq * dil
        knbuf[0:halo, :] = _head_rms(kp_ref[...], kw_ref[...])
        knbuf[halo:halo + tt, :] = _head_rms(kc_ref[...], kw_ref[...])
        vbuf[0:halo, :] = vp_ref[...]
        vbuf[halo:halo + tt, :] = vc_ref[...]
        bias = jnp.where((steps >= 0) & (steps <= window // dil), (-slope) * (steps * dil).astype(F32), NEG)
        bias_first = jnp.where((kj >= tq) | jnp.logical_not(first_tile), bias, NEG)
        subtiles = [(r + halo * j, j == 0) for r in range(dil) for j in range(tt // halo)]
        for i0 in range(0, len(subtiles), group):
            grp = subtiles[i0:i0 + group]
            rows = [(lambda n, st=st: pl.ds(st, n, stride=dil) if dil > 1 else pl.ds(st, n)) for st, _ in grp]
            ss = [_dot_nt(_head_rms(q_ref[rw(tq), :], qw_ref[...]) * scale, knbuf[rw(2 * tq), :])
                  + (bias_first if first else bias) for rw, (_, first) in zip(rows, grp)]
            ms = [jnp.max(s, axis=1, keepdims=True) for s in ss]
            ps = [jnp.exp(s - m) for s, m in zip(ss, ms)]
            dens = [jnp.sum(p, axis=1, keepdims=True) for p in ps]
            for rw, p, den, m in zip(rows, ps, dens, ms):
                obuf[g, rw(tq), :] = _dot(p / den, vbuf[rw(2 * tq), :])
                lbuf[g, rw(tq), :] = jnp.broadcast_to(m + jnp.log(den), (tq, hd))

    for c in range(0, tt, mix_rows):
        cs = slice(c, c + mix_rows)
        ls = [lbuf[g, cs, :] for g in range(n_pat)]
        mx = functools.reduce(jnp.maximum, ls)
        es = [jnp.exp(l - mx) for l in ls]
        tot = functools.reduce(lambda a, b: a + b, es)
        o = sum((e / tot) * obuf[g, cs, :] for g, e in enumerate(es))
        y_ref[cs, :] = (o * _silu(z_ref[cs, :])).astype(y_ref.dtype)


def _dilated_layer(x2d, hn, next_w, bsz, seq, w_in, q_norm_w, k_norm_w, w_out):
    t, d = x2d.shape
    hd = q_norm_w.shape[-1]
    inner = w_out.shape[0]
    heads = inner // hd
    n_pat = len(D_PATTERNS)
    cols = 3 * n_pat * inner + inner
    tq = D_QBLK
    max_dil = max(dil for _, dil in D_PATTERNS)
    tt = tq * max_dil
    assert hd == LANES and seq % tt == 0
    assert all(window // dil <= tq and tt % (tq * dil) == 0 for window, dil in D_PATTERNS)
    slopes = jnp.asarray(2.0 ** (-8.0 * np.arange(1, heads + 1) / heads), dtype=F32)
    u = _in_proj(hn, w_in, group=hd, name="d_in").reshape(cols // hd, bsz, seq, hd)

    in_specs = [pl.BlockSpec(memory_space=pltpu.SMEM)]
    args = [slopes]
    for g, (window, dil) in enumerate(D_PATTERNS):
        halo = tq * dil
        cur = lambda part, g=g: pl.BlockSpec((None, None, tt, hd),
                                             lambda b, s, h: ((3 * g + part) * heads + h, b, s, 0))
        prev = lambda part, g=g, halo=halo: pl.BlockSpec(
            (None, None, halo, hd),
            lambda b, s, h: ((3 * g + part) * heads + h, b, jnp.maximum(s * (tt // halo) - 1, 0), 0))
        in_specs += [cur(0), prev(1), cur(1), prev(2), cur(2)]
        args += [u] * 5
    wspec = pl.BlockSpec((1, hd), lambda b, s, h: (0, 0))
    in_specs += [pl.BlockSpec((None, None, tt, hd), lambda b, s, h: (3 * n_pat * heads + h, b, s, 0)), wspec, wspec]
    args += [u, q_norm_w.reshape(1, hd), k_norm_w.reshape(1, hd)]
    y = pl.pallas_call(
        functools.partial(_dilated_body, patterns=D_PATTERNS, tq=tq, mix_rows=_pick(tt, (256, 128)), group=4),
        grid=(bsz, seq // tt, heads), in_specs=in_specs,
        out_specs=pl.BlockSpec((None, tt, hd), lambda b, s, h: (b, s, h)),
        out_shape=jax.ShapeDtypeStruct((bsz, seq, inner), BF16),
        scratch_shapes=[pltpu.VMEM((2 * tt, hd), F32), pltpu.VMEM((2 * tt, hd), F32),
                        pltpu.VMEM((n_pat, tt, hd), F32), pltpu.VMEM((n_pat, tt, hd), F32)],
        compiler_params=_params("parallel", "parallel", "parallel"), name="d_attn",
    )(*args)
    return _out_proj(y.reshape(t, inner), w_out, x2d, next_w, "d_out")


def _nsa_prep_body(k0_ref, v0_ref, k1_ref, v1_ref, k2_ref, v2_ref, pe_ref, wk_ref, wv_ref, knw_ref,
                   kc_ref, vc_ref, ks_ref, vs_ref, kw_ref, vw_ref, *, stride):
    n_c = kc_ref.shape[0]

    def compress(x_ref, w_ref, pe):
        p1 = jnp.zeros(kc_ref.shape, F32)
        p2 = jnp.zeros(kc_ref.shape, F32)
        for l in range(stride):
            x = x_ref[pl.ds(l, n_c, stride=stride), :]
            p1 = p1 + _dot(x + pe[l:l + 1, :], w_ref[l])
            p2 = p2 + _dot(x + pe[stride + l:stride + l + 1, :], w_ref[stride + l])
        return p1 + pltpu.roll(p2, n_c - 1, 0)

    kc_ref[...] = _head_rms(compress(k0_ref, wk_ref, pe_ref[0]), knw_ref[0:1, :])
    vc_ref[...] = compress(v0_ref, wv_ref, pe_ref[1])

    seq, hd = k1_ref.shape
    pos = lax.broadcasted_iota(jnp.int32, (seq, hd), 0)
    lane = lax.broadcasted_iota(jnp.int32, (seq, hd), 1)
    split_shift = NSA_POS_SPLIT.bit_length() - 1
    pos_hi = ((pos >> split_shift) << split_shift).astype(F32)
    pos_lo = (pos & (NSA_POS_SPLIT - 1)).astype(F32)
    pos_cols = jnp.where(lane < 3, pos_hi, jnp.where(lane < NSA_SEL_OFF, pos_lo, 0.0))
    sel_blk = pos >> (B_SEL_LEN.bit_length() - 1)
    onehot = jnp.where(lane - NSA_SEL_OFF == sel_blk, NEG, 0.0)
    ones = jnp.ones((seq, hd), vs_ref.dtype)
    ks_ref[:, 0:hd] = _head_rms(k1_ref[...], knw_ref[1:2, :]).astype(ks_ref.dtype)
    ks_ref[:, hd:2 * hd] = (pos_cols + onehot).astype(ks_ref.dtype)
    vs_ref[:, 0:hd] = v1_ref[...].astype(vs_ref.dtype)
    vs_ref[:, hd:2 * hd] = ones
    kw_ref[:, 0:hd] = _head_rms(k2_ref[...], knw_ref[2:3, :]).astype(kw_ref.dtype)
    kw_ref[:, hd:2 * hd] = pos_cols.astype(kw_ref.dtype)
    vw_ref[:, 0:hd] = v2_ref[...].astype(vw_ref.dtype)
    vw_ref[:, hd:2 * hd] = ones


def _nsa_body(sl_ref, sp_ref, q_ref, z_ref, g_ref, qw_ref, kc_ref, vc_ref, ks_ref, vs_ref, kw_ref, vw_ref, ov_ref,
              y_ref, q2, acc, m_scr, o_scr, used, *, rep, hd, n_top, n_blk, kb_sel):
    g = pl.program_id(1)
    i = pl.program_id(2)
    tq = q_ref.shape[0]
    n_cmp = kc_ref.shape[0]
    t0 = i * tq
    scale = hd ** -0.5
    rows = lambda r: slice(r * tq, (r + 1) * tq)

    for r in range(rep):
        q2[rows(r), 0:hd] = (_head_rms(q_ref[:, r * hd:(r + 1) * hd], qw_ref[...]) * scale).astype(q2.dtype)
    gts = _sigmoid(g_ref[...])

    def gate(r, branch):
        return gts[:, 3 * r + branch:3 * r + branch + 1]

    t_col = t0 + lax.broadcasted_iota(jnp.int32, (tq, 1), 0)
    lane = lax.broadcasted_iota(jnp.int32, (tq, hd), 1)
    pieces = []
    for r in range(rep):
        pieces.append(jnp.where((lane == 0) | (lane == 3), sp_ref[g, 3 * r],
                                jnp.where((lane == 1) | (lane == 4), sp_ref[g, 3 * r + 1],
                                          jnp.where((lane == 2) | (lane == 5), sp_ref[g, 3 * r + 2], 0.0))))
        q2[rows(r), hd:2 * hd] = pieces[r].astype(q2.dtype)

    n_w = B_WIN + tq
    w_start = pl.multiple_of(jnp.maximum(t0 - B_WIN, 0), tq)
    w_dist = t_col - (w_start + lax.broadcasted_iota(jnp.int32, (tq, n_w), 1))
    w_bias = jnp.where((w_dist >= 0) & (w_dist < B_WIN), 0.0, NEG)
    kblk = kw_ref[pl.ds(w_start, n_w), :]
    vblk = vw_ref[pl.ds(w_start, n_w), :]
    s_all = lax.dot_general(q2[...], kblk, (((1,), (1,)), ((), ())), preferred_element_type=F32)
    half = rep // 2
    for r0 in (0, half):
        ss = [s_all[rows(r), :] + w_bias for r in range(r0, r0 + half)]
        ms = [jnp.max(s, axis=1, keepdims=True) for s in ss]
        ps = [jnp.exp(s - m).astype(BF16) for s, m in zip(ss, ms)]
        for r, p in zip(range(r0, r0 + half), ps):
            res = jnp.dot(p, vblk, preferred_element_type=F32)
            o_scr[rows(r), :] = gate(r, 2) * (res[:, 0:hd] / res[:, hd:2 * hd])

    cmp_end = lax.broadcasted_iota(jnp.int32, (tq, n_cmp), 1) * B_CMP_STRIDE + (B_CMP_LEN - 1)
    dist_ci = t_col - cmp_end
    mask_c = dist_ci >= 0
    dist_c = dist_ci.astype(F32)
    s_all = _dot_nt(q2[:, 0:hd], kc_ref[...])
    ss = [jnp.where(mask_c, s_all[rows(r), :] - sl_ref[g, r] * dist_c, NEG) for r in range(rep)]
    ms = [jnp.max(s, axis=1, keepdims=True) for s in ss]
    es = [jnp.where(mask_c, jnp.exp(s - m), 0.0) for s, m in zip(ss, ms)]
    dens = [jnp.sum(e, axis=1, keepdims=True) for e in es]
    ps = [e / jnp.where(den > 0, den, 1.0) for e, den in zip(es, dens)]
    psum = functools.reduce(lambda a, b: a + b, ps)
    o_c = _dot(jnp.concatenate([p.astype(BF16) for p in ps], axis=0), vc_ref[...])
    for r in range(rep):
        o_scr[rows(r), :] += gate(r, 0) * o_c[rows(r), :]

    imp = jnp.dot(psum, ov_ref[...], preferred_element_type=F32, precision=lax.Precision.HIGHEST)
    blk = lane - NSA_SEL_OFF
    in_range = (blk >= 0) & (blk < n_blk)
    cur = t_col >> (B_SEL_LEN.bit_length() - 1)
    forced = (blk == 0) | (blk == cur) | (blk == cur - 1)
    imp = jnp.where(forced, imp + FORCE, imp)
    imp = jnp.where(in_range & (blk * B_SEL_LEN <= t_col), imp, NEG)
    imp_t = imp.T
    n_grp = -(-(NSA_SEL_OFF + n_blk) // SUBLANES)
    grp = [imp_t[SUBLANES * k:SUBLANES * (k + 1), :] for k in range(n_grp)]
    cnt = [jnp.zeros((SUBLANES, tq), F32) for _ in range(n_grp)]
    sub = lax.broadcasted_iota(jnp.int32, (SUBLANES, tq), 0)
    for mb in range(n_blk):
        slot = NSA_SEL_OFF + mb
        row = imp_t[slot:slot + 1, :]
        for k in range(n_grp):
            if SUBLANES * k > slot:
                beats = row >= grp[k]
            elif SUBLANES * (k + 1) <= slot:
                beats = row > grp[k]
            else:
                beats = (row > grp[k]) | ((row == grp[k]) & (sub > slot - SUBLANES * k))
            cnt[k] = cnt[k] + jnp.where(beats, 1.0, 0.0)
    cnt_t = jnp.concatenate(cnt + [jnp.zeros((hd - SUBLANES * n_grp, tq), F32)], axis=0)
    not_sel = jnp.where((cnt_t < n_top) & (imp_t > NEG / 2), 0.0, 1.0).T

    sel_bf = jnp.where(in_range, 1.0 - not_sel, 0.0).astype(BF16)
    per_slot = jnp.dot(jnp.ones((SUBLANES, tq), BF16), sel_bf, preferred_element_type=F32)
    slot_io = lax.broadcasted_iota(jnp.int32, (hd, hd), 0) - NSA_SEL_OFF
    kb_io = lax.broadcasted_iota(jnp.int32, (hd, hd), 1)
    kb_shift = (kb_sel // B_SEL_LEN).bit_length() - 1
    group = jnp.where((slot_io >= 0) & ((slot_io >> kb_shift) == kb_io), 1.0, 0.0).astype(BF16)
    per_kb = jnp.dot(per_slot.astype(BF16), group, preferred_element_type=F32)
    lane_s = lax.broadcasted_iota(jnp.int32, (SUBLANES, hd), 1)
    for kb in range(used.shape[0]):
        used[kb] = jnp.max(jnp.where(lane_s == kb, per_kb, 0.0)).astype(jnp.int32)

    for r in range(rep):
        q2[rows(r), hd:2 * hd] = jnp.where(in_range, not_sel, pieces[r]).astype(q2.dtype)

    acc[...] = jnp.zeros_like(acc)
    m_scr[...] = jnp.full(m_scr.shape, NEG, F32)
    n_chunk = kb_sel // hd

    def sel_block(start, bias):
        kblk = ks_ref[pl.ds(start, kb_sel), :]
        vblk = vs_ref[pl.ds(start, kb_sel), :]
        s_all = lax.dot_general(q2[...], kblk, (((1,), (1,)), ((), ())), preferred_element_type=F32)
        ss = [s_all[rows(r), :] if bias is None else s_all[rows(r), :] + bias for r in range(rep)]
        mxs = [functools.reduce(jnp.maximum, [s[:, c * hd:(c + 1) * hd] for c in range(n_chunk)]) for s in ss]
        m_olds = [m_scr[rows(r), :] for r in range(rep)]
        m_news = [jnp.maximum(m_old, jnp.max(mx, axis=1, keepdims=True))
                  for m_old, mx in zip(m_olds, mxs)]
        alphas = [jnp.exp(m_old - m_new) for m_old, m_new in zip(m_olds, m_news)]
        ps = [jnp.exp(s - jnp.concatenate([m_new] * n_chunk, axis=1)).astype(BF16) for s, m_new in zip(ss, m_news)]
        for r in range(rep):
            acc[rows(r), :] = (jnp.concatenate([alphas[r], alphas[r]], axis=1) * acc[rows(r), :]
                               + jnp.dot(ps[r], vblk, preferred_element_type=F32))
            m_scr[rows(r), :] = m_news[r]

    def sel_step(kb, carry):
        @pl.when(used[kb] > 0)
        def _():
            sel_block(pl.multiple_of(kb * kb_sel, kb_sel), None)

        return carry

    n_full = lax.div(t0, kb_sel)
    lax.fori_loop(0, n_full, sel_step, 0)
    d_start = pl.multiple_of(n_full * kb_sel, kb_sel)
    d_pos = d_start + lax.broadcasted_iota(jnp.int32, (tq, kb_sel), 1)
    sel_block(d_start, jnp.where(d_pos <= t_col, 0.0, NEG))
    for r in range(rep):
        o_scr[rows(r), :] += gate(r, 1) * (acc[rows(r), 0:hd] / acc[rows(r), hd:2 * hd])

    for r in range(rep):
        hs = slice(r * hd, (r + 1) * hd)
        y_ref[:, hs] = (o_scr[rows(r), :] * _silu(z_ref[:, hs])).astype(y_ref.dtype)


def _nsa_layer(x2d, hn, next_w, bsz, seq, w_in, cmp_pe, cmp_wk, cmp_wv, q_norm_w, k_norm_w, w_out):
    t, d = x2d.shape
    hd = q_norm_w.shape[-1]
    inner = w_out.shape[0]
    heads = inner // hd
    rep = heads // B_KV
    kvw = B_KV * hd
    n_main = 2 * inner + 6 * kvw
    assert B_CMP_LEN == 2 * B_CMP_STRIDE and seq % B_CMP_STRIDE == 0 and 3 * heads <= LANES
    n_cmp = seq // B_CMP_STRIDE
    n_blk = seq // B_SEL_LEN
    n_top = min(B_N_SEL, n_blk)
    tq = 256
    kb_sel = 512
    assert hd == LANES and NSA_SEL_OFF + n_blk <= hd and NSA_POS_SPLIT == hd and rep % 2 == 0
    assert kb_sel % tq == 0 and seq % kb_sel == 0 and B_WIN % tq == 0 and seq >= B_WIN + tq

    u = _in_proj(hn, w_in, n=n_main, transposed=True, name="b_in").reshape(bsz, seq, n_main)
    gts = _in_proj(hn, w_in, n=LANES, col0=n_main, transposed=True, name="b_gates")
    gts = gts[:, :3 * heads].reshape(bsz, seq, B_KV, 3 * rep)
    gts = jnp.transpose(gts, (0, 2, 1, 3))

    kv0 = 2 * inner // hd
    kvspec = lambda br, kv: pl.BlockSpec((None, seq, hd), lambda b, g: (b, 0, kv0 + (br * 2 + kv) * B_KV + g))
    cspec = pl.BlockSpec((None, None, n_cmp, hd), lambda b, g: (b, g, 0, 0))
    sspec = pl.BlockSpec((None, None, seq, 2 * hd), lambda b, g: (b, g, 0, 0))
    wspec = pl.BlockSpec((B_CMP_LEN, hd, hd), lambda b, g: (0, 0, 0))
    c_sh = jax.ShapeDtypeStruct((bsz, B_KV, n_cmp, hd), F32)
    s_sh = jax.ShapeDtypeStruct((bsz, B_KV, seq, 2 * hd), BF16)
    kc, vc, ks, vs, kw, vw = pl.pallas_call(
        functools.partial(_nsa_prep_body, stride=B_CMP_STRIDE),
        grid=(bsz, B_KV),
        in_specs=[kvspec(0, 0), kvspec(0, 1), kvspec(1, 0), kvspec(1, 1), kvspec(2, 0), kvspec(2, 1),
                  pl.BlockSpec((2, B_CMP_LEN, hd), lambda b, g: (0, 0, 0)), wspec, wspec,
                  pl.BlockSpec((3, hd), lambda b, g: (0, 0))],
        out_specs=[cspec, cspec, sspec, sspec, sspec, sspec],
        out_shape=[c_sh, c_sh, s_sh, s_sh, s_sh, s_sh],
        compiler_params=_params("parallel", "parallel"), name="b_prep",
    )(u, u, u, u, u, u, cmp_pe, cmp_wk.reshape(B_CMP_LEN, hd, hd).astype(BF16),
      cmp_wv.reshape(B_CMP_LEN, hd, hd).astype(BF16), k_norm_w)

    cmp_start = np.arange(n_cmp) * B_CMP_STRIDE
    sel_start = np.arange(n_blk) * B_SEL_LEN
    ov = np.clip(np.minimum(cmp_start[:, None] + B_CMP_LEN, sel_start[None, :] + B_SEL_LEN)
                 - np.maximum(cmp_start[:, None], sel_start[None, :]), 0, None) / B_CMP_LEN
    ov[cmp_start + B_CMP_LEN > seq] = 0.0
    ov = np.pad(ov, ((0, 0), (NSA_SEL_OFF, hd - NSA_SEL_OFF - n_blk)))
    slopes_np = (2.0 ** (-8.0 * np.arange(1, heads + 1) / heads)).astype(np.float32)
    slopes = jnp.asarray(slopes_np).reshape(B_KV, rep)
    to_bf16 = lambda v: v.astype(ml_dtypes.bfloat16).astype(np.float32)
    s_hi = to_bf16(slopes_np)
    s_mid = to_bf16(slopes_np - s_hi)
    s_lo = to_bf16(slopes_np - s_hi - s_mid)
    pieces = jnp.asarray(np.stack([s_hi, s_mid, s_lo], axis=-1).reshape(B_KV, 3 * rep))

    rw = rep * hd
    cspec3 = pl.BlockSpec((None, None, n_cmp, hd), lambda b, g, i: (b, g, 0, 0))
    sspec3 = pl.BlockSpec((None, None, seq, 2 * hd), lambda b, g, i: (b, g, 0, 0))
    y = pl.pallas_call(
        functools.partial(_nsa_body, rep=rep, hd=hd, n_top=n_top, n_blk=n_blk, kb_sel=kb_sel),
        grid=(bsz, B_KV, seq // tq),
        in_specs=[pl.BlockSpec(memory_space=pltpu.SMEM), pl.BlockSpec(memory_space=pltpu.SMEM),
                  pl.BlockSpec((None, tq, rw), lambda b, g, i: (b, i, g)),
                  pl.BlockSpec((None, tq, rw), lambda b, g, i: (b, i, inner // rw + g)),
                  pl.BlockSpec((None, None, tq, 3 * rep), lambda b, g, i: (b, g, i, 0)),
                  pl.BlockSpec((1, hd), lambda b, g, i: (0, 0)),
                  cspec3, cspec3, sspec3, sspec3, sspec3, sspec3,
                  pl.BlockSpec((n_cmp, hd), lambda b, g, i: (0, 0))],
        out_specs=pl.BlockSpec((None, tq, rw), lambda b, g, i: (b, i, g)),
        out_shape=jax.ShapeDtypeStruct((bsz, seq, inner), BF16),
        scratch_shapes=[pltpu.VMEM((rep * tq, 2 * hd), BF16), pltpu.VMEM((rep * tq, 2 * hd), F32),
                        pltpu.VMEM((rep * tq, hd), F32), pltpu.VMEM((rep * tq, hd), F32),
                        pltpu.SMEM((seq // kb_sel,), jnp.int32)],
        compiler_params=_params("parallel", "parallel", "arbitrary"), name="b_attn",
    )(slopes, pieces, u, u, gts, q_norm_w.reshape(1, hd), kc, vc, ks, vs, kw, vw, jnp.asarray(ov, dtype=F32))
    return _out_proj(y.reshape(t, inner), w_out, x2d, next_w, "b_out")


def kernel(x, norm_w, a_w_in, a_conv_w, a_conv_b, a_gate_b, a_out_norm_w, a_w_out, b_w_in, b_cmp_pe, b_cmp_wk, b_cmp_wv, b_q_norm_w, b_k_norm_w, b_w_out, c_w_in, c_conv_w, c_conv_b, c_w_a, c_b_a, c_w_x, c_b_x, c_lambda, c_w_out, d_w_in, d_q_norm_w, d_k_norm_w, d_w_out):
    bsz, seq, d = x.shape
    depth = norm_w.shape[0]
    x2d = x.reshape(bsz * seq, d)
    hn = (_rmsnorm(x2d, norm_w[0]), None)
    for layer in range(depth):
        kind, j = layer % 4, layer // 4
        next_w = norm_w[layer + 1] if layer + 1 < depth else None
        if kind == 0:
            x2d, hn = _mlstm_layer(x2d, hn, next_w, bsz, seq, a_w_in[j], a_conv_w[j], a_conv_b[j], a_gate_b[j],
                                   a_out_norm_w[j], a_w_out[j])
        elif kind == 1:
            x2d, hn = _nsa_layer(x2d, hn, next_w, bsz, seq, b_w_in[j], b_cmp_pe[j], b_cmp_wk[j], b_cmp_wv[j],
                                 b_q_norm_w[j], b_k_norm_w[j], b_w_out[j])
        elif kind == 2:
            x2d, hn = _rglru_layer(x2d, hn, next_w, bsz, seq, c_w_in[j], c_conv_w[j], c_conv_b[j], c_w_a[j], c_b_a[j],
                                   c_w_x[j], c_b_x[j], c_lambda[j], c_w_out[j])
        else:
            x2d, hn = _dilated_layer(x2d, hn, next_w, bsz, seq, d_w_in[j], d_q_norm_w[j], d_k_norm_w[j], d_w_out[j])
    return x2d.reshape(bsz, seq, d)
```
